```python
import math
import jax
import jax.numpy as jnp
from jax import lax
import numpy as np

D_MODEL = 1024
BATCH = 8
SEQ = 2048
DEPTH = 2

GRID_W = 64
CTX_LEN = 256

HG_HEADS = 4
HG_DK = 128
HG_DV = 128
RET_HEADS = 4
RET_DK = 64
RET_DV = 128
GQA_HEADS = 4
GQA_KV_HEADS = 2
GQA_HD = 128
DIFF_HEADS = 4
DIFF_HD = 64

N_EXPERTS = 32
TOP_K = 4
D_EXPERT = D_MODEL
SWIGLU_LIMIT = 7.0
SWIGLU_ALPHA = 1.702

CHUNK = 64
Q_BLOCK = 128
MOE_BLOCK = 256
ROPE_THETA = 10000.0
NORM_EPS = 1e-6

N_REC_LAYERS = (DEPTH + 1) // 2
N_ATT_LAYERS = DEPTH // 2

HG_W = HG_HEADS * HG_DK
REC_SPLITS = (HG_W, HG_W, HG_W, HG_HEADS * HG_DV, HG_HEADS * HG_DV,
              RET_HEADS * RET_DK, RET_HEADS * RET_DK, RET_HEADS * RET_DV, RET_HEADS * RET_DV)
REC_IN = sum(REC_SPLITS)
REC_MIX = HG_HEADS * HG_DV + RET_HEADS * RET_DV
ATT_SPLITS = (GQA_HEADS * GQA_HD, GQA_KV_HEADS * GQA_HD, GQA_KV_HEADS * GQA_HD,
              DIFF_HEADS * 2 * DIFF_HD, DIFF_HEADS * 2 * DIFF_HD, DIFF_HEADS * 2 * DIFF_HD)
ATT_IN = sum(ATT_SPLITS)
ATT_MIX = GQA_HEADS * GQA_HD + DIFF_HEADS * 2 * DIFF_HD

kernel_name = 'hybrid_flow_backbone'

F32 = jnp.float32


def rms_norm(x, gain):
    xf = x.astype(F32)
    y = xf * lax.rsqrt(jnp.mean(xf * xf, axis=-1, keepdims=True) + NORM_EPS)
    return (y * gain.astype(F32)).astype(x.dtype)


def head_group_norm(x, gain):
    xf = x.astype(F32)
    xc = xf - jnp.mean(xf, axis=-1, keepdims=True)
    y = xc * lax.rsqrt(jnp.mean(xc * xc, axis=-1, keepdims=True) + NORM_EPS)
    return (y * gain.astype(F32)).astype(x.dtype)


def _split(x, sizes):
    cuts = [int(s) for s in np.cumsum(sizes)[:-1]]
    return jnp.split(x, cuts, axis=-1)


def _heads(x, n_heads):
    b, t, _ = x.shape
    return x.reshape(b, t, n_heads, -1).transpose(0, 2, 1, 3)


def _unheads(x):
    b, h, t, d = x.shape
    return x.transpose(0, 2, 1, 3).reshape(b, t, h * d)


def axial_rope(n_tok, head_dim):
    rows = n_tok // GRID_W
    row = jnp.broadcast_to(jnp.arange(rows, dtype=F32)[:, None], (rows, GRID_W)).reshape(-1)
    col = jnp.broadcast_to(jnp.arange(GRID_W, dtype=F32)[None, :], (rows, GRID_W)).reshape(-1)
    axis_dim = head_dim // 2
    inv_freq = ROPE_THETA ** (-jnp.arange(0, axis_dim, 2, dtype=F32) / axis_dim)
    ang = jnp.concatenate([row[:, None] * inv_freq, col[:, None] * inv_freq], axis=-1)
    return jnp.cos(ang), jnp.sin(ang)


def apply_rope(x, cos, sin):
    xf = x.astype(F32).reshape(*x.shape[:-1], -1, 2)
    x1, x2 = xf[..., 0], xf[..., 1]
    y = jnp.stack([x1 * cos - x2 * sin, x1 * sin + x2 * cos], axis=-1)
    return y.reshape(x.shape).astype(x.dtype)


def modulation(cond, w_ada, b_ada):
    m = jax.nn.silu(cond) @ w_ada + b_ada
    return jnp.split(m[..., None, :], 6, axis=-1)


def chunk_recurrence(q, k, v, log_f, s0):
    bsz, h, t, _ = q.shape
    dv = v.shape[-1]
    n = t // CHUNK

    def rs(a):
        return a.astype(F32).reshape(a.shape[0], a.shape[1], n, CHUNK, a.shape[-1])

    q, k, v, log_f = rs(q), rs(k), rs(v), rs(log_f)
    b = jnp.cumsum(log_f, axis=3)
    b_last = b[:, :, :, -1:, :]
    q_in = q * jnp.exp(b)
    k_in = k * jnp.exp(-b)
    k_state = k * jnp.exp(b_last - b)
    mask = jnp.tril(jnp.ones((CHUNK, CHUNK), dtype=bool))
    att = jnp.where(mask, jnp.einsum('bhncd,bhnsd->bhncs', q_in, k_in), 0.0)
    o_intra = jnp.einsum('bhncs,bhnsv->bhncv', att, v)
    kv = jnp.einsum('bhnsd,bhnsv->nbhdv', k_state, v)
    decay = jnp.moveaxis(jnp.exp(b_last[:, :, :, 0, :]), 2, 0)

    def step(s, inp):
        d, kv_n = inp
        return s * d[..., None] + kv_n, s

    s_last, s_prev = lax.scan(step, s0.astype(F32), (decay, kv))
    o_inter = jnp.einsum('bhncd,nbhdv->bhncv', q_in, s_prev)
    return (o_intra + o_inter).reshape(bsz, h, t, dv), s_last


def _directional(q, k, v, log_f, s0, reverse):
    if reverse:
        fl = lambda a: jnp.flip(a, axis=2)
        o, s = chunk_recurrence(fl(q), fl(k), fl(v), fl(log_f), s0)
        return jnp.flip(o, axis=2), s
    return chunk_recurrence(q, k, v, log_f, s0)


def _to_blocks(x):
    *lead, t, d = x.shape
    return jnp.moveaxis(x.reshape(*lead, t // Q_BLOCK, Q_BLOCK, d), -3, 0)


def _from_blocks(y):
    y = jnp.moveaxis(y, 0, -3)
    *lead, nb, qb, d = y.shape
    return y.reshape(*lead, nb * qb, d)


def gqa_attend(q, k, v):
    scale = q.shape[-1] ** -0.5

    def block(qb):
        s = jnp.einsum('bkgqd,bksd->bkgqs', qb, k).astype(F32) * scale
        p = jax.nn.softmax(s, axis=-1).astype(v.dtype)
        return jnp.einsum('bkgqs,bksd->bkgqd', p, v)

    return _from_blocks(lax.map(block, _to_blocks(q)))


def diff_attend(q, k, v, lam):
    scale = q.shape[-1] ** -0.5

    def block(qb):
        s = jnp.einsum('bhiqd,bhisd->bhiqs', qb, k).astype(F32) * scale
        p = jax.nn.softmax(s, axis=-1)
        a = (p[:, :, 0] - lam * p[:, :, 1]).astype(v.dtype)
        return jnp.einsum('bhqs,bhsv->bhqv', a, v)

    return _from_blocks(lax.map(block, _to_blocks(q)))


def recurrent_mixer(a_lat, a_ctx, w_in, lb, w_out, hg_gain, ret_gain, rope_r, need_ctx):
    log_gamma = jnp.log(1.0 - 2.0 ** (-5.0 - jnp.arange(RET_HEADS, dtype=F32)))
    log_gammas = (log_gamma, log_gamma[::-1])

    def prep(a, rope):
        hq, hf_f, hf_b, hi, hg, rq, rk, rv, rg = _split(a @ w_in, REC_SPLITS)
        q = jax.nn.silu(_heads(hq, HG_HEADS).astype(F32)) * HG_DK ** -0.5
        f = tuple(_heads(lb[d] + (1.0 - lb[d]) * jax.nn.sigmoid(z.astype(F32)), HG_HEADS)
                  for d, z in enumerate((hf_f, hf_b)))
        i = _heads(hi, HG_HEADS)
        rq_h = _heads(rq, RET_HEADS)
        rk_h = _heads(rk, RET_HEADS) * RET_DK ** -0.5
        if rope is not None:
            rq_h, rk_h = apply_rope(rq_h, *rope), apply_rope(rk_h, *rope)
        return q, f, i, hg, rq_h, rk_h, _heads(rv, RET_HEADS), rg

    cq, cf, ci, cg, crq, crk, crv, crg = prep(a_ctx, None)
    lq, lf, li, lg, lrq, lrk, lrv, lrg = prep(a_lat, rope_r)
    bsz, n_ctx, n_lat = a_lat.shape[0], a_ctx.shape[1], a_lat.shape[1]
    hg_c = hg_l = ret_c = ret_l = 0.0
    for d, rev in enumerate((False, True)):
        s0 = jnp.zeros((bsz, HG_HEADS, HG_DK, HG_DV), F32)
        o, s = _directional(cq, 1.0 - cf[d], ci, jnp.log(cf[d]), s0, rev)
        if need_ctx:
            hg_c = hg_c + o
        o, _ = _directional(lq, 1.0 - lf[d], li, jnp.log(lf[d]), s, rev)
        hg_l = hg_l + o
        s0 = jnp.zeros((bsz, RET_HEADS, RET_DK, RET_DV), F32)
        dec = log_gammas[d][None, :, None, None]
        o, s = _directional(crq, crk, crv, jnp.broadcast_to(dec, (1, RET_HEADS, n_ctx, 1)), s0, rev)
        if need_ctx:
            ret_c = ret_c + o
        o, _ = _directional(lrq, lrk, lrv, jnp.broadcast_to(dec, (1, RET_HEADS, n_lat, 1)), s, rev)
        ret_l = ret_l + o

    def merge(o_hg, o_ret, g_hg, g_ret, dtype):
        y_hg = rms_norm(o_hg, hg_gain) * jax.nn.silu(_heads(g_hg, HG_HEADS).astype(F32))
        y_ret = head_group_norm(o_ret, ret_gain) * jax.nn.silu(_heads(g_ret, RET_HEADS).astype(F32))
        y = jnp.concatenate([_unheads(y_hg), _unheads(y_ret)], axis=-1).astype(dtype)
        return y @ w_out

    y_lat = merge(hg_l, ret_l, lg, lrg, a_lat.dtype)
    y_ctx = merge(hg_c, ret_c, cg, crg, a_ctx.dtype) if need_ctx else None
    return y_lat, y_ctx


def attention_mixer(a_lat, a_ctx, w_in, w_out, q_gain, k_gain, dq_gain, dk_gain, lam_params,
                    diff_gain, lam_init, rope_g, rope_d, need_ctx):
    lp = lam_params.astype(F32)
    lam = jnp.exp(jnp.sum(lp[0] * lp[1])) - jnp.exp(jnp.sum(lp[2] * lp[3])) + lam_init

    def prep(a, ropes):
        gq, gk, gv, dq, dk, dv = _split(a @ w_in, ATT_SPLITS)
        b, t, _ = a.shape
        gq = rms_norm(_heads(gq, GQA_HEADS), q_gain)
        gk = rms_norm(_heads(gk, GQA_KV_HEADS), k_gain)
        dq = rms_norm(_heads(dq, 2 * DIFF_HEADS), dq_gain)
        dk = rms_norm(_heads(dk, 2 * DIFF_HEADS), dk_gain)
        if ropes is not None:
            (cg, sg), (cd, sd) = ropes
            gq, gk = apply_rope(gq, cg, sg), apply_rope(gk, cg, sg)
            dq, dk = apply_rope(dq, cd, sd), apply_rope(dk, cd, sd)
        gq = gq.reshape(b, GQA_KV_HEADS, GQA_HEADS // GQA_KV_HEADS, t, GQA_HD)
        dq = dq.reshape(b, DIFF_HEADS, 2, t, DIFF_HD)
        dk = dk.reshape(b, DIFF_HEADS, 2, t, DIFF_HD)
        return gq, gk, _heads(gv, GQA_KV_HEADS), dq, dk, _heads(dv, DIFF_HEADS)

    cgq, cgk, cgv, cdq, cdk, cdv = prep(a_ctx, None)
    lgq, lgk, lgv, ldq, ldk, ldv = prep(a_lat, (rope_g, rope_d))
    cat = lambda u, w: jnp.concatenate([u, w], axis=-2)

    def merge(o_g, o_d, dtype):
        b, _, _, t, _ = o_g.shape
        o_g = o_g.reshape(b, GQA_HEADS, t, GQA_HD)
        o_d = rms_norm(o_d, diff_gain) * (1.0 - lam_init)
        y = jnp.concatenate([_unheads(o_g), _unheads(o_d)], axis=-1).astype(dtype)
        return y @ w_out

    y_lat = merge(gqa_attend(lgq, cat(cgk, lgk), cat(cgv, lgv)),
                  diff_attend(ldq, cat(cdk, ldk), cat(cdv, ldv), lam), a_lat.dtype)
    y_ctx = merge(gqa_attend(cgq, cgk, cgv), diff_attend(cdq, cdk, cdv, lam), a_ctx.dtype) if need_ctx else None
    return y_lat, y_ctx


def clamped_swiglu(gu):
    gate, up = jnp.split(gu, 2, axis=-1)
    gate = jnp.minimum(gate, SWIGLU_LIMIT)
    up = jnp.clip(up, -SWIGLU_LIMIT, SWIGLU_LIMIT)
    return (up + 1.0) * gate * jax.nn.sigmoid(SWIGLU_ALPHA * gate)


def moe(xf, w_router, b_router, w_gu, b_gu, w_down, b_down):
    n, d = xf.shape
    logits = (xf @ w_router + b_router).astype(F32)
    top_val, top_idx = lax.top_k(logits, TOP_K)
    gates = jax.nn.softmax(top_val, axis=-1).reshape(-1)
    flat_e = top_idx.reshape(-1)
    n_assign = n * TOP_K
    order = jnp.argsort(flat_e)
    sorted_e = flat_e[order]
    tok = order // TOP_K
    counts = jnp.zeros((N_EXPERTS,), jnp.int32).at[flat_e].add(1)
    padded = (counts + MOE_BLOCK - 1) // MOE_BLOCK * MOE_BLOCK
    pad_end = jnp.cumsum(padded)
    pad_start = pad_end - padded
    start = jnp.cumsum(counts) - counts
    dest = pad_start[sorted_e] + jnp.arange(n_assign, dtype=jnp.int32) - start[sorted_e]
    n_blocks = -(-n_assign // MOE_BLOCK) + N_EXPERTS
    row_tok = jnp.full((n_blocks * MOE_BLOCK,), n, jnp.int32).at[dest].set(tok)
    block_e = jnp.minimum(jnp.searchsorted(pad_end, jnp.arange(n_blocks, dtype=jnp.int32) * MOE_BLOCK,
                                           side='right'), N_EXPERTS - 1)
    x_rows = jnp.concatenate([xf, jnp.zeros((1, d), xf.dtype)], axis=0)[row_tok]
    x_rows = x_rows.reshape(n_blocks, MOE_BLOCK, d)

    def expert_block(args):
        xb, e = args
        hb = clamped_swiglu(xb @ w_gu[e] + b_gu[e])
        return hb @ w_down[e] + b_down[e]

    y_rows = lax.map(expert_block, (x_rows, block_e)).reshape(n_blocks * MOE_BLOCK, d)
    y_assign = y_rows[dest] * gates[order][:, None].astype(y_rows.dtype)
    return jax.ops.segment_sum(y_assign, tok, num_segments=n)


def setup_inputs(seed: int = 0) -> dict:
    key = jax.random.key(seed)
    k = jax.random.split(key, 27)
    nrm = lambda i, shape, scale: jax.random.normal(k[i], shape, F32) * scale
    gain = lambda i, shape: 1.0 + 0.05 * jax.random.normal(k[i], shape, F32)
    D = D_MODEL
    return {
        'x': nrm(0, (BATCH, SEQ, D), 1.0),
        'c': nrm(1, (BATCH, D), 1.0),
        'ctx': nrm(2, (BATCH, CTX_LEN, D), 1.0),
        'c_ctx': nrm(3, (D,), 1.0),
        'norm_mix': gain(4, (DEPTH, D)),
        'norm_ffn': gain(5, (DEPTH, D)),
        'w_ada': nrm(6, (DEPTH, D, 6 * D), 0.5 * D ** -0.5),
        'b_ada': nrm(7, (DEPTH, 6 * D), 0.01),
        'rec_w_in': nrm(8, (N_REC_LAYERS, D, REC_IN), D ** -0.5),
        'rec_lb_logits': nrm(9, (DEPTH + 1, 2, HG_W), 0.5),
        'rec_w_out': nrm(10, (N_REC_LAYERS, REC_MIX, D), REC_MIX ** -0.5),
        'rec_hg_gain': gain(11, (N_REC_LAYERS, HG_DV)),
        'rec_ret_gain': gain(12, (N_REC_LAYERS, RET_DV)),
        'att_w_in': nrm(13, (N_ATT_LAYERS, D, ATT_IN), D ** -0.5),
        'att_w_out': nrm(14, (N_ATT_LAYERS, ATT_MIX, D), ATT_MIX ** -0.5),
        'att_q_gain': gain(15, (N_ATT_LAYERS, GQA_HD)),
        'att_k_gain': gain(16, (N_ATT_LAYERS, GQA_HD)),
        'diff_q_gain': gain(17, (N_ATT_LAYERS, DIFF_HD)),
        'diff_k_gain': gain(18, (N_ATT_LAYERS, DIFF_HD)),
        'diff_lambda': nrm(19, (N_ATT_LAYERS, 4, DIFF_HD), 0.1),
        'diff_gain': gain(20, (N_ATT_LAYERS, 2 * DIFF_HD)),
        'w_router': nrm(21, (DEPTH, D, N_EXPERTS), D ** -0.5),
        'b_router': nrm(22, (DEPTH, N_EXPERTS), 0.01),
        'w_gu': nrm(23, (DEPTH, N_EXPERTS, D, 2 * D_EXPERT), D ** -0.5),
        'b_gu': nrm(24, (DEPTH, N_EXPERTS, 2 * D_EXPERT), 0.01),
        'w_down': nrm(25, (DEPTH, N_EXPERTS, D_EXPERT, D), D_EXPERT ** -0.5),
        'b_down': nrm(26, (DEPTH, N_EXPERTS, D), 0.01),
    }


def reference(x, c, ctx, c_ctx, norm_mix, norm_ffn, w_ada, b_ada, rec_w_in, rec_lb_logits, rec_w_out,
              rec_hg_gain, rec_ret_gain, att_w_in, att_w_out, att_q_gain, att_k_gain, diff_q_gain,
              diff_k_gain, diff_lambda, diff_gain, w_router, b_router, w_gu, b_gu, w_down, b_down):
    d = x.shape[-1]
    n_lat = x.shape[1]
    ropes = {hd: axial_rope(n_lat, hd) for hd in sorted({GQA_HD, DIFF_HD, RET_DK})}
    lb_all = jnp.cumsum(jax.nn.softmax(rec_lb_logits.astype(F32), axis=0), axis=0)
    h_lat, h_ctx = x, ctx
    for l in range(DEPTH):
        need_ctx = l < DEPTH - 1
        sh1, sc1, g1, sh2, sc2, g2 = modulation(c, w_ada[l], b_ada[l])
        csh1, csc1, cg1, csh2, csc2, cg2 = modulation(c_ctx, w_ada[l], b_ada[l])
        a_lat = rms_norm(h_lat, norm_mix[l]) * (1.0 + sc1) + sh1
        a_ctx = rms_norm(h_ctx, norm_mix[l]) * (1.0 + csc1) + csh1
        j = l // 2
        if l % 2 == 0:
            m_lat, m_ctx = recurrent_mixer(a_lat, a_ctx, rec_w_in[j], lb_all[l], rec_w_out[j],
                                           rec_hg_gain[j], rec_ret_gain[j], ropes[RET_DK], need_ctx)
        else:
            m_lat, m_ctx = attention_mixer(a_lat, a_ctx, att_w_in[j], att_w_out[j], att_q_gain[j],
                                           att_k_gain[j], diff_q_gain[j], diff_k_gain[j], diff_lambda[j],
                                           diff_gain[j], 0.8 - 0.6 * math.exp(-0.3 * l),
                                           ropes[GQA_HD], ropes[DIFF_HD], need_ctx)
        h_lat = h_lat + g1 * m_lat
        f_lat = rms_norm(h_lat, norm_ffn[l]) * (1.0 + sc2) + sh2
        moe_args = (w_router[l], b_router[l], w_gu[l], b_gu[l], w_down[l], b_down[l])
        if need_ctx:
            h_ctx = h_ctx + cg1 * m_ctx
            f_ctx = rms_norm(h_ctx, norm_ffn[l]) * (1.0 + csc2) + csh2
            n_l = f_lat.shape[0] * f_lat.shape[1]
            y = moe(jnp.concatenate([f_lat.reshape(-1, d), f_ctx.reshape(-1, d)], axis=0), *moe_args)
            h_lat = h_lat + g2 * y[:n_l].reshape(f_lat.shape)
            h_ctx = h_ctx + cg2 * y[n_l:].reshape(f_ctx.shape)
        else:
            h_lat = h_lat + g2 * moe(f_lat.reshape(-1, d), *moe_args).reshape(f_lat.shape)
    return h_lat
```

```python
import functools
import math

import numpy as np
import jax
import jax.numpy as jnp
from jax import lax
from jax.experimental import pallas as pl
from jax.experimental.pallas import tpu as pltpu

F32 = jnp.float32
BF16 = jnp.bfloat16
I32 = jnp.int32

GRID_W = 64
HG_HEADS, HG_DK, HG_DV = 4, 128, 128
RET_HEADS, RET_DK, RET_DV = 4, 64, 128
GQA_HEADS, GQA_KV_HEADS, GQA_HD = 4, 2, 128
DIFF_HEADS, DIFF_HD = 4, 64
N_EXPERTS, TOP_K = 32, 4
SWIGLU_LIMIT, SWIGLU_ALPHA = 7.0, 1.702
CHUNK = 64
ROPE_THETA = 10000.0
NORM_EPS = 1e-6

LANES = 128
E_PAD = LANES
NEG_BIG = -1e30
V7X_VMEM_LIMIT = 56 * 1024 * 1024
HALF = 512

_NT = (((1,), (1,)), ((), ()))
_TN = (((0,), (0,)), ((), ()))


def _cparams(*sem):
    return pltpu.CompilerParams(dimension_semantics=sem, vmem_limit_bytes=V7X_VMEM_LIMIT)


def _silu(x):
    return x * jax.nn.sigmoid(x)


def _dot(a, b):
    return jnp.dot(a, b, preferred_element_type=F32)


def _split_bf16(x, terms):
    out = []
    for _ in range(terms):
        p = x.astype(BF16)
        out.append(p)
        x = x - p.astype(F32)
    return out


def _norm_mod(x, gain, shift, scale):
    y = x * lax.rsqrt(jnp.mean(x * x, axis=-1, keepdims=True) + NORM_EPS) * gain
    return y * (1.0 + scale) + shift


def _unified_row(i, tpb, nct, nb):
    return jnp.where(lax.rem(i, tpb) < nct, nb, lax.div(i, tpb))


def _ada_kernel(cond_ref, w_ref, b_ref, o_ref):
    s = _silu(cond_ref[...]).astype(BF16)
    o_ref[0] = _dot(s, w_ref[0].astype(BF16)) + b_ref[0]


def _ada_call(cond, w_ada, b_ada):
    n_layers, d, d6 = w_ada.shape
    r = cond.shape[0]
    tn = d6 // 4
    return pl.pallas_call(
        _ada_kernel,
        grid=(n_layers, d6 // tn),
        in_specs=[pl.BlockSpec((r, d), lambda l, n: (0, 0)),
                  pl.BlockSpec((1, d, tn), lambda l, n: (l, 0, n)),
                  pl.BlockSpec((1, 1, tn), lambda l, n: (l, 0, n))],
        out_specs=pl.BlockSpec((1, r, tn), lambda l, n: (l, 0, n)),
        out_shape=jax.ShapeDtypeStruct((n_layers, r, d6), F32),
        compiler_params=_cparams("arbitrary", "arbitrary"),
        name="ada_modulation",
    )(cond, w_ada, b_ada.reshape(n_layers, 1, d6))


def _inproj_kernel(h_ref, mod_ref, gain_ref, w_ref, o_ref, *, d, tpb, nct, nb):
    r = _unified_row(pl.program_id(0), tpb, nct, nb)
    a = _norm_mod(h_ref[...], gain_ref[...], mod_ref[pl.ds(r, 1), 0:d], mod_ref[pl.ds(r, 1), d:2 * d])
    o_ref[...] = _dot(a.astype(BF16), w_ref[...])


def _inproj_call(h, mod, gain, w, *, tm, tpb, nct, nb):
    nt, d = h.shape
    n_out = w.shape[1]
    return pl.pallas_call(
        functools.partial(_inproj_kernel, d=d, tpb=tpb, nct=nct, nb=nb),
        grid=(nt // tm,),
        in_specs=[pl.BlockSpec((tm, d), lambda i: (i, 0)),
                  pl.BlockSpec(mod.shape, lambda i: (0, 0)),
                  pl.BlockSpec((1, d), lambda i: (0, 0)),
                  pl.BlockSpec((d, n_out), lambda i: (0, 0))],
        out_specs=pl.BlockSpec((tm, n_out), lambda i: (i, 0)),
        out_shape=jax.ShapeDtypeStruct((nt, n_out), F32),
        compiler_params=_cparams("arbitrary"),
        name="rec_in_proj",
    )(h, mod, gain, w)


def _swap_halves(x, group):
    if group == LANES:
        return pltpu.roll(x, LANES // 2, 1)
    lane = lax.broadcasted_iota(I32, x.shape, 1)
    half = group // 2
    return jnp.where(lax.rem(lane, group) < half, pltpu.roll(x, LANES - half, 1), pltpu.roll(x, half, 1))


def _rope(x, cos, sin, group):
    parts = [_swap_halves(x[:, s:s + LANES], group) for s in range(0, x.shape[1], LANES)]
    rot = parts[0] if len(parts) == 1 else jnp.concatenate(parts, axis=1)
    return x * cos + rot * sin


def _rec_direction(d, r0, hq, hf, hi, rqk, rv, cs, sn, lb_ref, dmat_ref, qs_ref, ks_ref, g64_ref,
                   ohg, oret, st_hg, st_ret, tri_bf, tri_mask):
    rows = pl.ds(r0, CHUNK)
    lbv = lb_ref[d:d + 1, :]
    f = lbv + (1.0 - lbv) * jax.nn.sigmoid(hf[rows, :])
    logf = jnp.log(f)
    kk = 1.0 - f
    b = sum(_dot(tri_bf, p) for p in _split_bf16(logf, 3))
    tot = b[CHUNK - 1:CHUNK, :] if d == 0 else b[0:1, :]
    q = _silu(hq[rows, :]) * (HG_DK ** -0.5)
    q_in = (q * jnp.exp(b)).astype(BF16)
    k_in = (kk * jnp.exp(-b)).astype(BF16)
    k_st = (kk * jnp.exp(tot - b)).astype(BF16)
    dec = jnp.exp(tot)
    v = hi[rows, :].astype(BF16)
    for h in range(HG_HEADS):
        sl = slice(h * HG_DK, (h + 1) * HG_DK)
        att = jnp.where(tri_mask, lax.dot_general(q_in[:, sl], k_in[:, sl], _NT, preferred_element_type=F32), 0.0)
        s_t = st_hg[d, h]
        o = _dot(att.astype(BF16), v[:, sl]) + lax.dot_general(
            q_in[:, sl], s_t.astype(BF16), _NT, preferred_element_type=F32)
        ohg[rows, sl] = o
        st_hg[d, h] = s_t * dec[:, sl] + lax.dot_general(v[:, sl], k_st[:, sl], _TN, preferred_element_type=F32)
    xr = _rope(rqk[rows, :], cs[rows, :], sn[rows, :], RET_DK)
    nq = RET_HEADS * RET_DK
    q_r = xr[:, 0:nq]
    k_r = xr[:, nq:2 * nq]
    q_b = q_r.astype(BF16)
    k_b = k_r.astype(BF16)
    q_sc = (q_r * qs_ref[d]).astype(BF16)
    k_sc = (k_r * ks_ref[d]).astype(BF16)
    vv = rv[rows, :].astype(BF16)
    for h in range(RET_HEADS):
        sk = slice(h * RET_DK, (h + 1) * RET_DK)
        sv = slice(h * RET_DV, (h + 1) * RET_DV)
        att = lax.dot_general(q_b[:, sk], k_b[:, sk], _NT, preferred_element_type=F32) * dmat_ref[d, h]
        s_t = st_ret[d, h]
        o = _dot(att.astype(BF16), vv[:, sv]) + lax.dot_general(
            q_sc[:, sk], s_t.astype(BF16), _NT, preferred_element_type=F32)
        oret[rows, sv] = o
        st_ret[d, h] = s_t * g64_ref[d, 0:1, sk] + lax.dot_general(
            vv[:, sv], k_sc[:, sk], _TN, preferred_element_type=F32)


def _rec_kernel(hq_f, hf_f, hi_f, rqk_f, rv_f, cs_f, sn_f,
                hq_b, hf_b, hi_b, rqk_b, rv_b, cs_b, sn_b,
                lb_ref, dmat_ref, qs_ref, ks_ref, g64_ref,
                ohg_f, oret_f, ohg_b, oret_b, st_hg, st_ret, *, tc):
    @pl.when(pl.program_id(1) == 0)
    def _():
        st_hg[...] = jnp.zeros_like(st_hg)
        st_ret[...] = jnp.zeros_like(st_ret)

    nch = tc // CHUNK
    row = lax.broadcasted_iota(I32, (CHUNK, CHUNK), 0)
    col = lax.broadcasted_iota(I32, (CHUNK, CHUNK), 1)
    mask_f = col <= row
    mask_b = col >= row
    tri_f = mask_f.astype(F32).astype(BF16)
    tri_b = mask_b.astype(F32).astype(BF16)

    def body(ci, carry):
        common = (lb_ref, dmat_ref, qs_ref, ks_ref, g64_ref)
        _rec_direction(0, pl.multiple_of(ci * CHUNK, CHUNK), hq_f, hf_f, hi_f, rqk_f, rv_f, cs_f, sn_f,
                       *common, ohg_f, oret_f, st_hg, st_ret, tri_f, mask_f)
        _rec_direction(1, pl.multiple_of((nch - 1 - ci) * CHUNK, CHUNK), hq_b, hf_b, hi_b, rqk_b, rv_b,
                       cs_b, sn_b, *common, ohg_b, oret_b, st_hg, st_ret, tri_b, mask_b)
        return carry

    lax.fori_loop(0, nch, body, 0)


def _rec_call(p0, lb, cs, sn, dmat, qs, ks, g64, *, nb, tc, tpb, nct):
    nt = p0.shape[0]

    def fwd(b, j):
        return j

    def bwd(b, j):
        return jnp.where(j < nct, nct - 1 - j, tpb + nct - 1 - j)

    def pspec(col, pos):
        return pl.BlockSpec((tc, HALF), lambda b, j: (b * tpb + pos(b, j), col))

    def tspec(pos):
        return pl.BlockSpec((tc, HALF), lambda b, j: (pos(b, j), 0))

    def whole(a):
        return pl.BlockSpec(a.shape, lambda b, j: (0,) * a.ndim)

    in_specs = ([pspec(0, fwd), pspec(1, fwd), pspec(3, fwd), pspec(5, fwd), pspec(6, fwd), tspec(fwd), tspec(fwd)]
                + [pspec(0, bwd), pspec(2, bwd), pspec(3, bwd), pspec(5, bwd), pspec(6, bwd), tspec(bwd), tspec(bwd)]
                + [whole(a) for a in (lb, dmat, qs, ks, g64)])
    out_f = pl.BlockSpec((tc, HALF), lambda b, j: (b * tpb + fwd(b, j), 0))
    out_b = pl.BlockSpec((tc, HALF), lambda b, j: (b * tpb + bwd(b, j), 0))
    o_shape = jax.ShapeDtypeStruct((nt, HALF), F32)
    return pl.pallas_call(
        functools.partial(_rec_kernel, tc=tc),
        grid=(nb, tpb),
        in_specs=in_specs,
        out_specs=[out_f, out_f, out_b, out_b],
        out_shape=[o_shape] * 4,
        scratch_shapes=[pltpu.VMEM((2, HG_HEADS, HG_DV, HG_DK), F32),
                        pltpu.VMEM((2, RET_HEADS, RET_DV, RET_DK), F32)],
        compiler_params=_cparams("arbitrary", "arbitrary"),
        name="rec_scan",
    )(p0, p0, p0, p0, p0, cs, sn, p0, p0, p0, p0, p0, cs, sn, lb, dmat, qs, ks, g64)


def _rec_merge_kernel(ohf, ohb, orf, orb, hg_ref, rg_ref, hgain, rgain, yh_ref, yr_ref):
    oh = ohf[...] + ohb[...]
    orr = orf[...] + orb[...]
    hg = hg_ref[...]
    rg = rg_ref[...]
    for h in range(HG_HEADS):
        sl = slice(h * LANES, (h + 1) * LANES)
        o = oh[:, sl]
        y = o * lax.rsqrt(jnp.mean(o * o, axis=-1, keepdims=True) + NORM_EPS) * hgain[...]
        yh_ref[:, sl] = (y * _silu(hg[:, sl])).astype(BF16)
        o = orr[:, sl]
        oc = o - jnp.mean(o, axis=-1, keepdims=True)
        y = oc * lax.rsqrt(jnp.mean(oc * oc, axis=-1, keepdims=True) + NORM_EPS) * rgain[...]
        yr_ref[:, sl] = (y * _silu(rg[:, sl])).astype(BF16)


def _rec_merge_call(ohf, orf, ohb, orb, p0, hgain, rgain, *, tm):
    nt = p0.shape[0]
    blk = pl.BlockSpec((tm, HALF), lambda i: (i, 0))
    vec = pl.BlockSpec((1, LANES), lambda i: (0, 0))
    y_shape = jax.ShapeDtypeStruct((nt, HALF), BF16)
    return pl.pallas_call(
        _rec_merge_kernel,
        grid=(nt // tm,),
        in_specs=[blk, blk, blk, blk,
                  pl.BlockSpec((tm, HALF), lambda i: (i, 4)), pl.BlockSpec((tm, HALF), lambda i: (i, 7)), vec, vec],
        out_specs=[blk, blk],
        out_shape=[y_shape, y_shape],
        compiler_params=_cparams("arbitrary"),
        name="rec_merge",
    )(ohf, ohb, orf, orb, p0, p0, hgain, rgain)


def _post_kernel(ya_ref, yb_ref, h_ref, mod_ref, wout_ref, gain_ref, wr_ref, br_ref,
                 hn_ref, f_ref, idx_ref, gate_ref, rank_ref, cnt_ref, cnt_sc, *, d, tm, row_fn):
    i = pl.program_id(0)

    @pl.when(i == 0)
    def _():
        cnt_sc[...] = jnp.zeros_like(cnt_sc)

    r = row_fn(i)
    y = jnp.concatenate([ya_ref[...], yb_ref[...]], axis=1)
    hn = h_ref[...] + mod_ref[pl.ds(r, 1), 2 * d:3 * d] * _dot(y, wout_ref[...])
    hn_ref[...] = hn
    f = _norm_mod(hn, gain_ref[...], mod_ref[pl.ds(r, 1), 3 * d:4 * d], mod_ref[pl.ds(r, 1), 4 * d:5 * d])
    f_ref[...] = f
    f_hi, f_lo = _split_bf16(f, 2)
    w_hi, w_lo = _split_bf16(wr_ref[...], 2)
    logits = _dot(f_hi, w_hi) + _dot(f_hi, w_lo) + _dot(f_lo, w_hi) + br_ref[...]
    lane = lax.broadcasted_iota(I32, (tm, E_PAD), 1)
    vals, idxs = [], []
    work = logits
    for _ in range(TOP_K):
        m = jnp.max(work, axis=-1, keepdims=True)
        sel = jnp.min(jnp.where(work == m, lane, E_PAD), axis=-1, keepdims=True)
        vals.append(m)
        idxs.append(sel)
        work = jnp.where(lane == sel, -jnp.inf, work)
    exps = [jnp.exp(v - vals[0]) for v in vals]
    inv = 1.0 / sum(exps)
    onehot = sum((lane == s).astype(F32) for s in idxs)
    rr = lax.broadcasted_iota(I32, (tm, tm), 0)
    cc = lax.broadcasted_iota(I32, (tm, tm), 1)
    prefix = _dot((cc < rr).astype(F32).astype(BF16), onehot.astype(BF16)) + cnt_sc[0:1, :]
    idx_slab = jnp.zeros((tm, E_PAD), I32)
    gate_slab = jnp.zeros((tm, E_PAD), F32)
    rank_slab = jnp.zeros((tm, E_PAD), F32)
    for k in range(TOP_K):
        idx_slab = jnp.where(lane == k, idxs[k], idx_slab)
        gate_slab = jnp.where(lane == k, exps[k] * inv, gate_slab)
        rk = jnp.sum(jnp.where(lane == idxs[k], prefix, 0.0), axis=-1, keepdims=True)
        rank_slab = jnp.where(lane == k, rk, rank_slab)
    idx_ref[...] = idx_slab
    gate_ref[...] = gate_slab
    rank_ref[...] = rank_slab.astype(I32)
    cnt_sc[...] = cnt_sc[...] + jnp.sum(onehot, axis=0, keepdims=True)
    cnt_ref[...] = cnt_sc[...]


def _post_call(ya, yb, h, mod, wout, gain, wr, br, *, tm, n_tiles, y_block, h_block, row_fn):
    d = h.shape[1]
    n = n_tiles * tm
    tile = lambda w: pl.BlockSpec((tm, w), lambda i: (i, 0))
    const = lambda a: pl.BlockSpec(a.shape, lambda i: (0,) * a.ndim)
    return pl.pallas_call(
        functools.partial(_post_kernel, d=d, tm=tm, row_fn=row_fn),
        grid=(n_tiles,),
        in_specs=[pl.BlockSpec((tm, HALF), lambda i: (y_block(i), 0)),
                  pl.BlockSpec((tm, HALF), lambda i: (y_block(i), 0)),
                  pl.BlockSpec((tm, d), lambda i: (h_block(i), 0)),
                  const(mod), const(wout), const(gain), const(wr), const(br)],
        out_specs=[tile(d), tile(d), tile(E_PAD), tile(E_PAD), tile(E_PAD),
                   pl.BlockSpec((8, E_PAD), lambda i: (0, 0))],
        out_shape=[jax.ShapeDtypeStruct((n, d), F32), jax.ShapeDtypeStruct((n, d), F32),
                   jax.ShapeDtypeStruct((n, E_PAD), I32), jax.ShapeDtypeStruct((n, E_PAD), F32),
                   jax.ShapeDtypeStruct((n, E_PAD), I32), jax.ShapeDtypeStruct((8, E_PAD), F32)],
        scratch_shapes=[pltpu.VMEM((8, E_PAD), F32)],
        compiler_params=_cparams("arbitrary"),
        name="post_mixer",
    )(ya, yb, h, mod, wout, gain, wr, br)


def _row_copy(src, s, dst, t, sem):
    return pltpu.make_async_copy(src.at[pl.ds(s, 1), :], dst.at[pl.ds(t, 1), :], sem)


def _dispatch_kernel(pos_ref, f_ref, xs_in, xs_out, sem, *, tm):
    del xs_in

    def start(t, c):
        for k in range(TOP_K):
            _row_copy(f_ref, t, xs_out, pos_ref[0, 0, t * TOP_K + k], sem).start()
        return c

    def wait(t, c):
        for k in range(TOP_K):
            _row_copy(f_ref, t, xs_out, pos_ref[0, 0, t * TOP_K + k], sem).wait()
        return c

    lax.fori_loop(0, tm, start, 0)
    lax.fori_loop(0, tm, wait, 0)


def _dispatch_call(pos3, f, n_rows, *, tm):
    n, d = f.shape
    return pl.pallas_call(
        functools.partial(_dispatch_kernel, tm=tm),
        grid=(n // tm,),
        in_specs=[pl.BlockSpec((1, 1, tm * TOP_K), lambda i: (i, 0, 0), memory_space=pltpu.SMEM),
                  pl.BlockSpec((tm, d), lambda i: (i, 0)),
                  pl.BlockSpec(memory_space=pl.ANY)],
        out_specs=pl.BlockSpec(memory_space=pl.ANY),
        out_shape=jax.ShapeDtypeStruct((n_rows, d), F32),
        scratch_shapes=[pltpu.SemaphoreType.DMA(())],
        input_output_aliases={2: 0},
        compiler_params=_cparams("arbitrary"),
        name="moe_dispatch",
    )(pos3, f, jnp.zeros((n_rows, d), F32))


def _expert_kernel(be_ref, nv_ref, xs_ref, wgu_ref, bgu_ref, wd_ref, bd_ref, y_ref, wgu_bf, wd_bf, *, de):
    i = pl.program_id(0)
    changed = jnp.logical_or(i == 0, be_ref[i] != be_ref[jnp.maximum(i - 1, 0)])

    @pl.when(changed)
    def _():
        wgu_bf[...] = wgu_ref[0].astype(BF16)
        wd_bf[...] = wd_ref[0].astype(BF16)

    @pl.when(i < nv_ref[0])
    def _():
        gu = _dot(xs_ref[...].astype(BF16), wgu_bf[...]) + bgu_ref[0]
        gate = jnp.minimum(gu[:, 0:de], SWIGLU_LIMIT)
        up = jnp.clip(gu[:, de:2 * de], -SWIGLU_LIMIT, SWIGLU_LIMIT)
        hid = (up + 1.0) * gate * jax.nn.sigmoid(SWIGLU_ALPHA * gate)
        y_ref[...] = _dot(hid.astype(BF16), wd_bf[...]) + bd_ref[0]

    @pl.when(i >= nv_ref[0])
    def _():
        y_ref[...] = jnp.zeros_like(y_ref)


def _expert_call(blk_e, n_valid, xs, w_gu, b_gu, w_down, b_down, *, tm):
    n_rows, d = xs.shape
    ne, _, de2 = w_gu.shape
    de = de2 // 2
    grid_spec = pltpu.PrefetchScalarGridSpec(
        num_scalar_prefetch=2,
        grid=(n_rows // tm,),
        in_specs=[pl.BlockSpec((tm, d), lambda i, be, nv: (jnp.minimum(i, nv[0] - 1), 0)),
                  pl.BlockSpec((1, d, de2), lambda i, be, nv: (be[i], 0, 0)),
                  pl.BlockSpec((1, 1, de2), lambda i, be, nv: (be[i], 0, 0)),
                  pl.BlockSpec((1, de, d), lambda i, be, nv: (be[i], 0, 0)),
                  pl.BlockSpec((1, 1, d), lambda i, be, nv: (be[i], 0, 0))],
        out_specs=pl.BlockSpec((tm, d), lambda i, be, nv: (i, 0)),
        scratch_shapes=[pltpu.VMEM((d, de2), BF16), pltpu.VMEM((de, d), BF16)])
    return pl.pallas_call(
        functools.partial(_expert_kernel, de=de),
        grid_spec=grid_spec,
        out_shape=jax.ShapeDtypeStruct((n_rows, d), F32),
        compiler_params=_cparams("arbitrary"),
        name="moe_experts",
    )(blk_e, n_valid, xs, w_gu, b_gu.reshape(ne, 1, de2), w_down, b_down.reshape(ne, 1, d))


def _combine_kernel(pos_ref, gate_ref, h_ref, mod_ref, y_hbm, o_ref, buf, sem, *, d, tm, row_fn):
    def start(t, c):
        for k in range(TOP_K):
            _row_copy(y_hbm, pos_ref[0, 0, t * TOP_K + k], buf.at[k], t, sem).start()
        return c

    def wait(t, c):
        for k in range(TOP_K):
            _row_copy(y_hbm, pos_ref[0, 0, t * TOP_K + k], buf.at[k], t, sem).wait()
        return c

    lax.fori_loop(0, tm, start, 0)
    lax.fori_loop(0, tm, wait, 0)
    r = row_fn(pl.program_id(0))
    gates = gate_ref[...]
    acc = gates[:, 0:1] * buf[0]
    for k in range(1, TOP_K):
        acc = acc + gates[:, k:k + 1] * buf[k]
    o_ref[...] = h_ref[...] + mod_ref[pl.ds(r, 1), 5 * d:6 * d] * acc


def _combine_call(pos3, gates, hn, mod, y_rows, *, tm, row_fn):
    n, d = hn.shape
    return pl.pallas_call(
        functools.partial(_combine_kernel, d=d, tm=tm, row_fn=row_fn),
        grid=(n // tm,),
        in_specs=[pl.BlockSpec((1, 1, tm * TOP_K), lambda i: (i, 0, 0), memory_space=pltpu.SMEM),
                  pl.BlockSpec((tm, E_PAD), lambda i: (i, 0)),
                  pl.BlockSpec((tm, d), lambda i: (i, 0)),
                  pl.BlockSpec(mod.shape, lambda i: (0, 0)),
                  pl.BlockSpec(memory_space=pl.ANY)],
        out_specs=pl.BlockSpec((tm, d), lambda i: (i, 0)),
        out_shape=jax.ShapeDtypeStruct((n, d), F32),
        scratch_shapes=[pltpu.VMEM((TOP_K, tm, d), F32), pltpu.SemaphoreType.DMA(())],
        compiler_params=_cparams("arbitrary"),
        name="moe_combine",
    )(pos3, gates, hn, mod, y_rows)


def _moe(f, hn, mod, idx_slab, gate_slab, rank_slab, counts, w_gu, b_gu, w_down, b_down, *, tm, row_fn):
    n, _ = f.shape
    n_blocks = -(-n * TOP_K // tm) + N_EXPERTS
    cnt = counts[0, :N_EXPERTS].astype(I32)
    padded = (cnt + tm - 1) // tm * tm
    pad_end = jnp.cumsum(padded)
    pad_start = pad_end - padded
    idx = idx_slab[:, :TOP_K]
    pos = pad_start[idx] + rank_slab[:, :TOP_K]
    pos3 = pos.reshape(n // tm, 1, tm * TOP_K)
    n_valid = (pad_end[-1] // tm).astype(I32)
    blk = jnp.arange(n_blocks, dtype=I32)
    blk_e = jnp.minimum(jnp.searchsorted(pad_end, blk * tm, side='right'), N_EXPERTS - 1).astype(I32)
    blk_e = jnp.where(blk < n_valid, blk_e, blk_e[jnp.maximum(n_valid - 1, 0)])
    xs = _dispatch_call(pos3, f, n_blocks * tm, tm=tm)
    y_rows = _expert_call(blk_e, n_valid.reshape(1), xs, w_gu, b_gu, w_down, b_down, tm=tm)
    return _combine_call(pos3, gate_slab, hn, mod, y_rows, tm=tm, row_fn=row_fn)


def _att_in_kernel(h_ref, mod_ref, gain_ref, w_ref, qg_gain, kg_gain, dq_gain, dk_gain, bd_ref,
                   cg_ref, sg_ref, cd_ref, sd_ref,
                   qg_ref, kg_ref, vg_ref, qd_ref, kd_ref, vd_ref, *, d, tpb, nct, nb):
    r = _unified_row(pl.program_id(0), tpb, nct, nb)
    a = _norm_mod(h_ref[...], gain_ref[...], mod_ref[pl.ds(r, 1), 0:d], mod_ref[pl.ds(r, 1), d:2 * d])
    p = _dot(a.astype(BF16), w_ref[...])
    cg, sg = cg_ref[...], sg_ref[...]

    def head_norm_rope(x, gain):
        y = x * lax.rsqrt(jnp.mean(x * x, axis=-1, keepdims=True) + NORM_EPS) * gain
        return _rope(y, cg, sg, GQA_HD).astype(BF16)

    c0 = 0
    for hh in range(GQA_HEADS):
        qg_ref[:, hh * GQA_HD:(hh + 1) * GQA_HD] = head_norm_rope(p[:, c0:c0 + GQA_HD], qg_gain[...])
        c0 += GQA_HD
    for hh in range(GQA_KV_HEADS):
        kg_ref[:, hh * GQA_HD:(hh + 1) * GQA_HD] = head_norm_rope(p[:, c0:c0 + GQA_HD], kg_gain[...])
        c0 += GQA_HD
    wv = GQA_KV_HEADS * GQA_HD
    vg_ref[...] = p[:, c0:c0 + wv].astype(BF16)
    c0 += wv

    def group_norm_rope(x, gain):
        ss = sum(_dot(piece, bd_ref[...]) for piece in _split_bf16(x * x, 2))
        y = x * lax.rsqrt(ss * (1.0 / DIFF_HD) + NORM_EPS) * gain
        return _rope(y, cd_ref[...], sd_ref[...], DIFF_HD).astype(BF16)

    qd_ref[...] = group_norm_rope(p[:, c0:c0 + HALF], dq_gain[...])
    c0 += HALF
    kd_ref[...] = group_norm_rope(p[:, c0:c0 + HALF], dk_gain[...])
    c0 += HALF
    vd_ref[...] = p[:, c0:c0 + HALF].astype(BF16)


def _att_in_call(h, mod, gain, w, qg_gain, kg_gain, dq_gain, dk_gain, bd, cg, sg, cd, sd, *, tm, tpb, nct, nb):
    nt, d = h.shape
    const = lambda a: pl.BlockSpec(a.shape, lambda i: (0,) * a.ndim)
    tab = lambda w_: pl.BlockSpec((tm, w_), lambda i: (lax.rem(i, tpb), 0))
    out = lambda w_: pl.BlockSpec((tm, w_), lambda i: (i, 0))
    shp = lambda w_: jax.ShapeDtypeStruct((nt, w_), BF16)
    wkv = GQA_KV_HEADS * GQA_HD
    return pl.pallas_call(
        functools.partial(_att_in_kernel, d=d, tpb=tpb, nct=nct, nb=nb),
        grid=(nt // tm,),
        in_specs=[pl.BlockSpec((tm, d), lambda i: (i, 0)), const(mod), const(gain), const(w),
                  const(qg_gain), const(kg_gain), const(dq_gain), const(dk_gain), const(bd),
                  tab(GQA_HD), tab(GQA_HD), tab(HALF), tab(HALF)],
        out_specs=[out(HALF), out(wkv), out(wkv), out(HALF), out(HALF), out(HALF)],
        out_shape=[shp(HALF), shp(wkv), shp(wkv), shp(HALF), shp(HALF), shp(HALF)],
        compiler_params=_cparams("arbitrary"),
        name="att_in_proj",
    )(h, mod, gain, w, qg_gain, kg_gain, dq_gain, dk_gain, bd, cg, sg, cd, sd)


def _gqa_kernel(q_ref, k_ref, v_ref, o_ref):
    s = lax.dot_general(q_ref[...], k_ref[...], _NT, preferred_element_type=F32) * (GQA_HD ** -0.5)
    e = jnp.exp(s - jnp.max(s, axis=-1, keepdims=True))
    o = _dot(e.astype(BF16), v_ref[...]) / jnp.sum(e, axis=-1, keepdims=True)
    o_ref[...] = o.astype(BF16)


def _gqa_call(qg, kg, vg, *, nb, tq, tot, nqt, lat0):
    group = GQA_HEADS // GQA_KV_HEADS
    upb = tot // tq
    return pl.pallas_call(
        _gqa_kernel,
        grid=(nb, GQA_HEADS, nqt),
        in_specs=[pl.BlockSpec((tq, GQA_HD), lambda b, h, q: (b * upb + lat0 + q, h)),
                  pl.BlockSpec((tot, GQA_HD), lambda b, h, q: (b, h // group)),
                  pl.BlockSpec((tot, GQA_HD), lambda b, h, q: (b, h // group))],
        out_specs=pl.BlockSpec((tq, GQA_HD), lambda b, h, q: (b * nqt + q, h)),
        out_shape=jax.ShapeDtypeStruct((nb * nqt * tq, HALF), BF16),
        compiler_params=_cparams("arbitrary", "arbitrary", "arbitrary"),
        name="gqa_attention",
    )(qg, kg, vg)


def _diff_kernel(q_ref, k_ref, v_ref, lam_ref, gain_ref, o_ref, *, lam_init):
    lp = lam_ref[...]
    lam = (jnp.exp(jnp.sum(lp[0:1] * lp[1:2], axis=-1, keepdims=True))
           - jnp.exp(jnp.sum(lp[2:3] * lp[3:4], axis=-1, keepdims=True)) + lam_init)
    q = q_ref[...]
    k = k_ref[...]
    lane = lax.broadcasted_iota(I32, q.shape, 1)
    zero = jnp.zeros_like(q)

    def probs(q_half):
        s = lax.dot_general(q_half, k, _NT, preferred_element_type=F32) * (DIFF_HD ** -0.5)
        e = jnp.exp(s - jnp.max(s, axis=-1, keepdims=True))
        return e, 1.0 / jnp.sum(e, axis=-1, keepdims=True)

    e1, inv1 = probs(jnp.where(lane < DIFF_HD, q, zero))
    e2, inv2 = probs(jnp.where(lane >= DIFF_HD, q, zero))
    a = e1 * inv1 - e2 * (lam * inv2)
    o = _dot(a.astype(BF16), v_ref[...])
    y = o * lax.rsqrt(jnp.mean(o * o, axis=-1, keepdims=True) + NORM_EPS) * gain_ref[...]
    o_ref[...] = (y * (1.0 - lam_init)).astype(BF16)


def _diff_call(qd, kd, vd, lam_params, gain, *, nb, tq, tot, nqt, lat0, lam_init):
    upb = tot // tq
    hd2 = 2 * DIFF_HD
    return pl.pallas_call(
        functools.partial(_diff_kernel, lam_init=lam_init),
        grid=(nb, DIFF_HEADS, nqt),
        in_specs=[pl.BlockSpec((tq, hd2), lambda b, h, q: (b * upb + lat0 + q, h)),
                  pl.BlockSpec((tot, hd2), lambda b, h, q: (b, h)),
                  pl.BlockSpec((tot, hd2), lambda b, h, q: (b, h)),
                  pl.BlockSpec(lam_params.shape, lambda b, h, q: (0, 0)),
                  pl.BlockSpec(gain.shape, lambda b, h, q: (0, 0))],
        out_specs=pl.BlockSpec((tq, hd2), lambda b, h, q: (b * nqt + q, h)),
        out_shape=jax.ShapeDtypeStruct((nb * nqt * tq, HALF), BF16),
        compiler_params=_cparams("arbitrary", "arbitrary", "arbitrary"),
        name="diff_attention",
    )(qd, kd, vd, lam_params, gain)


def _deinterleave(hd):
    return np.concatenate([np.arange(0, hd, 2), np.arange(1, hd, 2)])


def _head_perm(n_heads, hd):
    return np.concatenate([h * hd + _deinterleave(hd) for h in range(n_heads)])


def _rope_tables(n_lat, n_ctx, hd, reps):
    t = np.arange(n_lat)
    row, col = (t // GRID_W).astype(np.float64), (t % GRID_W).astype(np.float64)
    axis_dim = hd // 2
    inv_freq = ROPE_THETA ** (-np.arange(0, axis_dim, 2, dtype=np.float64) / axis_dim)
    ang = np.concatenate([row[:, None] * inv_freq, col[:, None] * inv_freq], axis=-1)
    cos = np.concatenate([np.ones((n_ctx, hd // 2)), np.cos(ang)], axis=0)
    sin = np.concatenate([np.zeros((n_ctx, hd // 2)), np.sin(ang)], axis=0)
    c = np.tile(np.concatenate([cos, cos], axis=1), (1, reps))
    s = np.tile(np.concatenate([-sin, sin], axis=1), (1, reps))
    return jnp.asarray(c, F32), jnp.asarray(s, F32)


def _retention_tables():
    gam = 1.0 - 2.0 ** (-5.0 - np.arange(RET_HEADS, dtype=np.float64))
    gams = (gam, gam[::-1])
    c = np.arange(CHUNK, dtype=np.float64)
    diff = c[:, None] - c[None, :]
    dmat = np.zeros((2, RET_HEADS, CHUNK, CHUNK))
    qs = np.zeros((2, CHUNK, RET_HEADS * RET_DK))
    ks = np.zeros((2, CHUNK, RET_HEADS * RET_DK))
    g64 = np.zeros((2, 8, RET_HEADS * RET_DK))
    for d in range(2):
        for h in range(RET_HEADS):
            g = gams[d][h]
            sl = slice(h * RET_DK, (h + 1) * RET_DK)
            if d == 0:
                dmat[d, h] = np.where(diff >= 0, g ** np.maximum(diff, 0), 0.0)
                qs[d, :, sl] = (g ** (c + 1))[:, None]
                ks[d, :, sl] = (g ** (CHUNK - 1 - c))[:, None]
            else:
                dmat[d, h] = np.where(diff <= 0, g ** np.maximum(-diff, 0), 0.0)
                qs[d, :, sl] = (g ** (CHUNK - c))[:, None]
                ks[d, :, sl] = (g ** c)[:, None]
            g64[d, :, sl] = g ** CHUNK
    return tuple(jnp.asarray(a, F32) for a in (dmat, qs, ks, g64))


def _plan(nb, n_lat, n_ctx):
    tm = 256 if (n_ctx % 256 == 0 and n_lat % 256 == 0) else 128
    assert n_ctx % tm == 0 and n_lat % tm == 0 and tm % CHUNK == 0 and n_lat % GRID_W == 0
    tq = tm
    return tm, tq


def kernel(x, c, ctx, c_ctx, norm_mix, norm_ffn, w_ada, b_ada, rec_w_in, rec_lb_logits, rec_w_out, rec_hg_gain,
           rec_ret_gain, att_w_in, att_w_out, att_q_gain, att_k_gain, diff_q_gain, diff_k_gain, diff_lambda,
           diff_gain, w_router, b_router, w_gu, b_gu, w_down, b_down):
    nb, n_lat, d = x.shape
    n_ctx = ctx.shape[1]
    assert w_ada.shape[0] == 2, "two layers: recurrent mixer then attention mixer"
    tm, tq = _plan(nb, n_lat, n_ctx)
    tot = n_ctx + n_lat
    tpb, nct = tot // tm, n_ctx // tm
    nt = nb * tot
    nql = n_lat // tm

    unified_row = functools.partial(_unified_row, tpb=tpb, nct=nct, nb=nb)
    latent_row = lambda i: lax.div(i, nql)
    latent_block = lambda i: lax.div(i, nql) * tpb + nct + lax.rem(i, nql)
    ident = lambda i: i

    n_cond = -(-(nb + 1) // 8) * 8
    cond = jnp.concatenate([c, c_ctx[None], jnp.zeros((n_cond - nb - 1, d), F32)], axis=0)
    mod = _ada_call(cond, w_ada, b_ada)

    wr = jnp.pad(w_router, ((0, 0), (0, 0), (0, E_PAD - N_EXPERTS)))
    br = jnp.pad(b_router, ((0, 0), (0, E_PAD - N_EXPERTS)), constant_values=NEG_BIG)[:, None, :]

    h = jnp.concatenate([ctx, x], axis=1).reshape(nt, d)

    lb = jnp.cumsum(jax.nn.softmax(rec_lb_logits.astype(F32), axis=0), axis=0)[0]
    w_in = rec_w_in[0]
    c_rq = HG_HEADS * HG_DK * 3 + HG_HEADS * HG_DV * 2
    nrk = RET_HEADS * RET_DK
    perm = _head_perm(RET_HEADS, RET_DK)
    w_in = jnp.concatenate([w_in[:, :c_rq], w_in[:, c_rq + perm], w_in[:, c_rq + nrk + perm] * (RET_DK ** -0.5),
                            w_in[:, c_rq + 2 * nrk:]], axis=1).astype(BF16)
    cs_r, sn_r = _rope_tables(n_lat, n_ctx, RET_DK, 2 * RET_HEADS)
    dmat, qs, ks, g64 = _retention_tables()
    p0 = _inproj_call(h, mod[0], norm_mix[0][None], w_in, tm=tm, tpb=tpb, nct=nct, nb=nb)
    ohf, orf, ohb, orb = _rec_call(p0, lb, cs_r, sn_r, dmat, qs, ks, g64, nb=nb, tc=tm, tpb=tpb, nct=nct)
    yh, yr = _rec_merge_call(ohf, orf, ohb, orb, p0, rec_hg_gain[0][None], rec_ret_gain[0][None], tm=tm)
    hn, f, idx_s, gate_s, rank_s, cnt = _post_call(
        yh, yr, h, mod[0], rec_w_out[0].astype(BF16), norm_ffn[0][None], wr[0], br[0],
        tm=tm, n_tiles=nt // tm, y_block=ident, h_block=ident, row_fn=unified_row)
    h = _moe(f, hn, mod[0], idx_s, gate_s, rank_s, cnt, w_gu[0], b_gu[0], w_down[0], b_down[0],
             tm=tm, row_fn=unified_row)

    pg, pd = _head_perm(GQA_HEADS, GQA_HD), _head_perm(2 * DIFF_HEADS, DIFF_HD)
    w_in = att_w_in[0]
    o_gk = GQA_HEADS * GQA_HD
    o_gv = o_gk + GQA_KV_HEADS * GQA_HD
    o_dq = o_gv + GQA_KV_HEADS * GQA_HD
    o_dk = o_dq + HALF
    o_dv = o_dk + HALF
    w_in = jnp.concatenate([w_in[:, pg], w_in[:, o_gk + pg[:GQA_KV_HEADS * GQA_HD]], w_in[:, o_gv:o_dq],
                            w_in[:, o_dq + pd], w_in[:, o_dk + pd], w_in[:, o_dv:]], axis=1).astype(BF16)
    dg, dd = _deinterleave(GQA_HD), _deinterleave(DIFF_HD)
    cg, sg = _rope_tables(n_lat, n_ctx, GQA_HD, 1)
    cd, sd = _rope_tables(n_lat, n_ctx, DIFF_HD, 2 * DIFF_HEADS)
    bd = jnp.asarray(np.kron(np.eye(2 * DIFF_HEADS), np.ones((DIFF_HD, DIFF_HD))), BF16)
    qg, kg, vg, qd, kd, vd = _att_in_call(
        h, mod[1], norm_mix[1][None], w_in, att_q_gain[0][dg][None], att_k_gain[0][dg][None],
        jnp.tile(diff_q_gain[0][dd], 2 * DIFF_HEADS)[None], jnp.tile(diff_k_gain[0][dd], 2 * DIFF_HEADS)[None],
        bd, cg, sg, cd, sd, tm=tm, tpb=tpb, nct=nct, nb=nb)
    nqt = n_lat // tq
    yg = _gqa_call(qg, kg, vg, nb=nb, tq=tq, tot=tot, nqt=nqt, lat0=n_ctx // tq)
    lam_init = 0.8 - 0.6 * math.exp(-0.3 * 1)
    yd = _diff_call(qd, kd, vd, diff_lambda[0], diff_gain[0][None], nb=nb, tq=tq, tot=tot, nqt=nqt,
                    lat0=n_ctx // tq, lam_init=lam_init)
    hn, f, idx_s, gate_s, rank_s, cnt = _post_call(
        yg, yd, h, mod[1], att_w_out[0].astype(BF16), norm_ffn[1][None], wr[1], br[1],
        tm=tm, n_tiles=nb * nql, y_block=ident, h_block=latent_block, row_fn=latent_row)
    out = _moe(f, hn, mod[1], idx_s, gate_s, rank_s, cnt, w_gu[1], b_gu[1], w_down[1], b_down[1],
               tm=tm, row_fn=latent_row)
    return out.reshape(nb, n_lat, d)
```

```python
import functools
import math

import numpy as np
import jax
import jax.numpy as jnp
from jax import lax
from jax.experimental import pallas as pl
from jax.experimental.pallas import tpu as pltpu

F32 = jnp.float32
BF16 = jnp.bfloat16
I32 = jnp.int32

GRID_W = 64
HG_HEADS, HG_DK, HG_DV = 4, 128, 128
RET_HEADS, RET_DK, RET_DV = 4, 64, 128
GQA_HEADS, GQA_KV_HEADS, GQA_HD = 4, 2, 128
DIFF_HEADS, DIFF_HD = 4, 64
N_EXPERTS, TOP_K = 32, 4
SWIGLU_LIMIT, SWIGLU_ALPHA = 7.0, 1.702
CHUNK = 64
ROPE_THETA = 10000.0
NORM_EPS = 1e-6

LANES = 128
E_PAD = LANES
NEG_BIG = -1e30
V7X_VMEM_LIMIT = 56 * 1024 * 1024
HALF = 512

_NT = (((1,), (1,)), ((), ()))
_TN = (((0,), (0,)), ((), ()))


def _cparams(*sem):
    return pltpu.CompilerParams(dimension_semantics=sem, vmem_limit_bytes=V7X_VMEM_LIMIT)


def _silu(x):
    return x * jax.nn.sigmoid(x)


def _dot(a, b):
    return jnp.dot(a, b, preferred_element_type=F32)


def _split_bf16(x, terms):
    out = []
    for _ in range(terms):
        p = x.astype(BF16)
        out.append(p)
        x = x - p.astype(F32)
    return out


def _norm_mod(x, gain, shift, scale):
    y = x * lax.rsqrt(jnp.mean(x * x, axis=-1, keepdims=True) + NORM_EPS) * gain
    return y * (1.0 + scale) + shift


def _unified_row(i, tpb, nct, nb):
    return jnp.where(lax.rem(i, tpb) < nct, nb, lax.div(i, tpb))


def _ada_kernel(cond_ref, w_ref, b_ref, o_ref):
    s = _silu(cond_ref[...]).astype(BF16)
    o_ref[0] = _dot(s, w_ref[0].astype(BF16)) + b_ref[0]


def _ada_call(cond, w_ada, b_ada):
    n_layers, d, d6 = w_ada.shape
    r = cond.shape[0]
    tn = d6 // 4
    return pl.pallas_call(
        _ada_kernel,
        grid=(n_layers, d6 // tn),
        in_specs=[pl.BlockSpec((r, d), lambda l, n: (0, 0)),
                  pl.BlockSpec((1, d, tn), lambda l, n: (l, 0, n)),
                  pl.BlockSpec((1, 1, tn), lambda l, n: (l, 0, n))],
        out_specs=pl.BlockSpec((1, r, tn), lambda l, n: (l, 0, n)),
        out_shape=jax.ShapeDtypeStruct((n_layers, r, d6), F32),
        compiler_params=_cparams("arbitrary", "arbitrary"),
        name="ada_modulation",
    )(cond, w_ada, b_ada.reshape(n_layers, 1, d6))


def _inproj_kernel(h_ref, mod_ref, gain_ref, w_ref, o_ref, *, d, tpb, nct, nb):
    r = _unified_row(pl.program_id(0), tpb, nct, nb)
    a = _norm_mod(h_ref[...], gain_ref[...], mod_ref[pl.ds(r, 1), 0:d], mod_ref[pl.ds(r, 1), d:2 * d])
    o_ref[...] = _dot(a.astype(BF16), w_ref[...])


def _inproj_call(h, mod, gain, w, *, tm, tpb, nct, nb):
    nt, d = h.shape
    n_out = w.shape[1]
    return pl.pallas_call(
        functools.partial(_inproj_kernel, d=d, tpb=tpb, nct=nct, nb=nb),
        grid=(nt // tm,),
        in_specs=[pl.BlockSpec((tm, d), lambda i: (i, 0)),
                  pl.BlockSpec(mod.shape, lambda i: (0, 0)),
                  pl.BlockSpec((1, d), lambda i: (0, 0)),
                  pl.BlockSpec((d, n_out), lambda i: (0, 0))],
        out_specs=pl.BlockSpec((tm, n_out), lambda i: (i, 0)),
        out_shape=jax.ShapeDtypeStruct((nt, n_out), F32),
        compiler_params=_cparams("arbitrary"),
        name="rec_in_proj",
    )(h, mod, gain, w)


def _swap_halves(x, group):
    if group == LANES:
        return pltpu.roll(x, LANES // 2, 1)
    lane = lax.broadcasted_iota(I32, x.shape, 1)
    half = group // 2
    return jnp.where(lax.rem(lane, group) < half, pltpu.roll(x, LANES - half, 1), pltpu.roll(x, half, 1))


def _rope(x, cos, sin, group):
    parts = [_swap_halves(x[:, s:s + LANES], group) for s in range(0, x.shape[1], LANES)]
    rot = parts[0] if len(parts) == 1 else jnp.concatenate(parts, axis=1)
    return x * cos + rot * sin


def _rec_direction(d, r0, hq, hf, hi, rqk, rv, cs, sn, lb_ref, dmat_ref, qs_ref, ks_ref, g64_ref,
                   ohg, oret, st_hg, st_ret, tri_bf, tri_mask):
    rows = pl.ds(r0, CHUNK)
    lbv = lb_ref[d:d + 1, :]
    f = lbv + (1.0 - lbv) * jax.nn.sigmoid(hf[rows, :])
    logf = jnp.log(f)
    kk = 1.0 - f
    b = sum(_dot(tri_bf, p) for p in _split_bf16(logf, 3))
    tot = b[CHUNK - 1:CHUNK, :] if d == 0 else b[0:1, :]
    q = _silu(hq[rows, :]) * (HG_DK ** -0.5)
    q_in = (q * jnp.exp(b)).astype(BF16)
    k_in = (kk * jnp.exp(-b)).astype(BF16)
    k_st = (kk * jnp.exp(tot - b)).astype(BF16)
    dec = jnp.exp(tot)
    v = hi[rows, :].astype(BF16)
    for h in range(HG_HEADS):
        sl = slice(h * HG_DK, (h + 1) * HG_DK)
        att = jnp.where(tri_mask, lax.dot_general(q_in[:, sl], k_in[:, sl], _NT, preferred_element_type=F32), 0.0)
        s_t = st_hg[d, h]
        o = _dot(att.astype(BF16), v[:, sl]) + lax.dot_general(
            q_in[:, sl], s_t.astype(BF16), _NT, preferred_element_type=F32)
        ohg[rows, sl] = o
        st_hg[d, h] = s_t * dec[:, sl] + lax.dot_general(v[:, sl], k_st[:, sl], _TN, preferred_element_type=F32)
    xr = _rope(rqk[rows, :], cs[rows, :], sn[rows, :], RET_DK)
    nq = RET_HEADS * RET_DK
    q_r = xr[:, 0:nq]
    k_r = xr[:, nq:2 * nq]
    q_b = q_r.astype(BF16)
    k_b = k_r.astype(BF16)
    q_sc = (q_r * qs_ref[d]).astype(BF16)
    k_sc = (k_r * ks_ref[d]).astype(BF16)
    vv = rv[rows, :].astype(BF16)
    for h in range(RET_HEADS):
        sk = slice(h * RET_DK, (h + 1) * RET_DK)
        sv = slice(h * RET_DV, (h + 1) * RET_DV)
        att = lax.dot_general(q_b[:, sk], k_b[:, sk], _NT, preferred_element_type=F32) * dmat_ref[d, h]
        s_t = st_ret[d, h]
        o = _dot(att.astype(BF16), vv[:, sv]) + lax.dot_general(
            q_sc[:, sk], s_t.astype(BF16), _NT, preferred_element_type=F32)
        oret[rows, sv] = o
        st_ret[d, h] = s_t * g64_ref[d, 0:1, sk] + lax.dot_general(
            vv[:, sv], k_sc[:, sk], _TN, preferred_element_type=F32)


def _rec_kernel(hq_f, hf_f, hi_f, rqk_f, rv_f, cs_f, sn_f,
                hq_b, hf_b, hi_b, rqk_b, rv_b, cs_b, sn_b,
                lb_ref, dmat_ref, qs_ref, ks_ref, g64_ref,
                ohg_f, oret_f, ohg_b, oret_b, st_hg, st_ret, *, tc):
    @pl.when(pl.program_id(1) == 0)
    def _():
        st_hg[...] = jnp.zeros_like(st_hg)
        st_ret[...] = jnp.zeros_like(st_ret)

    nch = tc // CHUNK
    row = lax.broadcasted_iota(I32, (CHUNK, CHUNK), 0)
    col = lax.broadcasted_iota(I32, (CHUNK, CHUNK), 1)
    mask_f = col <= row
    mask_b = col >= row
    tri_f = mask_f.astype(F32).astype(BF16)
    tri_b = mask_b.astype(F32).astype(BF16)

    def body(ci, carry):
        common = (lb_ref, dmat_ref, qs_ref, ks_ref, g64_ref)
        _rec_direction(0, pl.multiple_of(ci * CHUNK, CHUNK), hq_f, hf_f, hi_f, rqk_f, rv_f, cs_f, sn_f,
                       *common, ohg_f, oret_f, st_hg, st_ret, tri_f, mask_f)
        _rec_direction(1, pl.multiple_of((nch - 1 - ci) * CHUNK, CHUNK), hq_b, hf_b, hi_b, rqk_b, rv_b,
                       cs_b, sn_b, *common, ohg_b, oret_b, st_hg, st_ret, tri_b, mask_b)
        return carry

    lax.fori_loop(0, nch, body, 0)


def _rec_call(p0, lb, cs, sn, dmat, qs, ks, g64, *, nb, tc, tpb, nct):
    nt = p0.shape[0]

    def fwd(b, j):
        return j

    def bwd(b, j):
        return jnp.where(j < nct, nct - 1 - j, tpb + nct - 1 - j)

    def pspec(col, pos):
        return pl.BlockSpec((tc, HALF), lambda b, j: (b * tpb + pos(b, j), col))

    def tspec(pos):
        return pl.BlockSpec((tc, HALF), lambda b, j: (pos(b, j), 0))

    def whole(a):
        return pl.BlockSpec(a.shape, lambda b, j: (0,) * a.ndim)

    in_specs = ([pspec(0, fwd), pspec(1, fwd), pspec(3, fwd), pspec(5, fwd), pspec(6, fwd), tspec(fwd), tspec(fwd)]
                + [pspec(0, bwd), pspec(2, bwd), pspec(3, bwd), pspec(5, bwd), pspec(6, bwd), tspec(bwd), tspec(bwd)]
                + [whole(a) for a in (lb, dmat, qs, ks, g64)])
    out_f = pl.BlockSpec((tc, HALF), lambda b, j: (b * tpb + fwd(b, j), 0))
    out_b = pl.BlockSpec((tc, HALF), lambda b, j: (b * tpb + bwd(b, j), 0))
    o_shape = jax.ShapeDtypeStruct((nt, HALF), F32)
    return pl.pallas_call(
        functools.partial(_rec_kernel, tc=tc),
        grid=(nb, tpb),
        in_specs=in_specs,
        out_specs=[out_f, out_f, out_b, out_b],
        out_shape=[o_shape] * 4,
        scratch_shapes=[pltpu.VMEM((2, HG_HEADS, HG_DV, HG_DK), F32),
                        pltpu.VMEM((2, RET_HEADS, RET_DV, RET_DK), F32)],
        compiler_params=_cparams("arbitrary", "arbitrary"),
        name="rec_scan",
    )(p0, p0, p0, p0, p0, cs, sn, p0, p0, p0, p0, p0, cs, sn, lb, dmat, qs, ks, g64)


def _rec_merge_kernel(ohf, ohb, orf, orb, hg_ref, rg_ref, hgain, rgain, yh_ref, yr_ref):
    oh = ohf[...] + ohb[...]
    orr = orf[...] + orb[...]
    hg = hg_ref[...]
    rg = rg_ref[...]
    for h in range(HG_HEADS):
        sl = slice(h * LANES, (h + 1) * LANES)
        o = oh[:, sl]
        y = o * lax.rsqrt(jnp.mean(o * o, axis=-1, keepdims=True) + NORM_EPS) * hgain[...]
        yh_ref[:, sl] = (y * _silu(hg[:, sl])).astype(BF16)
        o = orr[:, sl]
        oc = o - jnp.mean(o, axis=-1, keepdims=True)
        y = oc * lax.rsqrt(jnp.mean(oc * oc, axis=-1, keepdims=True) + NORM_EPS) * rgain[...]
        yr_ref[:, sl] = (y * _silu(rg[:, sl])).astype(BF16)


def _rec_merge_call(ohf, orf, ohb, orb, p0, hgain, rgain, *, tm):
    nt = p0.shape[0]
    blk = pl.BlockSpec((tm, HALF), lambda i: (i, 0))
    vec = pl.BlockSpec((1, LANES), lambda i: (0, 0))
    y_shape = jax.ShapeDtypeStruct((nt, HALF), BF16)
    return pl.pallas_call(
        _rec_merge_kernel,
        grid=(nt // tm,),
        in_specs=[blk, blk, blk, blk,
                  pl.BlockSpec((tm, HALF), lambda i: (i, 4)), pl.BlockSpec((tm, HALF), lambda i: (i, 7)), vec, vec],
        out_specs=[blk, blk],
        out_shape=[y_shape, y_shape],
        compiler_params=_cparams("arbitrary"),
        name="rec_merge",
    )(ohf, ohb, orf, orb, p0, p0, hgain, rgain)


def _post_kernel(ya_ref, yb_ref, h_ref, mod_ref, wout_ref, gain_ref, wr_ref, br_ref,
                 hn_ref, f_ref, lpos_ref, gate_ref, ne_ref, cb_ref, cnt_ref, cnt_sc, *, d, tm, row_fn):
    i = pl.program_id(0)

    @pl.when(i == 0)
    def _():
        cnt_sc[...] = jnp.zeros_like(cnt_sc)

    r = row_fn(i)
    y = jnp.concatenate([ya_ref[...], yb_ref[...]], axis=1)
    hn = h_ref[...] + mod_ref[pl.ds(r, 1), 2 * d:3 * d] * _dot(y, wout_ref[...])
    hn_ref[...] = hn
    f = _norm_mod(hn, gain_ref[...], mod_ref[pl.ds(r, 1), 3 * d:4 * d], mod_ref[pl.ds(r, 1), 4 * d:5 * d])
    f_ref[...] = f
    f_hi, f_lo = _split_bf16(f, 2)
    w_hi, w_lo = _split_bf16(wr_ref[...], 2)
    logits = _dot(f_hi, w_hi) + _dot(f_hi, w_lo) + _dot(f_lo, w_hi) + br_ref[...]
    lane = lax.broadcasted_iota(I32, (tm, E_PAD), 1)
    vals, idxs = [], []
    work = logits
    for _ in range(TOP_K):
        m = jnp.max(work, axis=-1, keepdims=True)
        sel = jnp.min(jnp.where(work == m, lane, E_PAD), axis=-1, keepdims=True)
        vals.append(m)
        idxs.append(sel)
        work = jnp.where(lane == sel, -jnp.inf, work)
    exps = [jnp.exp(v - vals[0]) for v in vals]
    inv = 1.0 / sum(exps)
    onehot = sum((lane == s).astype(F32) for s in idxs)
    n_e = jnp.sum(onehot, axis=0, keepdims=True)
    rr = lax.broadcasted_iota(I32, (tm, tm), 0)
    cc = lax.broadcasted_iota(I32, (tm, tm), 1)
    within = _dot((cc < rr).astype(F32).astype(BF16), onehot.astype(BF16))
    ea = lax.broadcasted_iota(I32, (E_PAD, E_PAD), 0)
    eb = lax.broadcasted_iota(I32, (E_PAD, E_PAD), 1)
    before = _dot(jnp.broadcast_to(n_e, (8, E_PAD)).astype(BF16), (ea < eb).astype(F32).astype(BF16))[0:1]
    base = within + before
    lpos_slab = jnp.zeros((tm, E_PAD), F32)
    gate_slab = jnp.zeros((tm, E_PAD), F32)
    for k in range(TOP_K):
        gate_slab = jnp.where(lane == k, exps[k] * inv, gate_slab)
        lp = jnp.sum(jnp.where(lane == idxs[k], base, 0.0), axis=-1, keepdims=True)
        lpos_slab = jnp.where(lane == k, lp, lpos_slab)
    lpos_ref[...] = lpos_slab.astype(I32)
    gate_ref[...] = gate_slab
    ne_ref[0] = jnp.broadcast_to(n_e, (8, E_PAD))
    cb_ref[0] = cnt_sc[...]
    cnt_sc[...] = cnt_sc[...] + n_e
    cnt_ref[...] = cnt_sc[...]


def _post_call(ya, yb, h, mod, wout, gain, wr, br, *, tm, n_tiles, y_block, h_block, row_fn):
    d = h.shape[1]
    n = n_tiles * tm
    tile = lambda w: pl.BlockSpec((tm, w), lambda i: (i, 0))
    const = lambda a: pl.BlockSpec(a.shape, lambda i: (0,) * a.ndim)
    return pl.pallas_call(
        functools.partial(_post_kernel, d=d, tm=tm, row_fn=row_fn),
        grid=(n_tiles,),
        in_specs=[pl.BlockSpec((tm, HALF), lambda i: (y_block(i), 0)),
                  pl.BlockSpec((tm, HALF), lambda i: (y_block(i), 0)),
                  pl.BlockSpec((tm, d), lambda i: (h_block(i), 0)),
                  const(mod), const(wout), const(gain), const(wr), const(br)],
        out_specs=[tile(d), tile(d), tile(E_PAD), tile(E_PAD),
                   pl.BlockSpec((1, 8, E_PAD), lambda i: (i, 0, 0)), pl.BlockSpec((1, 8, E_PAD), lambda i: (i, 0, 0)),
                   pl.BlockSpec((8, E_PAD), lambda i: (0, 0))],
        out_shape=[jax.ShapeDtypeStruct((n, d), F32), jax.ShapeDtypeStruct((n, d), F32),
                   jax.ShapeDtypeStruct((n, E_PAD), I32), jax.ShapeDtypeStruct((n, E_PAD), F32),
                   jax.ShapeDtypeStruct((n_tiles, 8, E_PAD), F32), jax.ShapeDtypeStruct((n_tiles, 8, E_PAD), F32),
                   jax.ShapeDtypeStruct((8, E_PAD), F32)],
        scratch_shapes=[pltpu.VMEM((8, E_PAD), F32)],
        compiler_params=_cparams("arbitrary"),
        name="post_mixer",
    )(ya, yb, h, mod, wout, gain, wr, br)


_TAB = 3 * N_EXPERTS


def _chunk_sizes(limit):
    return [1 << b for b in range(limit.bit_length() - 1, -1, -1)]


def _segment_copies(tab_ref, base, local, slot, grouped, sem, *, tm, to_grouped):
    for e in range(N_EXPERTS):
        g0 = tab_ref[base + e]
        n = tab_ref[base + N_EXPERTS + e]
        l0 = tab_ref[base + 2 * N_EXPERTS + e]
        for size in _chunk_sizes(tm):
            done = jnp.bitwise_and(n, -2 * size)

            @pl.when(jnp.bitwise_and(n, size) != 0)
            def _():
                loc = local.at[slot, pl.ds(l0 + done, size)]
                grp = grouped.at[pl.ds(g0 + done, size)]
                (pltpu.make_async_copy(loc, grp, sem) if to_grouped else pltpu.make_async_copy(grp, loc, sem)).start()


def _pad_zero_copies(ztab_ref, zeros, grouped, sem, *, tm, wait):
    for e in range(N_EXPERTS):
        g0 = ztab_ref[e]
        n = ztab_ref[N_EXPERTS + e]
        for size in _chunk_sizes(tm // 2):
            done = jnp.bitwise_and(n, -2 * size)

            @pl.when(jnp.bitwise_and(n, size) != 0)
            def _():
                cp = pltpu.make_async_copy(zeros.at[pl.ds(0, size)], grouped.at[pl.ds(g0 + done, size)], sem)
                cp.wait() if wait else cp.start()


def _lanes_to_tiles(ref, slot, x):
    for c in range(x.shape[1] // LANES):
        if slot is None:
            ref[:, c, :] = x[:, c * LANES:(c + 1) * LANES]
        else:
            ref[slot, :, c, :] = x[:, c * LANES:(c + 1) * LANES]


def _tiles_to_lanes(ref, slot):
    n_c = ref.shape[-2]
    parts = [ref[:, c, :] if slot is None else ref[slot, :, c, :] for c in range(n_c)]
    return jnp.concatenate(parts, axis=1)


def _dispatch_kernel(tab_ref, ztab_ref, f_ref, lpos_ref, xs_ref, zbuf, zeros, sems, zsem, *, tm):
    i = pl.program_id(0)
    n_tiles = pl.num_programs(0)
    slot = lax.rem(i, 2)
    rows = tm * TOP_K

    def tile_bytes(s):
        return pltpu.make_async_copy(zbuf.at[s], xs_ref.at[pl.ds(0, rows)], sems.at[s])

    @pl.when(i >= 2)
    def _():
        tile_bytes(slot).wait()

    lane = lax.broadcasted_iota(I32, (tm, rows), 1)
    lp = lpos_ref[...]
    sel = lane == lp[:, 0:1]
    for k in range(1, TOP_K):
        sel = jnp.logical_or(sel, lane == lp[:, k:k + 1])
    z = lax.dot_general(jnp.where(sel, 1.0, 0.0).astype(BF16), f_ref[...].astype(BF16), _TN,
                        preferred_element_type=F32)
    _lanes_to_tiles(zbuf, slot, z)
    _segment_copies(tab_ref, i * _TAB, zbuf, slot, xs_ref, sems.at[slot], tm=tm, to_grouped=True)

    @pl.when(i == n_tiles - 1)
    def _():
        zeros[...] = jnp.zeros_like(zeros)
        _pad_zero_copies(ztab_ref, zeros, xs_ref, zsem, tm=tm, wait=False)
        _pad_zero_copies(ztab_ref, zeros, xs_ref, zsem, tm=tm, wait=True)
        half = tm // 2

        def tail(j, start):
            cp = pltpu.make_async_copy(zeros, xs_ref.at[pl.ds(pl.multiple_of(j * half, half), half)], zsem)
            cp.start() if start else cp.wait()

        first, last = ztab_ref[2 * N_EXPERTS], xs_ref.shape[0] // half
        lax.fori_loop(first, last, lambda j, c: (tail(j, True), c)[1], 0)
        lax.fori_loop(first, last, lambda j, c: (tail(j, False), c)[1], 0)
        tile_bytes(slot).wait()
        tile_bytes(1 - slot).wait()


def _dispatch_call(tab, ztab, f, lpos, n_rows, *, tm):
    n, d = f.shape
    assert n // tm >= 2 and tm & (tm - 1) == 0
    n_c = d // LANES
    grid_spec = pltpu.PrefetchScalarGridSpec(
        num_scalar_prefetch=2,
        grid=(n // tm,),
        in_specs=[pl.BlockSpec((tm, d), lambda i, t, z: (i, 0)),
                  pl.BlockSpec((tm, E_PAD), lambda i, t, z: (i, 0))],
        out_specs=pl.BlockSpec(memory_space=pl.ANY),
        scratch_shapes=[pltpu.VMEM((2, tm * TOP_K, n_c, LANES), F32), pltpu.VMEM((tm // 2, n_c, LANES), F32),
                        pltpu.SemaphoreType.DMA((2,)), pltpu.SemaphoreType.DMA(())])
    return pl.pallas_call(
        functools.partial(_dispatch_kernel, tm=tm),
        grid_spec=grid_spec,
        out_shape=jax.ShapeDtypeStruct((n_rows, n_c, LANES), F32),
        compiler_params=_cparams("arbitrary"),
        name="moe_dispatch",
    )(tab, ztab, f, lpos)


def _expert_kernel(be_ref, nv_ref, xs_ref, wgu_ref, bgu_ref, wd_ref, bd_ref, y_ref, wgu_bf, wd_bf, *, de):
    i = pl.program_id(0)
    changed = jnp.logical_or(i == 0, be_ref[i] != be_ref[jnp.maximum(i - 1, 0)])

    @pl.when(changed)
    def _():
        wgu_bf[...] = wgu_ref[0, 0].astype(BF16)
        wd_bf[...] = wd_ref[0, 0].astype(BF16)

    @pl.when(i < nv_ref[0])
    def _():
        gu = _dot(_tiles_to_lanes(xs_ref, None).astype(BF16), wgu_bf[...]) + bgu_ref[0, 0]
        gate = jnp.minimum(gu[:, 0:de], SWIGLU_LIMIT)
        up = jnp.clip(gu[:, de:2 * de], -SWIGLU_LIMIT, SWIGLU_LIMIT)
        hid = (up + 1.0) * gate * jax.nn.sigmoid(SWIGLU_ALPHA * gate)
        _lanes_to_tiles(y_ref, None, _dot(hid.astype(BF16), wd_bf[...]) + bd_ref[0, 0])

    @pl.when(i >= nv_ref[0])
    def _():
        y_ref[...] = jnp.zeros_like(y_ref)


def _expert_call(blk_e, n_valid, xs, w_gu, b_gu, w_down, b_down, *, tm, layer):
    n_rows, n_c, _ = xs.shape
    d = n_c * LANES
    nl, ne, _, de2 = w_gu.shape
    de = de2 // 2
    grid_spec = pltpu.PrefetchScalarGridSpec(
        num_scalar_prefetch=2,
        grid=(n_rows // tm,),
        in_specs=[pl.BlockSpec((tm, n_c, LANES), lambda i, be, nv: (jnp.minimum(i, nv[0] - 1), 0, 0)),
                  pl.BlockSpec((1, 1, d, de2), lambda i, be, nv: (layer, be[i], 0, 0)),
                  pl.BlockSpec((1, 1, 1, de2), lambda i, be, nv: (layer, be[i], 0, 0)),
                  pl.BlockSpec((1, 1, de, d), lambda i, be, nv: (layer, be[i], 0, 0)),
                  pl.BlockSpec((1, 1, 1, d), lambda i, be, nv: (layer, be[i], 0, 0))],
        out_specs=pl.BlockSpec((tm, n_c, LANES), lambda i, be, nv: (i, 0, 0)),
        scratch_shapes=[pltpu.VMEM((d, de2), BF16), pltpu.VMEM((de, d), BF16)])
    return pl.pallas_call(
        functools.partial(_expert_kernel, de=de),
        grid_spec=grid_spec,
        out_shape=jax.ShapeDtypeStruct((n_rows, n_c, LANES), F32),
        compiler_params=_cparams("arbitrary"),
        name="moe_experts",
    )(blk_e, n_valid, xs, w_gu, b_gu.reshape(nl, ne, 1, de2), w_down, b_down.reshape(nl, ne, 1, d))


def _combine_kernel(tab_ref, lpos_ref, gate_ref, h_ref, mod_ref, y_ref, o_ref, ybuf, sems, *, d, tm, row_fn):
    i = pl.program_id(0)
    n_tiles = pl.num_programs(0)
    slot = lax.rem(i, 2)
    rows = tm * TOP_K

    def fetch(tile, s):
        _segment_copies(tab_ref, tile * _TAB, ybuf, s, y_ref, sems.at[s], tm=tm, to_grouped=False)

    @pl.when(i == 0)
    def _():
        fetch(0, 0)

    @pl.when(i + 1 < n_tiles)
    def _():
        fetch(i + 1, 1 - slot)

    pltpu.make_async_copy(y_ref.at[pl.ds(0, rows)], ybuf.at[slot], sems.at[slot]).wait()
    yb = _tiles_to_lanes(ybuf, slot).astype(BF16)
    lane = lax.broadcasted_iota(I32, (tm, rows), 1)
    lp = lpos_ref[...]
    gates = gate_ref[...]
    g = jnp.zeros((tm, rows), F32)
    for k in range(TOP_K):
        g = jnp.where(lane == lp[:, k:k + 1], gates[:, k:k + 1], g)
    g_hi, g_lo = _split_bf16(g, 2)
    acc = _dot(g_hi, yb) + _dot(g_lo, yb)
    r = row_fn(i)
    o_ref[...] = h_ref[...] + mod_ref[pl.ds(r, 1), 5 * d:6 * d] * acc


def _combine_call(tab, lpos, gates, hn, mod, y_rows, *, tm, row_fn):
    n, d = hn.shape
    n_c = d // LANES
    grid_spec = pltpu.PrefetchScalarGridSpec(
        num_scalar_prefetch=1,
        grid=(n // tm,),
        in_specs=[pl.BlockSpec((tm, E_PAD), lambda i, t: (i, 0)),
                  pl.BlockSpec((tm, E_PAD), lambda i, t: (i, 0)),
                  pl.BlockSpec((tm, d), lambda i, t: (i, 0)),
                  pl.BlockSpec(mod.shape, lambda i, t: (0, 0)),
                  pl.BlockSpec(memory_space=pl.ANY)],
        out_specs=pl.BlockSpec((tm, d), lambda i, t: (i, 0)),
        scratch_shapes=[pltpu.VMEM((2, tm * TOP_K, n_c, LANES), F32), pltpu.SemaphoreType.DMA((2,))])
    return pl.pallas_call(
        functools.partial(_combine_kernel, d=d, tm=tm, row_fn=row_fn),
        grid_spec=grid_spec,
        out_shape=jax.ShapeDtypeStruct((n, d), F32),
        compiler_params=_cparams("arbitrary"),
        name="moe_combine",
    )(tab, lpos, gates, hn, mod, y_rows)


def _moe(f, hn, mod, lpos, gates, ne_t, cb_t, counts, w_gu, b_gu, w_down, b_down, *, tm, row_fn, layer):
    n, _ = f.shape
    n_blocks = -(-n * TOP_K // tm) + N_EXPERTS
    cnt = counts[0, :N_EXPERTS].astype(I32)
    padded = (cnt + tm - 1) // tm * tm
    pad_end = jnp.cumsum(padded)
    pad_start = pad_end - padded
    ne = ne_t[:, 0, :N_EXPERTS].astype(I32)
    cb = cb_t[:, 0, :N_EXPERTS].astype(I32)
    tab = jnp.concatenate([pad_start[None] + cb, ne, jnp.cumsum(ne, axis=1) - ne], axis=1).reshape(-1)
    n_valid = (pad_end[-1] // tm).astype(I32)
    ztab = jnp.concatenate([pad_start + cnt, padded - cnt, 2 * n_valid[None]])
    blk = jnp.arange(n_blocks, dtype=I32)
    blk_e = jnp.sum((pad_end[None, :] <= (jnp.minimum(blk, n_valid - 1) * tm)[:, None]).astype(I32), axis=1)
    blk_e = jnp.minimum(blk_e, N_EXPERTS - 1)
    xs = _dispatch_call(tab, ztab, f, lpos, n_blocks * tm, tm=tm)
    y_rows = _expert_call(blk_e, n_valid.reshape(1), xs, w_gu, b_gu, w_down, b_down, tm=tm, layer=layer)
    return _combine_call(tab, lpos, gates, hn, mod, y_rows, tm=tm, row_fn=row_fn)


def _att_in_kernel(h_ref, mod_ref, gain_ref, w_ref, qg_gain, kg_gain, dq_gain, dk_gain, bd_ref,
                   cg_ref, sg_ref, cd_ref, sd_ref,
                   qg_ref, kg_ref, vg_ref, qd_ref, kd_ref, vd_ref, *, d, tpb, nct, nb):
    r = _unified_row(pl.program_id(0), tpb, nct, nb)
    a = _norm_mod(h_ref[...], gain_ref[...], mod_ref[pl.ds(r, 1), 0:d], mod_ref[pl.ds(r, 1), d:2 * d])
    p = _dot(a.astype(BF16), w_ref[...])
    cg, sg = cg_ref[...], sg_ref[...]

    def head_norm_rope(x, gain):
        y = x * lax.rsqrt(jnp.mean(x * x, axis=-1, keepdims=True) + NORM_EPS) * gain
        return _rope(y, cg, sg, GQA_HD).astype(BF16)

    c0 = 0
    for hh in range(GQA_HEADS):
        qg_ref[:, hh * GQA_HD:(hh + 1) * GQA_HD] = head_norm_rope(p[:, c0:c0 + GQA_HD], qg_gain[...])
        c0 += GQA_HD
    for hh in range(GQA_KV_HEADS):
        kg_ref[:, hh * GQA_HD:(hh + 1) * GQA_HD] = head_norm_rope(p[:, c0:c0 + GQA_HD], kg_gain[...])
        c0 += GQA_HD
    wv = GQA_KV_HEADS * GQA_HD
    vg_ref[...] = p[:, c0:c0 + wv].astype(BF16)
    c0 += wv

    def group_norm_rope(x, gain):
        ss = sum(_dot(piece, bd_ref[...]) for piece in _split_bf16(x * x, 2))
        y = x * lax.rsqrt(ss * (1.0 / DIFF_HD) + NORM_EPS) * gain
        return _rope(y, cd_ref[...], sd_ref[...], DIFF_HD).astype(BF16)

    qd_ref[...] = group_norm_rope(p[:, c0:c0 + HALF], dq_gain[...])
    c0 += HALF
    kd_ref[...] = group_norm_rope(p[:, c0:c0 + HALF], dk_gain[...])
    c0 += HALF
    vd_ref[...] = p[:, c0:c0 + HALF].astype(BF16)


def _att_in_call(h, mod, gain, w, qg_gain, kg_gain, dq_gain, dk_gain, bd, cg, sg, cd, sd, *, tm, tpb, nct, nb):
    nt, d = h.shape
    const = lambda a: pl.BlockSpec(a.shape, lambda i: (0,) * a.ndim)
    tab = lambda w_: pl.BlockSpec((tm, w_), lambda i: (lax.rem(i, tpb), 0))
    out = lambda w_: pl.BlockSpec((tm, w_), lambda i: (i, 0))
    shp = lambda w_: jax.ShapeDtypeStruct((nt, w_), BF16)
    wkv = GQA_KV_HEADS * GQA_HD
    return pl.pallas_call(
        functools.partial(_att_in_kernel, d=d, tpb=tpb, nct=nct, nb=nb),
        grid=(nt // tm,),
        in_specs=[pl.BlockSpec((tm, d), lambda i: (i, 0)), const(mod), const(gain), const(w),
                  const(qg_gain), const(kg_gain), const(dq_gain), const(dk_gain), const(bd),
                  tab(GQA_HD), tab(GQA_HD), tab(HALF), tab(HALF)],
        out_specs=[out(HALF), out(wkv), out(wkv), out(HALF), out(HALF), out(HALF)],
        out_shape=[shp(HALF), shp(wkv), shp(wkv), shp(HALF), shp(HALF), shp(HALF)],
        compiler_params=_cparams("arbitrary"),
        name="att_in_proj",
    )(h, mod, gain, w, qg_gain, kg_gain, dq_gain, dk_gain, bd, cg, sg, cd, sd)


def _gqa_kernel(q_ref, k_ref, v_ref, o_ref):
    s = lax.dot_general(q_ref[...], k_ref[...], _NT, preferred_element_type=F32) * (GQA_HD ** -0.5)
    e = jnp.exp(s - jnp.max(s, axis=-1, keepdims=True))
    o = _dot(e.astype(BF16), v_ref[...]) / jnp.sum(e, axis=-1, keepdims=True)
    o_ref[...] = o.astype(BF16)


def _gqa_call(qg, kg, vg, *, nb, tq, tot, nqt, lat0):
    group = GQA_HEADS // GQA_KV_HEADS
    upb = tot // tq
    return pl.pallas_call(
        _gqa_kernel,
        grid=(nb, GQA_HEADS, nqt),
        in_specs=[pl.BlockSpec((tq, GQA_HD), lambda b, h, q: (b * upb + lat0 + q, h)),
                  pl.BlockSpec((tot, GQA_HD), lambda b, h, q: (b, h // group)),
                  pl.BlockSpec((tot, GQA_HD), lambda b, h, q: (b, h // group))],
        out_specs=pl.BlockSpec((tq, GQA_HD), lambda b, h, q: (b * nqt + q, h)),
        out_shape=jax.ShapeDtypeStruct((nb * nqt * tq, HALF), BF16),
        compiler_params=_cparams("arbitrary", "arbitrary", "arbitrary"),
        name="gqa_attention",
    )(qg, kg, vg)


def _diff_kernel(q_ref, k_ref, v_ref, lam_ref, gain_ref, o_ref, *, lam_init):
    lp = lam_ref[...]
    lam = (jnp.exp(jnp.sum(lp[0:1] * lp[1:2], axis=-1, keepdims=True))
           - jnp.exp(jnp.sum(lp[2:3] * lp[3:4], axis=-1, keepdims=True)) + lam_init)
    q = q_ref[...]
    k = k_ref[...]
    lane = lax.broadcasted_iota(I32, q.shape, 1)
    zero = jnp.zeros_like(q)

    def probs(q_half):
        s = lax.dot_general(q_half, k, _NT, preferred_element_type=F32) * (DIFF_HD ** -0.5)
        e = jnp.exp(s - jnp.max(s, axis=-1, keepdims=True))
        return e, 1.0 / jnp.sum(e, axis=-1, keepdims=True)

    e1, inv1 = probs(jnp.where(lane < DIFF_HD, q, zero))
    e2, inv2 = probs(jnp.where(lane >= DIFF_HD, q, zero))
    a = e1 * inv1 - e2 * (lam * inv2)
    o = _dot(a.astype(BF16), v_ref[...])
    y = o * lax.rsqrt(jnp.mean(o * o, axis=-1, keepdims=True) + NORM_EPS) * gain_ref[...]
    o_ref[...] = (y * (1.0 - lam_init)).astype(BF16)


def _diff_call(qd, kd, vd, lam_params, gain, *, nb, tq, tot, nqt, lat0, lam_init):
    upb = tot // tq
    hd2 = 2 * DIFF_HD
    return pl.pallas_call(
        functools.partial(_diff_kernel, lam_init=lam_init),
        grid=(nb, DIFF_HEADS, nqt),
        in_specs=[pl.BlockSpec((tq, hd2), lambda b, h, q: (b * upb + lat0 + q, h)),
                  pl.BlockSpec((tot, hd2), lambda b, h, q: (b, h)),
                  pl.BlockSpec((tot, hd2), lambda b, h, q: (b, h)),
                  pl.BlockSpec(lam_params.shape, lambda b, h, q: (0, 0)),
                  pl.BlockSpec(gain.shape, lambda b, h, q: (0, 0))],
        out_specs=pl.BlockSpec((tq, hd2), lambda b, h, q: (b * nqt + q, h)),
        out_shape=jax.ShapeDtypeStruct((nb * nqt * tq, HALF), BF16),
        compiler_params=_cparams("arbitrary", "arbitrary", "arbitrary"),
        name="diff_attention",
    )(qd, kd, vd, lam_params, gain)


def _deinterleave(hd):
    return np.concatenate([np.arange(0, hd, 2), np.arange(1, hd, 2)])


def _head_perm(n_heads, hd):
    return np.concatenate([h * hd + _deinterleave(hd) for h in range(n_heads)])


def _rope_tables(n_lat, n_ctx, hd, reps):
    t = np.arange(n_lat)
    row, col = (t // GRID_W).astype(np.float64), (t % GRID_W).astype(np.float64)
    axis_dim = hd // 2
    inv_freq = ROPE_THETA ** (-np.arange(0, axis_dim, 2, dtype=np.float64) / axis_dim)
    ang = np.concatenate([row[:, None] * inv_freq, col[:, None] * inv_freq], axis=-1)
    cos = np.concatenate([np.ones((n_ctx, hd // 2)), np.cos(ang)], axis=0)
    sin = np.concatenate([np.zeros((n_ctx, hd // 2)), np.sin(ang)], axis=0)
    c = np.tile(np.concatenate([cos, cos], axis=1), (1, reps))
    s = np.tile(np.concatenate([-sin, sin], axis=1), (1, reps))
    return jnp.asarray(c, F32), jnp.asarray(s, F32)


def _retention_tables():
    gam = 1.0 - 2.0 ** (-5.0 - np.arange(RET_HEADS, dtype=np.float64))
    gams = (gam, gam[::-1])
    c = np.arange(CHUNK, dtype=np.float64)
    diff = c[:, None] - c[None, :]
    dmat = np.zeros((2, RET_HEADS, CHUNK, CHUNK))
    qs = np.zeros((2, CHUNK, RET_HEADS * RET_DK))
    ks = np.zeros((2, CHUNK, RET_HEADS * RET_DK))
    g64 = np.zeros((2, 8, RET_HEADS * RET_DK))
    for d in range(2):
        for h in range(RET_HEADS):
            g = gams[d][h]
            sl = slice(h * RET_DK, (h + 1) * RET_DK)
            if d == 0:
                dmat[d, h] = np.where(diff >= 0, g ** np.maximum(diff, 0), 0.0)
                qs[d, :, sl] = (g ** (c + 1))[:, None]
                ks[d, :, sl] = (g ** (CHUNK - 1 - c))[:, None]
            else:
                dmat[d, h] = np.where(diff <= 0, g ** np.maximum(-diff, 0), 0.0)
                qs[d, :, sl] = (g ** (CHUNK - c))[:, None]
                ks[d, :, sl] = (g ** c)[:, None]
            g64[d, :, sl] = g ** CHUNK
    return tuple(jnp.asarray(a, F32) for a in (dmat, qs, ks, g64))


def _plan(nb, n_lat, n_ctx):
    tm = 256 if (n_ctx % 256 == 0 and n_lat % 256 == 0) else 128
    assert n_ctx % tm == 0 and n_lat % tm == 0 and tm % CHUNK == 0 and n_lat % GRID_W == 0
    tq = tm
    return tm, tq


def kernel(x, c, ctx, c_ctx, norm_mix, norm_ffn, w_ada, b_ada, rec_w_in, rec_lb_logits, rec_w_out, rec_hg_gain,
           rec_ret_gain, att_w_in, att_w_out, att_q_gain, att_k_gain, diff_q_gain, diff_k_gain, diff_lambda,
           diff_gain, w_router, b_router, w_gu, b_gu, w_down, b_down):
    nb, n_lat, d = x.shape
    n_ctx = ctx.shape[1]
    assert w_ada.shape[0] == 2, "two layers: recurrent mixer then attention mixer"
    tm, tq = _plan(nb, n_lat, n_ctx)
    tot = n_ctx + n_lat
    tpb, nct = tot // tm, n_ctx // tm
    nt = nb * tot
    nql = n_lat // tm

    unified_row = functools.partial(_unified_row, tpb=tpb, nct=nct, nb=nb)
    latent_row = lambda i: lax.div(i, nql)
    latent_block = lambda i: lax.div(i, nql) * tpb + nct + lax.rem(i, nql)
    ident = lambda i: i

    n_cond = -(-(nb + 1) // 8) * 8
    cond = jnp.concatenate([c, c_ctx[None], jnp.zeros((n_cond - nb - 1, d), F32)], axis=0)
    mod = _ada_call(cond, w_ada, b_ada)

    wr = jnp.pad(w_router, ((0, 0), (0, 0), (0, E_PAD - N_EXPERTS)))
    br = jnp.pad(b_router, ((0, 0), (0, E_PAD - N_EXPERTS)), constant_values=NEG_BIG)[:, None, :]

    h = jnp.concatenate([ctx, x], axis=1).reshape(nt, d)

    lb = jnp.cumsum(jax.nn.softmax(rec_lb_logits.astype(F32), axis=0), axis=0)[0]
    w_in = rec_w_in[0]
    c_rq = HG_HEADS * HG_DK * 3 + HG_HEADS * HG_DV * 2
    nrk = RET_HEADS * RET_DK
    perm = _head_perm(RET_HEADS, RET_DK)
    w_in = jnp.concatenate([w_in[:, :c_rq], w_in[:, c_rq + perm], w_in[:, c_rq + nrk + perm] * (RET_DK ** -0.5),
                            w_in[:, c_rq + 2 * nrk:]], axis=1).astype(BF16)
    cs_r, sn_r = _rope_tables(n_lat, n_ctx, RET_DK, 2 * RET_HEADS)
    dmat, qs, ks, g64 = _retention_tables()
    p0 = _inproj_call(h, mod[0], norm_mix[0][None], w_in, tm=tm, tpb=tpb, nct=nct, nb=nb)
    ohf, orf, ohb, orb = _rec_call(p0, lb, cs_r, sn_r, dmat, qs, ks, g64, nb=nb, tc=tm, tpb=tpb, nct=nct)
    yh, yr = _rec_merge_call(ohf, orf, ohb, orb, p0, rec_hg_gain[0][None], rec_ret_gain[0][None], tm=tm)
    hn, f, *routing = _post_call(
        yh, yr, h, mod[0], rec_w_out[0].astype(BF16), norm_ffn[0][None], wr[0], br[0],
        tm=tm, n_tiles=nt // tm, y_block=ident, h_block=ident, row_fn=unified_row)
    h = _moe(f, hn, mod[0], *routing, w_gu, b_gu, w_down, b_down, tm=tm, row_fn=unified_row, layer=0)

    pg, pd = _head_perm(GQA_HEADS, GQA_HD), _head_perm(2 * DIFF_HEADS, DIFF_HD)
    w_in = att_w_in[0]
    o_gk = GQA_HEADS * GQA_HD
    o_gv = o_gk + GQA_KV_HEADS * GQA_HD
    o_dq = o_gv + GQA_KV_HEADS * GQA_HD
    o_dk = o_dq + HALF
    o_dv = o_dk + HALF
    w_in = jnp.concatenate([w_in[:, pg], w_in[:, o_gk + pg[:GQA_KV_HEADS * GQA_HD]], w_in[:, o_gv:o_dq],
                            w_in[:, o_dq + pd], w_in[:, o_dk + pd], w_in[:, o_dv:]], axis=1).astype(BF16)
    dg, dd = _deinterleave(GQA_HD), _deinterleave(DIFF_HD)
    cg, sg = _rope_tables(n_lat, n_ctx, GQA_HD, 1)
    cd, sd = _rope_tables(n_lat, n_ctx, DIFF_HD, 2 * DIFF_HEADS)
    bd = jnp.asarray(np.kron(np.eye(2 * DIFF_HEADS), np.ones((DIFF_HD, DIFF_HD))), BF16)
    qg, kg, vg, qd, kd, vd = _att_in_call(
        h, mod[1], norm_mix[1][None], w_in, att_q_gain[0][dg][None], att_k_gain[0][dg][None],
        jnp.tile(diff_q_gain[0][dd], 2 * DIFF_HEADS)[None], jnp.tile(diff_k_gain[0][dd], 2 * DIFF_HEADS)[None],
        bd, cg, sg, cd, sd, tm=tm, tpb=tpb, nct=nct, nb=nb)
    nqt = n_lat // tq
    yg = _gqa_call(qg, kg, vg, nb=nb, tq=tq, tot=tot, nqt=nqt, lat0=n_ctx // tq)
    lam_init = 0.8 - 0.6 * math.exp(-0.3 * 1)
    yd = _diff_call(qd, kd, vd, diff_lambda[0], diff_gain[0][None], nb=nb, tq=tq, tot=tot, nqt=nqt,
                    lat0=n_ctx // tq, lam_init=lam_init)
    hn, f, *routing = _post_call(
        yg, yd, h, mod[1], att_w_out[0].astype(BF16), norm_ffn[1][None], wr[1], br[1],
        tm=tm, n_tiles=nb * nql, y_block=ident, h_block=latent_block, row_fn=latent_row)
    out = _moe(f, hn, mod[1], *routing, w_gu, b_gu, w_down, b_down, tm=tm, row_fn=latent_row, layer=1)
    return out.reshape(nb, n_lat, d)
```

```python
import functools
import math

import numpy as np
import jax
import jax.numpy as jnp
from jax import lax
from jax.experimental import pallas as pl
from jax.experimental.pallas import tpu as pltpu

F32 = jnp.float32
BF16 = jnp.bfloat16
I32 = jnp.int32

GRID_W = 64
HG_HEADS, HG_DK, HG_DV = 4, 128, 128
RET_HEADS, RET_DK, RET_DV = 4, 64, 128
GQA_HEADS, GQA_KV_HEADS, GQA_HD = 4, 2, 128
DIFF_HEADS, DIFF_HD = 4, 64
N_EXPERTS, TOP_K = 32, 4
SWIGLU_LIMIT, SWIGLU_ALPHA = 7.0, 1.702
CHUNK = 64
ROPE_THETA = 10000.0
NORM_EPS = 1e-6

LANES = 128
E_PAD = LANES
NEG_BIG = -1e30
V7X_VMEM_LIMIT = 56 * 1024 * 1024
HALF = 512

_NT = (((1,), (1,)), ((), ()))
_TN = (((0,), (0,)), ((), ()))


def _cparams(*sem):
    return pltpu.CompilerParams(dimension_semantics=sem, vmem_limit_bytes=V7X_VMEM_LIMIT)


def _silu(x):
    return x * jax.nn.sigmoid(x)


def _dot(a, b):
    return jnp.dot(a, b, preferred_element_type=F32)


def _split_bf16(x, terms):
    out = []
    for _ in range(terms):
        p = x.astype(BF16)
        out.append(p)
        x = x - p.astype(F32)
    return out


def _norm_mod(x, gain, shift, scale):
    y = x * lax.rsqrt(jnp.mean(x * x, axis=-1, keepdims=True) + NORM_EPS) * gain
    return y * (1.0 + scale) + shift


def _unified_row(i, tpb, nct, nb):
    return jnp.where(lax.rem(i, tpb) < nct, nb, lax.div(i, tpb))


def _ada_kernel(cond_ref, w_ref, b_ref, o_ref):
    s = _silu(cond_ref[...]).astype(BF16)
    o_ref[0] = _dot(s, w_ref[0].astype(BF16)) + b_ref[0]


def _ada_call(cond, w_ada, b_ada):
    n_layers, d, d6 = w_ada.shape
    r = cond.shape[0]
    tn = d6 // 4
    return pl.pallas_call(
        _ada_kernel,
        grid=(n_layers, d6 // tn),
        in_specs=[pl.BlockSpec((r, d), lambda l, n: (0, 0)),
                  pl.BlockSpec((1, d, tn), lambda l, n: (l, 0, n)),
                  pl.BlockSpec((1, 1, tn), lambda l, n: (l, 0, n))],
        out_specs=pl.BlockSpec((1, r, tn), lambda l, n: (l, 0, n)),
        out_shape=jax.ShapeDtypeStruct((n_layers, r, d6), F32),
        compiler_params=_cparams("arbitrary", "arbitrary"),
        name="ada_modulation",
    )(cond, w_ada, b_ada.reshape(n_layers, 1, d6))


def _inproj_kernel(h_ref, mod_ref, gain_ref, w_ref, o_ref, *, d, tpb, nct, nb):
    r = _unified_row(pl.program_id(0), tpb, nct, nb)
    a = _norm_mod(h_ref[...], gain_ref[...], mod_ref[pl.ds(r, 1), 0:d], mod_ref[pl.ds(r, 1), d:2 * d])
    o_ref[...] = _dot(a.astype(BF16), w_ref[...])


def _inproj_call(h, mod, gain, w, *, tm, tpb, nct, nb):
    nt, d = h.shape
    n_out = w.shape[1]
    return pl.pallas_call(
        functools.partial(_inproj_kernel, d=d, tpb=tpb, nct=nct, nb=nb),
        grid=(nt // tm,),
        in_specs=[pl.BlockSpec((tm, d), lambda i: (i, 0)),
                  pl.BlockSpec(mod.shape, lambda i: (0, 0)),
                  pl.BlockSpec((1, d), lambda i: (0, 0)),
                  pl.BlockSpec((d, n_out), lambda i: (0, 0))],
        out_specs=pl.BlockSpec((tm, n_out), lambda i: (i, 0)),
        out_shape=jax.ShapeDtypeStruct((nt, n_out), F32),
        compiler_params=_cparams("arbitrary"),
        name="rec_in_proj",
    )(h, mod, gain, w)


def _swap_halves(x, group):
    if group == LANES:
        return pltpu.roll(x, LANES // 2, 1)
    lane = lax.broadcasted_iota(I32, x.shape, 1)
    half = group // 2
    return jnp.where(lax.rem(lane, group) < half, pltpu.roll(x, LANES - half, 1), pltpu.roll(x, half, 1))


def _rope(x, cos, sin, group):
    parts = [_swap_halves(x[:, s:s + LANES], group) for s in range(0, x.shape[1], LANES)]
    rot = parts[0] if len(parts) == 1 else jnp.concatenate(parts, axis=1)
    return x * cos + rot * sin


def _rec_direction(d, r0, hq, hf, hi, rqk, rv, cs, sn, lb_ref, dmat_ref, qs_ref, ks_ref, g64_ref,
                   ohg, oret, st_hg, st_ret, tri_bf, tri_mask):
    rows = pl.ds(r0, CHUNK)
    lbv = lb_ref[d:d + 1, :]
    f = lbv + (1.0 - lbv) * jax.nn.sigmoid(hf[rows, :])
    logf = jnp.log(f)
    kk = 1.0 - f
    b = sum(_dot(tri_bf, p) for p in _split_bf16(logf, 3))
    tot = b[CHUNK - 1:CHUNK, :] if d == 0 else b[0:1, :]
    q = _silu(hq[rows, :]) * (HG_DK ** -0.5)
    q_in = (q * jnp.exp(b)).astype(BF16)
    k_in = (kk * jnp.exp(-b)).astype(BF16)
    k_st = (kk * jnp.exp(tot - b)).astype(BF16)
    dec = jnp.exp(tot)
    v = hi[rows, :].astype(BF16)
    for h in range(HG_HEADS):
        sl = slice(h * HG_DK, (h + 1) * HG_DK)
        att = jnp.where(tri_mask, lax.dot_general(q_in[:, sl], k_in[:, sl], _NT, preferred_element_type=F32), 0.0)
        s_t = st_hg[d, h]
        o = _dot(att.astype(BF16), v[:, sl]) + lax.dot_general(
            q_in[:, sl], s_t.astype(BF16), _NT, preferred_element_type=F32)
        ohg[rows, sl] = o
        st_hg[d, h] = s_t * dec[:, sl] + lax.dot_general(v[:, sl], k_st[:, sl], _TN, preferred_element_type=F32)
    xr = _rope(rqk[rows, :], cs[rows, :], sn[rows, :], RET_DK)
    nq = RET_HEADS * RET_DK
    q_r = xr[:, 0:nq]
    k_r = xr[:, nq:2 * nq]
    q_b = q_r.astype(BF16)
    k_b = k_r.astype(BF16)
    q_sc = (q_r * qs_ref[d]).astype(BF16)
    k_sc = (k_r * ks_ref[d]).astype(BF16)
    vv = rv[rows, :].astype(BF16)
    for h in range(RET_HEADS):
        sk = slice(h * RET_DK, (h + 1) * RET_DK)
        sv = slice(h * RET_DV, (h + 1) * RET_DV)
        att = lax.dot_general(q_b[:, sk], k_b[:, sk], _NT, preferred_element_type=F32) * dmat_ref[d, h]
        s_t = st_ret[d, h]
        o = _dot(att.astype(BF16), vv[:, sv]) + lax.dot_general(
            q_sc[:, sk], s_t.astype(BF16), _NT, preferred_element_type=F32)
        oret[rows, sv] = o
        st_ret[d, h] = s_t * g64_ref[d, 0:1, sk] + lax.dot_general(
            vv[:, sv], k_sc[:, sk], _TN, preferred_element_type=F32)


def _rec_kernel(hq_f, hf_f, hi_f, rqk_f, rv_f, cs_f, sn_f,
                hq_b, hf_b, hi_b, rqk_b, rv_b, cs_b, sn_b,
                lb_ref, dmat_ref, qs_ref, ks_ref, g64_ref,
                ohg_f, oret_f, ohg_b, oret_b, st_hg, st_ret, *, tc):
    @pl.when(pl.program_id(1) == 0)
    def _():
        st_hg[...] = jnp.zeros_like(st_hg)
        st_ret[...] = jnp.zeros_like(st_ret)

    nch = tc // CHUNK
    row = lax.broadcasted_iota(I32, (CHUNK, CHUNK), 0)
    col = lax.broadcasted_iota(I32, (CHUNK, CHUNK), 1)
    mask_f = col <= row
    mask_b = col >= row
    tri_f = mask_f.astype(F32).astype(BF16)
    tri_b = mask_b.astype(F32).astype(BF16)

    def body(ci, carry):
        common = (lb_ref, dmat_ref, qs_ref, ks_ref, g64_ref)
        _rec_direction(0, pl.multiple_of(ci * CHUNK, CHUNK), hq_f, hf_f, hi_f, rqk_f, rv_f, cs_f, sn_f,
                       *common, ohg_f, oret_f, st_hg, st_ret, tri_f, mask_f)
        _rec_direction(1, pl.multiple_of((nch - 1 - ci) * CHUNK, CHUNK), hq_b, hf_b, hi_b, rqk_b, rv_b,
                       cs_b, sn_b, *common, ohg_b, oret_b, st_hg, st_ret, tri_b, mask_b)
        return carry

    lax.fori_loop(0, nch, body, 0, unroll=True)


def _rec_call(p0, lb, cs, sn, dmat, qs, ks, g64, *, nb, tc, tpb, nct):
    nt = p0.shape[0]

    def fwd(b, j):
        return j

    def bwd(b, j):
        return jnp.where(j < nct, nct - 1 - j, tpb + nct - 1 - j)

    def pspec(col, pos):
        return pl.BlockSpec((tc, HALF), lambda b, j: (b * tpb + pos(b, j), col))

    def tspec(pos):
        return pl.BlockSpec((tc, HALF), lambda b, j: (pos(b, j), 0))

    def whole(a):
        return pl.BlockSpec(a.shape, lambda b, j: (0,) * a.ndim)

    in_specs = ([pspec(0, fwd), pspec(1, fwd), pspec(3, fwd), pspec(5, fwd), pspec(6, fwd), tspec(fwd), tspec(fwd)]
                + [pspec(0, bwd), pspec(2, bwd), pspec(3, bwd), pspec(5, bwd), pspec(6, bwd), tspec(bwd), tspec(bwd)]
                + [whole(a) for a in (lb, dmat, qs, ks, g64)])
    out_f = pl.BlockSpec((tc, HALF), lambda b, j: (b * tpb + fwd(b, j), 0))
    out_b = pl.BlockSpec((tc, HALF), lambda b, j: (b * tpb + bwd(b, j), 0))
    o_shape = jax.ShapeDtypeStruct((nt, HALF), F32)
    return pl.pallas_call(
        functools.partial(_rec_kernel, tc=tc),
        grid=(nb, tpb),
        in_specs=in_specs,
        out_specs=[out_f, out_f, out_b, out_b],
        out_shape=[o_shape] * 4,
        scratch_shapes=[pltpu.VMEM((2, HG_HEADS, HG_DV, HG_DK), F32),
                        pltpu.VMEM((2, RET_HEADS, RET_DV, RET_DK), F32)],
        compiler_params=_cparams("arbitrary", "arbitrary"),
        name="rec_scan",
    )(p0, p0, p0, p0, p0, cs, sn, p0, p0, p0, p0, p0, cs, sn, lb, dmat, qs, ks, g64)


def _rec_merge_kernel(ohf, ohb, orf, orb, hg_ref, rg_ref, hgain, rgain, yh_ref, yr_ref):
    oh = ohf[...] + ohb[...]
    orr = orf[...] + orb[...]
    hg = hg_ref[...]
    rg = rg_ref[...]
    for h in range(HG_HEADS):
        sl = slice(h * LANES, (h + 1) * LANES)
        o = oh[:, sl]
        y = o * lax.rsqrt(jnp.mean(o * o, axis=-1, keepdims=True) + NORM_EPS) * hgain[...]
        yh_ref[:, sl] = (y * _silu(hg[:, sl])).astype(BF16)
        o = orr[:, sl]
        oc = o - jnp.mean(o, axis=-1, keepdims=True)
        y = oc * lax.rsqrt(jnp.mean(oc * oc, axis=-1, keepdims=True) + NORM_EPS) * rgain[...]
        yr_ref[:, sl] = (y * _silu(rg[:, sl])).astype(BF16)


def _rec_merge_call(ohf, orf, ohb, orb, p0, hgain, rgain, *, tm):
    nt = p0.shape[0]
    blk = pl.BlockSpec((tm, HALF), lambda i: (i, 0))
    vec = pl.BlockSpec((1, LANES), lambda i: (0, 0))
    y_shape = jax.ShapeDtypeStruct((nt, HALF), BF16)
    return pl.pallas_call(
        _rec_merge_kernel,
        grid=(nt // tm,),
        in_specs=[blk, blk, blk, blk,
                  pl.BlockSpec((tm, HALF), lambda i: (i, 4)), pl.BlockSpec((tm, HALF), lambda i: (i, 7)), vec, vec],
        out_specs=[blk, blk],
        out_shape=[y_shape, y_shape],
        compiler_params=_cparams("arbitrary"),
        name="rec_merge",
    )(ohf, ohb, orf, orb, p0, p0, hgain, rgain)


def _post_kernel(ya_ref, yb_ref, h_ref, mod_ref, wout_ref, gain_ref, wr_ref, br_ref,
                 hn_ref, f_ref, lpos_ref, gate_ref, ne_ref, cb_ref, cnt_ref, cnt_sc, *, d, tm, row_fn):
    i = pl.program_id(0)

    @pl.when(i == 0)
    def _():
        cnt_sc[...] = jnp.zeros_like(cnt_sc)

    r = row_fn(i)
    y = jnp.concatenate([ya_ref[...], yb_ref[...]], axis=1)
    hn = h_ref[...] + mod_ref[pl.ds(r, 1), 2 * d:3 * d] * _dot(y, wout_ref[...])
    hn_ref[...] = hn
    f = _norm_mod(hn, gain_ref[...], mod_ref[pl.ds(r, 1), 3 * d:4 * d], mod_ref[pl.ds(r, 1), 4 * d:5 * d])
    f_ref[...] = f
    f_hi, f_lo = _split_bf16(f, 2)
    w_hi, w_lo = _split_bf16(wr_ref[...], 2)
    logits = _dot(f_hi, w_hi) + _dot(f_hi, w_lo) + _dot(f_lo, w_hi) + br_ref[...]
    lane = lax.broadcasted_iota(I32, (tm, E_PAD), 1)
    vals, idxs = [], []
    work = logits
    for _ in range(TOP_K):
        m = jnp.max(work, axis=-1, keepdims=True)
        sel = jnp.min(jnp.where(work == m, lane, E_PAD), axis=-1, keepdims=True)
        vals.append(m)
        idxs.append(sel)
        work = jnp.where(lane == sel, -jnp.inf, work)
    exps = [jnp.exp(v - vals[0]) for v in vals]
    inv = 1.0 / sum(exps)
    onehot = sum((lane == s).astype(F32) for s in idxs)
    n_e = jnp.sum(onehot, axis=0, keepdims=True)
    rr = lax.broadcasted_iota(I32, (tm, tm), 0)
    cc = lax.broadcasted_iota(I32, (tm, tm), 1)
    within = _dot((cc < rr).astype(F32).astype(BF16), onehot.astype(BF16))
    ea = lax.broadcasted_iota(I32, (E_PAD, E_PAD), 0)
    eb = lax.broadcasted_iota(I32, (E_PAD, E_PAD), 1)
    before = _dot(jnp.broadcast_to(n_e, (8, E_PAD)).astype(BF16), (ea < eb).astype(F32).astype(BF16))[0:1]
    base = within + before
    lpos_slab = jnp.zeros((tm, E_PAD), F32)
    gate_slab = jnp.zeros((tm, E_PAD), F32)
    for k in range(TOP_K):
        gate_slab = jnp.where(lane == k, exps[k] * inv, gate_slab)
        lp = jnp.sum(jnp.where(lane == idxs[k], base, 0.0), axis=-1, keepdims=True)
        lpos_slab = jnp.where(lane == k, lp, lpos_slab)
    lpos_ref[...] = lpos_slab.astype(I32)
    gate_ref[...] = gate_slab
    ne_ref[0] = jnp.broadcast_to(n_e, (8, E_PAD))
    cb_ref[0] = cnt_sc[...]
    cnt_sc[...] = cnt_sc[...] + n_e
    cnt_ref[...] = cnt_sc[...]


def _post_call(ya, yb, h, mod, wout, gain, wr, br, *, tm, n_tiles, y_block, h_block, row_fn):
    d = h.shape[1]
    n = n_tiles * tm
    tile = lambda w: pl.BlockSpec((tm, w), lambda i: (i, 0))
    const = lambda a: pl.BlockSpec(a.shape, lambda i: (0,) * a.ndim)
    return pl.pallas_call(
        functools.partial(_post_kernel, d=d, tm=tm, row_fn=row_fn),
        grid=(n_tiles,),
        in_specs=[pl.BlockSpec((tm, HALF), lambda i: (y_block(i), 0)),
                  pl.BlockSpec((tm, HALF), lambda i: (y_block(i), 0)),
                  pl.BlockSpec((tm, d), lambda i: (h_block(i), 0)),
                  const(mod), const(wout), const(gain), const(wr), const(br)],
        out_specs=[tile(d), tile(d), tile(E_PAD), tile(E_PAD),
                   pl.BlockSpec((1, 8, E_PAD), lambda i: (i, 0, 0)), pl.BlockSpec((1, 8, E_PAD), lambda i: (i, 0, 0)),
                   pl.BlockSpec((8, E_PAD), lambda i: (0, 0))],
        out_shape=[jax.ShapeDtypeStruct((n, d), F32), jax.ShapeDtypeStruct((n, d), F32),
                   jax.ShapeDtypeStruct((n, E_PAD), I32), jax.ShapeDtypeStruct((n, E_PAD), F32),
                   jax.ShapeDtypeStruct((n_tiles, 8, E_PAD), F32), jax.ShapeDtypeStruct((n_tiles, 8, E_PAD), F32),
                   jax.ShapeDtypeStruct((8, E_PAD), F32)],
        scratch_shapes=[pltpu.VMEM((8, E_PAD), F32)],
        compiler_params=_cparams("arbitrary"),
        name="post_mixer",
    )(ya, yb, h, mod, wout, gain, wr, br)


_TAB = 3 * N_EXPERTS


def _chunk_sizes(limit):
    return [1 << b for b in range(limit.bit_length() - 1, -1, -1)]


def _token_rows(start, count, pitch):
    return pl.ds(pl.multiple_of(start * pitch, pitch), count * pitch)


def _segment_copies(tab_ref, base, local, slot, grouped, sem, *, tm, pitch, to_grouped):
    for e in range(N_EXPERTS):
        g0 = tab_ref[base + e]
        n = tab_ref[base + N_EXPERTS + e]
        l0 = tab_ref[base + 2 * N_EXPERTS + e]
        for size in _chunk_sizes(tm):
            done = jnp.bitwise_and(n, -2 * size)

            @pl.when(jnp.bitwise_and(n, size) != 0)
            def _():
                loc = local.at[slot, _token_rows(l0 + done, size, pitch)]
                grp = grouped.at[_token_rows(g0 + done, size, pitch)]
                (pltpu.make_async_copy(loc, grp, sem) if to_grouped else pltpu.make_async_copy(grp, loc, sem)).start()


def _pad_zero_copies(ztab_ref, zeros, grouped, sem, *, tm, pitch, wait):
    for e in range(N_EXPERTS):
        g0 = ztab_ref[e]
        n = ztab_ref[N_EXPERTS + e]
        for size in _chunk_sizes(tm // 2):
            done = jnp.bitwise_and(n, -2 * size)

            @pl.when(jnp.bitwise_and(n, size) != 0)
            def _():
                cp = pltpu.make_async_copy(zeros.at[pl.ds(0, size * pitch)],
                                           grouped.at[_token_rows(g0 + done, size, pitch)], sem)
                cp.wait() if wait else cp.start()


def _lanes_to_tiles(ref, x):
    rows, pitch = x.shape[0], x.shape[1] // LANES
    for c in range(pitch):
        ref[pl.ds(c, rows, stride=pitch), :] = x[:, c * LANES:(c + 1) * LANES]


def _tiles_to_lanes(ref, pitch):
    rows = ref.shape[0] // pitch
    return jnp.concatenate([ref[pl.ds(c, rows, stride=pitch), :] for c in range(pitch)], axis=1)


def _dispatch_kernel(tab_ref, ztab_ref, f_ref, lpos_ref, xs_ref, zbuf, zeros, sems, zsem, *, tm, pitch):
    i = pl.program_id(0)
    n_tiles = pl.num_programs(0)
    slot = lax.rem(i, 2)
    rows = tm * TOP_K

    def tile_bytes(s):
        return pltpu.make_async_copy(zbuf.at[s], xs_ref.at[pl.ds(0, rows * pitch)], sems.at[s])

    @pl.when(i >= 2)
    def _():
        tile_bytes(slot).wait()

    lane = lax.broadcasted_iota(I32, (tm, rows), 1)
    lp = lpos_ref[...]
    sel = lane == lp[:, 0:1]
    for k in range(1, TOP_K):
        sel = jnp.logical_or(sel, lane == lp[:, k:k + 1])
    z = lax.dot_general(jnp.where(sel, 1.0, 0.0).astype(BF16), f_ref[...].astype(BF16), _TN,
                        preferred_element_type=F32)
    _lanes_to_tiles(zbuf.at[slot], z)
    _segment_copies(tab_ref, i * _TAB, zbuf, slot, xs_ref, sems.at[slot], tm=tm, pitch=pitch, to_grouped=True)

    @pl.when(i == n_tiles - 1)
    def _():
        zeros[...] = jnp.zeros_like(zeros)
        _pad_zero_copies(ztab_ref, zeros, xs_ref, zsem, tm=tm, pitch=pitch, wait=False)
        _pad_zero_copies(ztab_ref, zeros, xs_ref, zsem, tm=tm, pitch=pitch, wait=True)
        half = tm // 2

        def tail(j, start):
            cp = pltpu.make_async_copy(zeros, xs_ref.at[_token_rows(j * half, half, pitch)], zsem)
            cp.start() if start else cp.wait()

        first, last = ztab_ref[2 * N_EXPERTS], xs_ref.shape[0] // (half * pitch)
        lax.fori_loop(first, last, lambda j, c: (tail(j, True), c)[1], 0)
        lax.fori_loop(first, last, lambda j, c: (tail(j, False), c)[1], 0)
        tile_bytes(slot).wait()
        tile_bytes(1 - slot).wait()


def _dispatch_call(tab, ztab, f, lpos, n_rows, *, tm):
    n, d = f.shape
    assert n // tm >= 2 and tm & (tm - 1) == 0
    pitch = d // LANES
    grid_spec = pltpu.PrefetchScalarGridSpec(
        num_scalar_prefetch=2,
        grid=(n // tm,),
        in_specs=[pl.BlockSpec((tm, d), lambda i, t, z: (i, 0)),
                  pl.BlockSpec((tm, E_PAD), lambda i, t, z: (i, 0))],
        out_specs=pl.BlockSpec(memory_space=pl.ANY),
        scratch_shapes=[pltpu.VMEM((2, tm * TOP_K * pitch, LANES), F32), pltpu.VMEM((tm // 2 * pitch, LANES), F32),
                        pltpu.SemaphoreType.DMA((2,)), pltpu.SemaphoreType.DMA(())])
    return pl.pallas_call(
        functools.partial(_dispatch_kernel, tm=tm, pitch=pitch),
        grid_spec=grid_spec,
        out_shape=jax.ShapeDtypeStruct((n_rows * pitch, LANES), F32),
        compiler_params=_cparams("arbitrary"),
        name="moe_dispatch",
    )(tab, ztab, f, lpos)


def _expert_kernel(be_ref, nv_ref, xs_ref, wgu_ref, bgu_ref, wd_ref, bd_ref, y_ref, wgu_bf, wd_bf, *, de, pitch):
    i = pl.program_id(0)
    changed = jnp.logical_or(i == 0, be_ref[i] != be_ref[jnp.maximum(i - 1, 0)])

    @pl.when(changed)
    def _():
        wgu_bf[...] = wgu_ref[0, 0].astype(BF16)
        wd_bf[...] = wd_ref[0, 0].astype(BF16)

    @pl.when(i < nv_ref[0])
    def _():
        gu = _dot(_tiles_to_lanes(xs_ref, pitch).astype(BF16), wgu_bf[...]) + bgu_ref[0, 0]
        gate = jnp.minimum(gu[:, 0:de], SWIGLU_LIMIT)
        up = jnp.clip(gu[:, de:2 * de], -SWIGLU_LIMIT, SWIGLU_LIMIT)
        hid = (up + 1.0) * gate * jax.nn.sigmoid(SWIGLU_ALPHA * gate)
        _lanes_to_tiles(y_ref, _dot(hid.astype(BF16), wd_bf[...]) + bd_ref[0, 0])

    @pl.when(i >= nv_ref[0])
    def _():
        y_ref[...] = jnp.zeros_like(y_ref)


def _expert_call(blk_e, n_valid, xs, w_gu, b_gu, w_down, b_down, *, tm, layer):
    nl, ne, d, de2 = w_gu.shape
    pitch = d // LANES
    n_rows = xs.shape[0] // pitch
    de = de2 // 2
    grid_spec = pltpu.PrefetchScalarGridSpec(
        num_scalar_prefetch=2,
        grid=(n_rows // tm,),
        in_specs=[pl.BlockSpec((tm * pitch, LANES), lambda i, be, nv: (jnp.minimum(i, nv[0] - 1), 0)),
                  pl.BlockSpec((1, 1, d, de2), lambda i, be, nv: (layer, be[i], 0, 0)),
                  pl.BlockSpec((1, 1, 1, de2), lambda i, be, nv: (layer, be[i], 0, 0)),
                  pl.BlockSpec((1, 1, de, d), lambda i, be, nv: (layer, be[i], 0, 0)),
                  pl.BlockSpec((1, 1, 1, d), lambda i, be, nv: (layer, be[i], 0, 0))],
        out_specs=pl.BlockSpec((tm * pitch, LANES), lambda i, be, nv: (i, 0)),
        scratch_shapes=[pltpu.VMEM((d, de2), BF16), pltpu.VMEM((de, d), BF16)])
    return pl.pallas_call(
        functools.partial(_expert_kernel, de=de, pitch=pitch),
        grid_spec=grid_spec,
        out_shape=jax.ShapeDtypeStruct((n_rows * pitch, LANES), F32),
        compiler_params=_cparams("arbitrary"),
        name="moe_experts",
    )(blk_e, n_valid, xs, w_gu, b_gu.reshape(nl, ne, 1, de2), w_down, b_down.reshape(nl, ne, 1, d))


def _combine_kernel(tab_ref, lpos_ref, gate_ref, h_ref, mod_ref, y_ref, o_ref, ybuf, sems, *, d, tm, row_fn):
    i = pl.program_id(0)
    n_tiles = pl.num_programs(0)
    slot = lax.rem(i, 2)
    rows = tm * TOP_K
    pitch = d // LANES

    def fetch(tile, s):
        _segment_copies(tab_ref, tile * _TAB, ybuf, s, y_ref, sems.at[s], tm=tm, pitch=pitch, to_grouped=False)

    @pl.when(i == 0)
    def _():
        fetch(0, 0)

    @pl.when(i + 1 < n_tiles)
    def _():
        fetch(i + 1, 1 - slot)

    pltpu.make_async_copy(y_ref.at[pl.ds(0, rows * pitch)], ybuf.at[slot], sems.at[slot]).wait()
    yb = _tiles_to_lanes(ybuf.at[slot], pitch).astype(BF16)
    lane = lax.broadcasted_iota(I32, (tm, rows), 1)
    lp = lpos_ref[...]
    gates = gate_ref[...]
    g = jnp.zeros((tm, rows), F32)
    for k in range(TOP_K):
        g = jnp.where(lane == lp[:, k:k + 1], gates[:, k:k + 1], g)
    g_hi, g_lo = _split_bf16(g, 2)
    acc = _dot(g_hi, yb) + _dot(g_lo, yb)
    r = row_fn(i)
    o_ref[...] = h_ref[...] + mod_ref[pl.ds(r, 1), 5 * d:6 * d] * acc


def _combine_call(tab, lpos, gates, hn, mod, y_rows, *, tm, row_fn):
    n, d = hn.shape
    pitch = d // LANES
    grid_spec = pltpu.PrefetchScalarGridSpec(
        num_scalar_prefetch=1,
        grid=(n // tm,),
        in_specs=[pl.BlockSpec((tm, E_PAD), lambda i, t: (i, 0)),
                  pl.BlockSpec((tm, E_PAD), lambda i, t: (i, 0)),
                  pl.BlockSpec((tm, d), lambda i, t: (i, 0)),
                  pl.BlockSpec(mod.shape, lambda i, t: (0, 0)),
                  pl.BlockSpec(memory_space=pl.ANY)],
        out_specs=pl.BlockSpec((tm, d), lambda i, t: (i, 0)),
        scratch_shapes=[pltpu.VMEM((2, tm * TOP_K * pitch, LANES), F32), pltpu.SemaphoreType.DMA((2,))])
    return pl.pallas_call(
        functools.partial(_combine_kernel, d=d, tm=tm, row_fn=row_fn),
        grid_spec=grid_spec,
        out_shape=jax.ShapeDtypeStruct((n, d), F32),
        compiler_params=_cparams("arbitrary"),
        name="moe_combine",
    )(tab, lpos, gates, hn, mod, y_rows)


def _moe(f, hn, mod, lpos, gates, ne_t, cb_t, counts, w_gu, b_gu, w_down, b_down, *, tm, row_fn, layer):
    n, _ = f.shape
    n_blocks = -(-n * TOP_K // tm) + N_EXPERTS
    cnt = counts[0, :N_EXPERTS].astype(I32)
    padded = (cnt + tm - 1) // tm * tm
    pad_end = jnp.cumsum(padded)
    pad_start = pad_end - padded
    ne = ne_t[:, 0, :N_EXPERTS].astype(I32)
    cb = cb_t[:, 0, :N_EXPERTS].astype(I32)
    tab = jnp.concatenate([pad_start[None] + cb, ne, jnp.cumsum(ne, axis=1) - ne], axis=1).reshape(-1)
    n_valid = (pad_end[-1] // tm).astype(I32)
    ztab = jnp.concatenate([pad_start + cnt, padded - cnt, 2 * n_valid[None]])
    blk = jnp.arange(n_blocks, dtype=I32)
    blk_e = jnp.sum((pad_end[None, :] <= (jnp.minimum(blk, n_valid - 1) * tm)[:, None]).astype(I32), axis=1)
    blk_e = jnp.minimum(blk_e, N_EXPERTS - 1)
    xs = _dispatch_call(tab, ztab, f, lpos, n_blocks * tm, tm=tm)
    y_rows = _expert_call(blk_e, n_valid.reshape(1), xs, w_gu, b_gu, w_down, b_down, tm=tm, layer=layer)
    return _combine_call(tab, lpos, gates, hn, mod, y_rows, tm=tm, row_fn=row_fn)


def _att_in_kernel(h_ref, mod_ref, gain_ref, w_ref, qg_gain, kg_gain, dq_gain, dk_gain, bd_ref,
                   cg_ref, sg_ref, cd_ref, sd_ref,
                   qg_ref, kg_ref, vg_ref, qd_ref, kd_ref, vd_ref, *, d, tpb, nct, nb):
    r = _unified_row(pl.program_id(0), tpb, nct, nb)
    a = _norm_mod(h_ref[...], gain_ref[...], mod_ref[pl.ds(r, 1), 0:d], mod_ref[pl.ds(r, 1), d:2 * d])
    p = _dot(a.astype(BF16), w_ref[...])
    cg, sg = cg_ref[...], sg_ref[...]

    def head_norm_rope(x, gain):
        y = x * lax.rsqrt(jnp.mean(x * x, axis=-1, keepdims=True) + NORM_EPS) * gain
        return _rope(y, cg, sg, GQA_HD).astype(BF16)

    c0 = 0
    for hh in range(GQA_HEADS):
        qg_ref[:, hh * GQA_HD:(hh + 1) * GQA_HD] = head_norm_rope(p[:, c0:c0 + GQA_HD], qg_gain[...])
        c0 += GQA_HD
    for hh in range(GQA_KV_HEADS):
        kg_ref[:, hh * GQA_HD:(hh + 1) * GQA_HD] = head_norm_rope(p[:, c0:c0 + GQA_HD], kg_gain[...])
        c0 += GQA_HD
    wv = GQA_KV_HEADS * GQA_HD
    vg_ref[...] = p[:, c0:c0 + wv].astype(BF16)
    c0 += wv

    def group_norm_rope(x, gain):
        ss = sum(_dot(piece, bd_ref[...]) for piece in _split_bf16(x * x, 2))
        y = x * lax.rsqrt(ss * (1.0 / DIFF_HD) + NORM_EPS) * gain
        return _rope(y, cd_ref[...], sd_ref[...], DIFF_HD).astype(BF16)

    qd_ref[...] = group_norm_rope(p[:, c0:c0 + HALF], dq_gain[...])
    c0 += HALF
    kd_ref[...] = group_norm_rope(p[:, c0:c0 + HALF], dk_gain[...])
    c0 += HALF
    vd_ref[...] = p[:, c0:c0 + HALF].astype(BF16)


def _att_in_call(h, mod, gain, w, qg_gain, kg_gain, dq_gain, dk_gain, bd, cg, sg, cd, sd, *, tm, tpb, nct, nb):
    nt, d = h.shape
    const = lambda a: pl.BlockSpec(a.shape, lambda i: (0,) * a.ndim)
    tab = lambda w_: pl.BlockSpec((tm, w_), lambda i: (lax.rem(i, tpb), 0))
    out = lambda w_: pl.BlockSpec((tm, w_), lambda i: (i, 0))
    shp = lambda w_: jax.ShapeDtypeStruct((nt, w_), BF16)
    wkv = GQA_KV_HEADS * GQA_HD
    return pl.pallas_call(
        functools.partial(_att_in_kernel, d=d, tpb=tpb, nct=nct, nb=nb),
        grid=(nt // tm,),
        in_specs=[pl.BlockSpec((tm, d), lambda i: (i, 0)), const(mod), const(gain), const(w),
                  const(qg_gain), const(kg_gain), const(dq_gain), const(dk_gain), const(bd),
                  tab(GQA_HD), tab(GQA_HD), tab(HALF), tab(HALF)],
        out_specs=[out(HALF), out(wkv), out(wkv), out(HALF), out(HALF), out(HALF)],
        out_shape=[shp(HALF), shp(wkv), shp(wkv), shp(HALF), shp(HALF), shp(HALF)],
        compiler_params=_cparams("arbitrary"),
        name="att_in_proj",
    )(h, mod, gain, w, qg_gain, kg_gain, dq_gain, dk_gain, bd, cg, sg, cd, sd)


def _softmax_parts(q, k):
    s = lax.dot_general(q, k, _NT, preferred_element_type=F32)
    e = jnp.exp2(s - jnp.max(s, axis=-1, keepdims=True))
    return e, 1.0 / jnp.sum(e, axis=-1, keepdims=True)


def _gqa_kernel(q_ref, k_ref, v_ref, o_ref, *, group):
    for g in range(group):
        sl = slice(g * GQA_HD, (g + 1) * GQA_HD)
        e, inv = _softmax_parts(q_ref[:, sl], k_ref[...])
        o_ref[:, sl] = (_dot(e.astype(BF16), v_ref[...]) * inv).astype(BF16)


def _gqa_call(qg, kg, vg, *, nb, tq, tot, nqt, lat0):
    group = GQA_HEADS // GQA_KV_HEADS
    upb = tot // tq
    return pl.pallas_call(
        functools.partial(_gqa_kernel, group=group),
        grid=(nb, GQA_KV_HEADS, nqt),
        in_specs=[pl.BlockSpec((tq, group * GQA_HD), lambda b, h, q: (b * upb + lat0 + q, h)),
                  pl.BlockSpec((tot, GQA_HD), lambda b, h, q: (b, h)),
                  pl.BlockSpec((tot, GQA_HD), lambda b, h, q: (b, h))],
        out_specs=pl.BlockSpec((tq, group * GQA_HD), lambda b, h, q: (b * nqt + q, h)),
        out_shape=jax.ShapeDtypeStruct((nb * nqt * tq, HALF), BF16),
        compiler_params=_cparams("arbitrary", "arbitrary", "arbitrary"),
        name="gqa_attention",
    )(qg, kg, vg)


def _diff_kernel(q_ref, k_ref, v_ref, lam_ref, gain_ref, o_ref, *, lam_init):
    lp = lam_ref[...]
    lam = (jnp.exp(jnp.sum(lp[0:1] * lp[1:2], axis=-1, keepdims=True))
           - jnp.exp(jnp.sum(lp[2:3] * lp[3:4], axis=-1, keepdims=True)) + lam_init)
    hd2 = 2 * DIFF_HD
    for j in range(q_ref.shape[1] // hd2):
        sl = slice(j * hd2, (j + 1) * hd2)
        q = q_ref[:, sl]
        k = k_ref[:, sl]
        lane = lax.broadcasted_iota(I32, q.shape, 1)
        zero = jnp.zeros_like(q)
        e1, inv1 = _softmax_parts(jnp.where(lane < DIFF_HD, q, zero), k)
        e2, inv2 = _softmax_parts(jnp.where(lane >= DIFF_HD, q, zero), k)
        a = e1 * inv1 - e2 * (lam * inv2)
        o = _dot(a.astype(BF16), v_ref[:, sl])
        y = o * lax.rsqrt(jnp.mean(o * o, axis=-1, keepdims=True) + NORM_EPS) * gain_ref[...]
        o_ref[:, sl] = (y * (1.0 - lam_init)).astype(BF16)


def _diff_call(qd, kd, vd, lam_params, gain, *, nb, tq, tot, nqt, lat0, lam_init):
    upb = tot // tq
    hd2 = 2 * (2 * DIFF_HD)
    return pl.pallas_call(
        functools.partial(_diff_kernel, lam_init=lam_init),
        grid=(nb, HALF // hd2, nqt),
        in_specs=[pl.BlockSpec((tq, hd2), lambda b, h, q: (b * upb + lat0 + q, h)),
                  pl.BlockSpec((tot, hd2), lambda b, h, q: (b, h)),
                  pl.BlockSpec((tot, hd2), lambda b, h, q: (b, h)),
                  pl.BlockSpec(lam_params.shape, lambda b, h, q: (0, 0)),
                  pl.BlockSpec(gain.shape, lambda b, h, q: (0, 0))],
        out_specs=pl.BlockSpec((tq, hd2), lambda b, h, q: (b * nqt + q, h)),
        out_shape=jax.ShapeDtypeStruct((nb * nqt * tq, HALF), BF16),
        compiler_params=_cparams("arbitrary", "arbitrary", "arbitrary"),
        name="diff_attention",
    )(qd, kd, vd, lam_params, gain)


def _deinterleave(hd):
    return np.concatenate([np.arange(0, hd, 2), np.arange(1, hd, 2)])


def _head_perm(n_heads, hd):
    return np.concatenate([h * hd + _deinterleave(hd) for h in range(n_heads)])


def _rope_tables(n_lat, n_ctx, hd, reps):
    t = np.arange(n_lat)
    row, col = (t // GRID_W).astype(np.float64), (t % GRID_W).astype(np.float64)
    axis_dim = hd // 2
    inv_freq = ROPE_THETA ** (-np.arange(0, axis_dim, 2, dtype=np.float64) / axis_dim)
    ang = np.concatenate([row[:, None] * inv_freq, col[:, None] * inv_freq], axis=-1)
    cos = np.concatenate([np.ones((n_ctx, hd // 2)), np.cos(ang)], axis=0)
    sin = np.concatenate([np.zeros((n_ctx, hd // 2)), np.sin(ang)], axis=0)
    c = np.tile(np.concatenate([cos, cos], axis=1), (1, reps))
    s = np.tile(np.concatenate([-sin, sin], axis=1), (1, reps))
    return jnp.asarray(c, F32), jnp.asarray(s, F32)


def _retention_tables():
    gam = 1.0 - 2.0 ** (-5.0 - np.arange(RET_HEADS, dtype=np.float64))
    gams = (gam, gam[::-1])
    c = np.arange(CHUNK, dtype=np.float64)
    diff = c[:, None] - c[None, :]
    dmat = np.zeros((2, RET_HEADS, CHUNK, CHUNK))
    qs = np.zeros((2, CHUNK, RET_HEADS * RET_DK))
    ks = np.zeros((2, CHUNK, RET_HEADS * RET_DK))
    g64 = np.zeros((2, 8, RET_HEADS * RET_DK))
    for d in range(2):
        for h in range(RET_HEADS):
            g = gams[d][h]
            sl = slice(h * RET_DK, (h + 1) * RET_DK)
            if d == 0:
                dmat[d, h] = np.where(diff >= 0, g ** np.maximum(diff, 0), 0.0)
                qs[d, :, sl] = (g ** (c + 1))[:, None]
                ks[d, :, sl] = (g ** (CHUNK - 1 - c))[:, None]
            else:
                dmat[d, h] = np.where(diff <= 0, g ** np.maximum(-diff, 0), 0.0)
                qs[d, :, sl] = (g ** (CHUNK - c))[:, None]
                ks[d, :, sl] = (g ** c)[:, None]
            g64[d, :, sl] = g ** CHUNK
    return tuple(jnp.asarray(a, F32) for a in (dmat, qs, ks, g64))


def _plan(nb, n_lat, n_ctx):
    tm = 256 if (n_ctx % 256 == 0 and n_lat % 256 == 0) else 128
    assert n_ctx % tm == 0 and n_lat % tm == 0 and tm % CHUNK == 0 and n_lat % GRID_W == 0
    tq = tm
    return tm, tq


def kernel(x, c, ctx, c_ctx, norm_mix, norm_ffn, w_ada, b_ada, rec_w_in, rec_lb_logits, rec_w_out, rec_hg_gain,
           rec_ret_gain, att_w_in, att_w_out, att_q_gain, att_k_gain, diff_q_gain, diff_k_gain, diff_lambda,
           diff_gain, w_router, b_router, w_gu, b_gu, w_down, b_down):
    nb, n_lat, d = x.shape
    n_ctx = ctx.shape[1]
    assert w_ada.shape[0] == 2, "two layers: recurrent mixer then attention mixer"
    tm, tq = _plan(nb, n_lat, n_ctx)
    tot = n_ctx + n_lat
    tpb, nct = tot // tm, n_ctx // tm
    nt = nb * tot
    nql = n_lat // tm

    unified_row = functools.partial(_unified_row, tpb=tpb, nct=nct, nb=nb)
    latent_row = lambda i: lax.div(i, nql)
    latent_block = lambda i: lax.div(i, nql) * tpb + nct + lax.rem(i, nql)
    ident = lambda i: i

    n_cond = -(-(nb + 1) // 8) * 8
    cond = jnp.concatenate([c, c_ctx[None], jnp.zeros((n_cond - nb - 1, d), F32)], axis=0)
    mod = _ada_call(cond, w_ada, b_ada)

    wr = jnp.pad(w_router, ((0, 0), (0, 0), (0, E_PAD - N_EXPERTS)))
    br = jnp.pad(b_router, ((0, 0), (0, E_PAD - N_EXPERTS)), constant_values=NEG_BIG)[:, None, :]

    h = jnp.concatenate([ctx, x], axis=1).reshape(nt, d)

    lb = jnp.cumsum(jax.nn.softmax(rec_lb_logits.astype(F32), axis=0), axis=0)[0]
    w_in = rec_w_in[0]
    c_rq = HG_HEADS * HG_DK * 3 + HG_HEADS * HG_DV * 2
    nrk = RET_HEADS * RET_DK
    perm = _head_perm(RET_HEADS, RET_DK)
    w_in = jnp.concatenate([w_in[:, :c_rq], w_in[:, c_rq + perm], w_in[:, c_rq + nrk + perm] * (RET_DK ** -0.5),
                            w_in[:, c_rq + 2 * nrk:]], axis=1).astype(BF16)
    cs_r, sn_r = _rope_tables(n_lat, n_ctx, RET_DK, 2 * RET_HEADS)
    dmat, qs, ks, g64 = _retention_tables()
    p0 = _inproj_call(h, mod[0], norm_mix[0][None], w_in, tm=tm, tpb=tpb, nct=nct, nb=nb)
    ohf, orf, ohb, orb = _rec_call(p0, lb, cs_r, sn_r, dmat, qs, ks, g64, nb=nb, tc=tm, tpb=tpb, nct=nct)
    yh, yr = _rec_merge_call(ohf, orf, ohb, orb, p0, rec_hg_gain[0][None], rec_ret_gain[0][None], tm=tm)
    hn, f, *routing = _post_call(
        yh, yr, h, mod[0], rec_w_out[0].astype(BF16), norm_ffn[0][None], wr[0], br[0],
        tm=tm, n_tiles=nt // tm, y_block=ident, h_block=ident, row_fn=unified_row)
    h = _moe(f, hn, mod[0], *routing, w_gu, b_gu, w_down, b_down, tm=tm, row_fn=unified_row, layer=0)

    pg, pd = _head_perm(GQA_HEADS, GQA_HD), _head_perm(2 * DIFF_HEADS, DIFF_HD)
    w_in = att_w_in[0]
    o_gk = GQA_HEADS * GQA_HD
    o_gv = o_gk + GQA_KV_HEADS * GQA_HD
    o_dq = o_gv + GQA_KV_HEADS * GQA_HD
    o_dk = o_dq + HALF
    o_dv = o_dk + HALF
    w_in = jnp.concatenate([w_in[:, pg], w_in[:, o_gk + pg[:GQA_KV_HEADS * GQA_HD]], w_in[:, o_gv:o_dq],
                            w_in[:, o_dq + pd], w_in[:, o_dk + pd], w_in[:, o_dv:]], axis=1).astype(BF16)
    dg, dd = _deinterleave(GQA_HD), _deinterleave(DIFF_HD)
    cg, sg = _rope_tables(n_lat, n_ctx, GQA_HD, 1)
    cd, sd = _rope_tables(n_lat, n_ctx, DIFF_HD, 2 * DIFF_HEADS)
    bd = jnp.asarray(np.kron(np.eye(2 * DIFF_HEADS), np.ones((DIFF_HD, DIFF_HD))), BF16)
    log2e = math.log2(math.e)
    qg, kg, vg, qd, kd, vd = _att_in_call(
        h, mod[1], norm_mix[1][None], w_in,
        (att_q_gain[0][dg] * (GQA_HD ** -0.5 * log2e))[None], att_k_gain[0][dg][None],
        jnp.tile(diff_q_gain[0][dd] * (DIFF_HD ** -0.5 * log2e), 2 * DIFF_HEADS)[None],
        jnp.tile(diff_k_gain[0][dd], 2 * DIFF_HEADS)[None],
        bd, cg, sg, cd, sd, tm=tm, tpb=tpb, nct=nct, nb=nb)
    nqt = n_lat // tq
    yg = _gqa_call(qg, kg, vg, nb=nb, tq=tq, tot=tot, nqt=nqt, lat0=n_ctx // tq)
    lam_init = 0.8 - 0.6 * math.exp(-0.3 * 1)
    yd = _diff_call(qd, kd, vd, diff_lambda[0], diff_gain[0][None], nb=nb, tq=tq, tot=tot, nqt=nqt,
                    lat0=n_ctx // tq, lam_init=lam_init)
    hn, f, *routing = _post_call(
        yg, yd, h, mod[1], att_w_out[0].astype(BF16), norm_ffn[1][None], wr[1], br[1],
        tm=tm, n_tiles=nb * nql, y_block=ident, h_block=latent_block, row_fn=latent_row)
    out = _moe(f, hn, mod[1], *routing, w_gu, b_gu, w_down, b_down, tm=tm, row_fn=latent_row, layer=1)
    return out.reshape(nb, n_lat, d)
```

```python
import functools
import math

import numpy as np
import jax
import jax.numpy as jnp
from jax import lax
from jax.experimental import pallas as pl
from jax.experimental.pallas import tpu as pltpu

F32 = jnp.float32
BF16 = jnp.bfloat16
I32 = jnp.int32

GRID_W = 64
HG_HEADS, HG_DK, HG_DV = 4, 128, 128
RET_HEADS, RET_DK, RET_DV = 4, 64, 128
GQA_HEADS, GQA_KV_HEADS, GQA_HD = 4, 2, 128
DIFF_HEADS, DIFF_HD = 4, 64
N_EXPERTS, TOP_K = 32, 4
SWIGLU_LIMIT, SWIGLU_ALPHA = 7.0, 1.702
CHUNK = 64
ROPE_THETA = 10000.0
NORM_EPS = 1e-6

LANES = 128
E_PAD = LANES
NEG_BIG = -1e30
V7X_VMEM_LIMIT = 56 * 1024 * 1024
HALF = 512

_NT = (((1,), (1,)), ((), ()))
_TN = (((0,), (0,)), ((), ()))


def _cparams(*sem):
    return pltpu.CompilerParams(dimension_semantics=sem, vmem_limit_bytes=V7X_VMEM_LIMIT)


def _silu(x):
    return x * jax.nn.sigmoid(x)


def _dot(a, b):
    return jnp.dot(a, b, preferred_element_type=F32)


def _split_bf16(x, terms):
    out = []
    for _ in range(terms):
        p = x.astype(BF16)
        out.append(p)
        x = x - p.astype(F32)
    return out


def _norm_mod(x, gain, shift, scale):
    y = x * lax.rsqrt(jnp.mean(x * x, axis=-1, keepdims=True) + NORM_EPS) * gain
    return y * (1.0 + scale) + shift


def _unified_row(i, tpb, nct, nb):
    return jnp.where(lax.rem(i, tpb) < nct, nb, lax.div(i, tpb))


def _ada_kernel(cond_ref, w_ref, b_ref, o_ref):
    s = _silu(cond_ref[...]).astype(BF16)
    o_ref[0] = _dot(s, w_ref[0].astype(BF16)) + b_ref[0]


def _ada_call(cond, w_ada, b_ada):
    n_layers, d, d6 = w_ada.shape
    r = cond.shape[0]
    tn = d6 // 4
    return pl.pallas_call(
        _ada_kernel,
        grid=(n_layers, d6 // tn),
        in_specs=[pl.BlockSpec((r, d), lambda l, n: (0, 0)),
                  pl.BlockSpec((1, d, tn), lambda l, n: (l, 0, n)),
                  pl.BlockSpec((1, 1, tn), lambda l, n: (l, 0, n))],
        out_specs=pl.BlockSpec((1, r, tn), lambda l, n: (l, 0, n)),
        out_shape=jax.ShapeDtypeStruct((n_layers, r, d6), F32),
        compiler_params=_cparams("arbitrary", "arbitrary"),
        name="ada_modulation",
    )(cond, w_ada, b_ada.reshape(n_layers, 1, d6))


def _inproj_kernel(h_ref, mod_ref, gain_ref, w_ref, o_ref, *, d, tpb, nct, nb):
    r = _unified_row(pl.program_id(0), tpb, nct, nb)
    a = _norm_mod(h_ref[...], gain_ref[...], mod_ref[pl.ds(r, 1), 0:d], mod_ref[pl.ds(r, 1), d:2 * d])
    o_ref[...] = _dot(a.astype(BF16), w_ref[...])


def _inproj_call(h, mod, gain, w, *, tm, tpb, nct, nb):
    nt, d = h.shape
    n_out = w.shape[1]
    return pl.pallas_call(
        functools.partial(_inproj_kernel, d=d, tpb=tpb, nct=nct, nb=nb),
        grid=(nt // tm,),
        in_specs=[pl.BlockSpec((tm, d), lambda i: (i, 0)),
                  pl.BlockSpec(mod.shape, lambda i: (0, 0)),
                  pl.BlockSpec((1, d), lambda i: (0, 0)),
                  pl.BlockSpec((d, n_out), lambda i: (0, 0))],
        out_specs=pl.BlockSpec((tm, n_out), lambda i: (i, 0)),
        out_shape=jax.ShapeDtypeStruct((nt, n_out), F32),
        compiler_params=_cparams("arbitrary"),
        name="rec_in_proj",
    )(h, mod, gain, w)


def _swap_halves(x, group):
    if group == LANES:
        return pltpu.roll(x, LANES // 2, 1)
    lane = lax.broadcasted_iota(I32, x.shape, 1)
    half = group // 2
    return jnp.where(lax.rem(lane, group) < half, pltpu.roll(x, LANES - half, 1), pltpu.roll(x, half, 1))


def _rope(x, cos, sin, group):
    parts = [_swap_halves(x[:, s:s + LANES], group) for s in range(0, x.shape[1], LANES)]
    rot = parts[0] if len(parts) == 1 else jnp.concatenate(parts, axis=1)
    return x * cos + rot * sin


def _hgrn_chunk(d, r0, hq, hf, hi, lb_ref, ohg, st_hg, tri_bf, tri_mask):
    rows = pl.ds(r0, CHUNK)
    lbv = lb_ref[d:d + 1, :]
    f = lbv + (1.0 - lbv) * jax.nn.sigmoid(hf[rows, :])
    logf = jnp.log(f)
    kk = 1.0 - f
    b = sum(_dot(tri_bf, p) for p in _split_bf16(logf, 3))
    tot = b[CHUNK - 1:CHUNK, :] if d == 0 else b[0:1, :]
    q = _silu(hq[rows, :]) * (HG_DK ** -0.5)
    q_in = (q * jnp.exp(b)).astype(BF16)
    k_in = (kk * jnp.exp(-b)).astype(BF16)
    k_st = (kk * jnp.exp(tot - b)).astype(BF16)
    dec = jnp.exp(tot)
    v = hi[rows, :].astype(BF16)
    for h in range(HG_HEADS):
        sl = slice(h * HG_DK, (h + 1) * HG_DK)
        att = jnp.where(tri_mask, lax.dot_general(q_in[:, sl], k_in[:, sl], _NT, preferred_element_type=F32), 0.0)
        s_t = st_hg[d, h]
        o = _dot(att.astype(BF16), v[:, sl]) + lax.dot_general(
            q_in[:, sl], s_t.astype(BF16), _NT, preferred_element_type=F32)
        ohg[rows, sl] = o
        st_hg[d, h] = s_t * dec[:, sl] + lax.dot_general(v[:, sl], k_st[:, sl], _TN, preferred_element_type=F32)


def _retention_tile(d, rqk, rv, cs, sn, dmat_ref, qs_ref, ks_ref, gpow_ref, oret, st_ret):
    xr = _rope(rqk[...], cs[...], sn[...], RET_DK)
    nq = RET_HEADS * RET_DK
    q_r = xr[:, 0:nq]
    k_r = xr[:, nq:2 * nq]
    q_b = q_r.astype(BF16)
    k_b = k_r.astype(BF16)
    q_sc = (q_r * qs_ref[d]).astype(BF16)
    k_sc = (k_r * ks_ref[d]).astype(BF16)
    vv = rv[...].astype(BF16)
    for h in range(RET_HEADS):
        sk = slice(h * RET_DK, (h + 1) * RET_DK)
        sv = slice(h * RET_DV, (h + 1) * RET_DV)
        att = lax.dot_general(q_b[:, sk], k_b[:, sk], _NT, preferred_element_type=F32) * dmat_ref[d, h]
        s_t = st_ret[d, h]
        o = _dot(att.astype(BF16), vv[:, sv]) + lax.dot_general(
            q_sc[:, sk], s_t.astype(BF16), _NT, preferred_element_type=F32)
        oret[:, sv] = o
        st_ret[d, h] = s_t * gpow_ref[d, 0:1, sk] + lax.dot_general(
            vv[:, sv], k_sc[:, sk], _TN, preferred_element_type=F32)


def _rec_kernel(hq_f, hf_f, hi_f, rqk_f, rv_f, cs_f, sn_f,
                hq_b, hf_b, hi_b, rqk_b, rv_b, cs_b, sn_b,
                lb_ref, dmat_ref, qs_ref, ks_ref, g64_ref,
                ohg_f, oret_f, ohg_b, oret_b, st_hg, st_ret, *, tc):
    @pl.when(pl.program_id(1) == 0)
    def _():
        st_hg[...] = jnp.zeros_like(st_hg)
        st_ret[...] = jnp.zeros_like(st_ret)

    nch = tc // CHUNK
    row = lax.broadcasted_iota(I32, (CHUNK, CHUNK), 0)
    col = lax.broadcasted_iota(I32, (CHUNK, CHUNK), 1)
    mask_f = col <= row
    mask_b = col >= row
    tri_f = mask_f.astype(F32).astype(BF16)
    tri_b = mask_b.astype(F32).astype(BF16)

    ret_tables = (dmat_ref, qs_ref, ks_ref, g64_ref)
    _retention_tile(0, rqk_f, rv_f, cs_f, sn_f, *ret_tables, oret_f, st_ret)
    _retention_tile(1, rqk_b, rv_b, cs_b, sn_b, *ret_tables, oret_b, st_ret)
    for ci in range(nch):
        _hgrn_chunk(0, ci * CHUNK, hq_f, hf_f, hi_f, lb_ref, ohg_f, st_hg, tri_f, mask_f)
        _hgrn_chunk(1, (nch - 1 - ci) * CHUNK, hq_b, hf_b, hi_b, lb_ref, ohg_b, st_hg, tri_b, mask_b)


def _rec_call(p0, lb, cs, sn, dmat, qs, ks, g64, *, nb, tc, tpb, nct):
    nt = p0.shape[0]

    def fwd(b, j):
        return j

    def bwd(b, j):
        return jnp.where(j < nct, nct - 1 - j, tpb + nct - 1 - j)

    def pspec(col, pos):
        return pl.BlockSpec((tc, HALF), lambda b, j: (b * tpb + pos(b, j), col))

    def tspec(pos):
        return pl.BlockSpec((tc, HALF), lambda b, j: (pos(b, j), 0))

    def whole(a):
        return pl.BlockSpec(a.shape, lambda b, j: (0,) * a.ndim)

    in_specs = ([pspec(0, fwd), pspec(1, fwd), pspec(3, fwd), pspec(5, fwd), pspec(6, fwd), tspec(fwd), tspec(fwd)]
                + [pspec(0, bwd), pspec(2, bwd), pspec(3, bwd), pspec(5, bwd), pspec(6, bwd), tspec(bwd), tspec(bwd)]
                + [whole(a) for a in (lb, dmat, qs, ks, g64)])
    out_f = pl.BlockSpec((tc, HALF), lambda b, j: (b * tpb + fwd(b, j), 0))
    out_b = pl.BlockSpec((tc, HALF), lambda b, j: (b * tpb + bwd(b, j), 0))
    o_shape = jax.ShapeDtypeStruct((nt, HALF), F32)
    return pl.pallas_call(
        functools.partial(_rec_kernel, tc=tc),
        grid=(nb, tpb),
        in_specs=in_specs,
        out_specs=[out_f, out_f, out_b, out_b],
        out_shape=[o_shape] * 4,
        scratch_shapes=[pltpu.VMEM((2, HG_HEADS, HG_DV, HG_DK), F32),
                        pltpu.VMEM((2, RET_HEADS, RET_DV, RET_DK), F32)],
        compiler_params=_cparams("arbitrary", "arbitrary"),
        name="rec_scan",
    )(p0, p0, p0, p0, p0, cs, sn, p0, p0, p0, p0, p0, cs, sn, lb, dmat, qs, ks, g64)


def _rec_merge_kernel(ohf, ohb, orf, orb, hg_ref, rg_ref, hgain, rgain, yh_ref, yr_ref):
    oh = ohf[...] + ohb[...]
    orr = orf[...] + orb[...]
    hg = hg_ref[...]
    rg = rg_ref[...]
    for h in range(HG_HEADS):
        sl = slice(h * LANES, (h + 1) * LANES)
        o = oh[:, sl]
        y = o * lax.rsqrt(jnp.mean(o * o, axis=-1, keepdims=True) + NORM_EPS) * hgain[...]
        yh_ref[:, sl] = (y * _silu(hg[:, sl])).astype(BF16)
        o = orr[:, sl]
        oc = o - jnp.mean(o, axis=-1, keepdims=True)
        y = oc * lax.rsqrt(jnp.mean(oc * oc, axis=-1, keepdims=True) + NORM_EPS) * rgain[...]
        yr_ref[:, sl] = (y * _silu(rg[:, sl])).astype(BF16)


def _rec_merge_call(ohf, orf, ohb, orb, p0, hgain, rgain, *, tm):
    nt = p0.shape[0]
    blk = pl.BlockSpec((tm, HALF), lambda i: (i, 0))
    vec = pl.BlockSpec((1, LANES), lambda i: (0, 0))
    y_shape = jax.ShapeDtypeStruct((nt, HALF), BF16)
    return pl.pallas_call(
        _rec_merge_kernel,
        grid=(nt // tm,),
        in_specs=[blk, blk, blk, blk,
                  pl.BlockSpec((tm, HALF), lambda i: (i, 4)), pl.BlockSpec((tm, HALF), lambda i: (i, 7)), vec, vec],
        out_specs=[blk, blk],
        out_shape=[y_shape, y_shape],
        compiler_params=_cparams("arbitrary"),
        name="rec_merge",
    )(ohf, ohb, orf, orb, p0, p0, hgain, rgain)


def _post_kernel(ya_ref, yb_ref, h_ref, mod_ref, wout_ref, gain_ref, wr_ref, br_ref,
                 hn_ref, f_ref, lpos_ref, gate_ref, ne_ref, cb_ref, cnt_ref, cnt_sc, *, d, tm, row_fn):
    i = pl.program_id(0)

    @pl.when(i == 0)
    def _():
        cnt_sc[...] = jnp.zeros_like(cnt_sc)

    r = row_fn(i)
    y = jnp.concatenate([ya_ref[...], yb_ref[...]], axis=1)
    hn = h_ref[...] + mod_ref[pl.ds(r, 1), 2 * d:3 * d] * _dot(y, wout_ref[...])
    hn_ref[...] = hn
    f = _norm_mod(hn, gain_ref[...], mod_ref[pl.ds(r, 1), 3 * d:4 * d], mod_ref[pl.ds(r, 1), 4 * d:5 * d])
    f_ref[...] = f
    f_hi, f_lo = _split_bf16(f, 2)
    w_hi, w_lo = _split_bf16(wr_ref[...], 2)
    logits = _dot(f_hi, w_hi) + _dot(f_hi, w_lo) + _dot(f_lo, w_hi) + br_ref[...]
    lane = lax.broadcasted_iota(I32, (tm, E_PAD), 1)
    vals, idxs = [], []
    work = logits
    for _ in range(TOP_K):
        m = jnp.max(work, axis=-1, keepdims=True)
        sel = jnp.min(jnp.where(work == m, lane, E_PAD), axis=-1, keepdims=True)
        vals.append(m)
        idxs.append(sel)
        work = jnp.where(lane == sel, -jnp.inf, work)
    exps = [jnp.exp(v - vals[0]) for v in vals]
    inv = 1.0 / sum(exps)
    onehot = sum((lane == s).astype(F32) for s in idxs)
    n_e = jnp.sum(onehot, axis=0, keepdims=True)
    rr = lax.broadcasted_iota(I32, (tm, tm), 0)
    cc = lax.broadcasted_iota(I32, (tm, tm), 1)
    within = _dot((cc < rr).astype(F32).astype(BF16), onehot.astype(BF16))
    ea = lax.broadcasted_iota(I32, (E_PAD, E_PAD), 0)
    eb = lax.broadcasted_iota(I32, (E_PAD, E_PAD), 1)
    before = _dot(jnp.broadcast_to(n_e, (8, E_PAD)).astype(BF16), (ea < eb).astype(F32).astype(BF16))[0:1]
    base = within + before
    lpos_slab = jnp.zeros((tm, E_PAD), F32)
    gate_slab = jnp.zeros((tm, E_PAD), F32)
    for k in range(TOP_K):
        gate_slab = jnp.where(lane == k, exps[k] * inv, gate_slab)
        lp = jnp.sum(jnp.where(lane == idxs[k], base, 0.0), axis=-1, keepdims=True)
        lpos_slab = jnp.where(lane == k, lp, lpos_slab)
    lpos_ref[...] = lpos_slab.astype(I32)
    gate_ref[...] = gate_slab
    ne_ref[0] = jnp.broadcast_to(n_e, (8, E_PAD))
    cb_ref[0] = cnt_sc[...]
    cnt_sc[...] = cnt_sc[...] + n_e
    cnt_ref[...] = cnt_sc[...]


def _post_call(ya, yb, h, mod, wout, gain, wr, br, *, tm, n_tiles, y_block, h_block, row_fn):
    d = h.shape[1]
    n = n_tiles * tm
    tile = lambda w: pl.BlockSpec((tm, w), lambda i: (i, 0))
    const = lambda a: pl.BlockSpec(a.shape, lambda i: (0,) * a.ndim)
    return pl.pallas_call(
        functools.partial(_post_kernel, d=d, tm=tm, row_fn=row_fn),
        grid=(n_tiles,),
        in_specs=[pl.BlockSpec((tm, HALF), lambda i: (y_block(i), 0)),
                  pl.BlockSpec((tm, HALF), lambda i: (y_block(i), 0)),
                  pl.BlockSpec((tm, d), lambda i: (h_block(i), 0)),
                  const(mod), const(wout), const(gain), const(wr), const(br)],
        out_specs=[tile(d), tile(d), tile(E_PAD), tile(E_PAD),
                   pl.BlockSpec((1, 8, E_PAD), lambda i: (i, 0, 0)), pl.BlockSpec((1, 8, E_PAD), lambda i: (i, 0, 0)),
                   pl.BlockSpec((8, E_PAD), lambda i: (0, 0))],
        out_shape=[jax.ShapeDtypeStruct((n, d), F32), jax.ShapeDtypeStruct((n, d), F32),
                   jax.ShapeDtypeStruct((n, E_PAD), I32), jax.ShapeDtypeStruct((n, E_PAD), F32),
                   jax.ShapeDtypeStruct((n_tiles, 8, E_PAD), F32), jax.ShapeDtypeStruct((n_tiles, 8, E_PAD), F32),
                   jax.ShapeDtypeStruct((8, E_PAD), F32)],
        scratch_shapes=[pltpu.VMEM((8, E_PAD), F32)],
        compiler_params=_cparams("arbitrary"),
        name="post_mixer",
    )(ya, yb, h, mod, wout, gain, wr, br)


_TAB = 3 * N_EXPERTS


def _chunk_sizes(limit):
    return [1 << b for b in range(limit.bit_length() - 1, -1, -1)]


def _token_rows(start, count, pitch):
    return pl.ds(pl.multiple_of(start * pitch, pitch), count * pitch)


def _segment_copies(tab_ref, base, local, slot, grouped, sem, *, tm, pitch, to_grouped):
    for e in range(N_EXPERTS):
        g0 = tab_ref[base + e]
        n = tab_ref[base + N_EXPERTS + e]
        l0 = tab_ref[base + 2 * N_EXPERTS + e]
        for size in _chunk_sizes(tm):
            done = jnp.bitwise_and(n, -2 * size)

            @pl.when(jnp.bitwise_and(n, size) != 0)
            def _():
                loc = local.at[slot, _token_rows(l0 + done, size, pitch)]
                grp = grouped.at[_token_rows(g0 + done, size, pitch)]
                (pltpu.make_async_copy(loc, grp, sem) if to_grouped else pltpu.make_async_copy(grp, loc, sem)).start()


def _pad_zero_copies(ztab_ref, zeros, grouped, sem, *, tm, pitch, wait):
    for e in range(N_EXPERTS):
        g0 = ztab_ref[e]
        n = ztab_ref[N_EXPERTS + e]
        for size in _chunk_sizes(tm // 2):
            done = jnp.bitwise_and(n, -2 * size)

            @pl.when(jnp.bitwise_and(n, size) != 0)
            def _():
                cp = pltpu.make_async_copy(zeros.at[pl.ds(0, size * pitch)],
                                           grouped.at[_token_rows(g0 + done, size, pitch)], sem)
                cp.wait() if wait else cp.start()


def _lanes_to_tiles(ref, x):
    rows, pitch = x.shape[0], x.shape[1] // LANES
    for c in range(pitch):
        ref[pl.ds(c, rows, stride=pitch), :] = x[:, c * LANES:(c + 1) * LANES]


def _tiles_to_lanes(ref, pitch):
    rows = ref.shape[0] // pitch
    return jnp.concatenate([ref[pl.ds(c, rows, stride=pitch), :] for c in range(pitch)], axis=1)


def _dispatch_kernel(tab_ref, ztab_ref, f_ref, lpos_ref, xs_ref, zbuf, zeros, sems, zsem, *, tm, pitch):
    i = pl.program_id(0)
    n_tiles = pl.num_programs(0)
    slot = lax.rem(i, 2)
    rows = tm * TOP_K

    def tile_bytes(s):
        return pltpu.make_async_copy(zbuf.at[s], xs_ref.at[pl.ds(0, rows * pitch)], sems.at[s])

    @pl.when(i >= 2)
    def _():
        tile_bytes(slot).wait()

    lane = lax.broadcasted_iota(I32, (tm, rows), 1)
    lp = lpos_ref[...]
    sel = lane == lp[:, 0:1]
    for k in range(1, TOP_K):
        sel = jnp.logical_or(sel, lane == lp[:, k:k + 1])
    z = lax.dot_general(jnp.where(sel, 1.0, 0.0).astype(BF16), f_ref[...].astype(BF16), _TN,
                        preferred_element_type=F32)
    _lanes_to_tiles(zbuf.at[slot], z)
    _segment_copies(tab_ref, i * _TAB, zbuf, slot, xs_ref, sems.at[slot], tm=tm, pitch=pitch, to_grouped=True)

    @pl.when(i == n_tiles - 1)
    def _():
        zeros[...] = jnp.zeros_like(zeros)
        _pad_zero_copies(ztab_ref, zeros, xs_ref, zsem, tm=tm, pitch=pitch, wait=False)
        _pad_zero_copies(ztab_ref, zeros, xs_ref, zsem, tm=tm, pitch=pitch, wait=True)
        half = tm // 2

        def tail(j, start):
            cp = pltpu.make_async_copy(zeros, xs_ref.at[_token_rows(j * half, half, pitch)], zsem)
            cp.start() if start else cp.wait()

        first, last = ztab_ref[2 * N_EXPERTS], xs_ref.shape[0] // (half * pitch)
        lax.fori_loop(first, last, lambda j, c: (tail(j, True), c)[1], 0)
        lax.fori_loop(first, last, lambda j, c: (tail(j, False), c)[1], 0)
        tile_bytes(slot).wait()
        tile_bytes(1 - slot).wait()


def _dispatch_call(tab, ztab, f, lpos, n_rows, *, tm):
    n, d = f.shape
    assert n // tm >= 2 and tm & (tm - 1) == 0
    pitch = d // LANES
    grid_spec = pltpu.PrefetchScalarGridSpec(
        num_scalar_prefetch=2,
        grid=(n // tm,),
        in_specs=[pl.BlockSpec((tm, d), lambda i, t, z: (i, 0)),
                  pl.BlockSpec((tm, E_PAD), lambda i, t, z: (i, 0))],
        out_specs=pl.BlockSpec(memory_space=pl.ANY),
        scratch_shapes=[pltpu.VMEM((2, tm * TOP_K * pitch, LANES), F32), pltpu.VMEM((tm // 2 * pitch, LANES), F32),
                        pltpu.SemaphoreType.DMA((2,)), pltpu.SemaphoreType.DMA(())])
    return pl.pallas_call(
        functools.partial(_dispatch_kernel, tm=tm, pitch=pitch),
        grid_spec=grid_spec,
        out_shape=jax.ShapeDtypeStruct((n_rows * pitch, LANES), F32),
        compiler_params=_cparams("arbitrary"),
        name="moe_dispatch",
    )(tab, ztab, f, lpos)


def _expert_kernel(b0_ref, nb_ref, xs_ref, wgu_ref, bgu_ref, wd_ref, bd_ref, y_ref,
                   xbuf, ybuf, wgu_bf, wd_bf, xsem, ysem, *, de, tm, pitch):
    e = pl.program_id(0)
    nb = nb_ref[e]
    b0 = b0_ref[e]
    rows = tm * pitch

    def block(j):
        return pl.ds(pl.multiple_of(j * rows, rows), rows)

    def x_copy(j, s):
        return pltpu.make_async_copy(xs_ref.at[block(b0 + j)], xbuf.at[s], xsem.at[s])

    def y_copy(j, s):
        return pltpu.make_async_copy(ybuf.at[s], y_ref.at[block(b0 + j)], ysem.at[s])

    @pl.when(nb > 0)
    def _():
        x_copy(0, 0).start()
        wgu_bf[...] = wgu_ref[0, 0].astype(BF16)
        wd_bf[...] = wd_ref[0, 0].astype(BF16)

    def body(j, c):
        s = lax.rem(j, 2)
        x_copy(j, s).wait()

        @pl.when(j + 1 < nb)
        def _():
            x_copy(j + 1, 1 - s).start()

        gu = _dot(_tiles_to_lanes(xbuf.at[s], pitch).astype(BF16), wgu_bf[...]) + bgu_ref[0, 0]
        gate = jnp.minimum(gu[:, 0:de], SWIGLU_LIMIT)
        up = jnp.clip(gu[:, de:2 * de], -SWIGLU_LIMIT, SWIGLU_LIMIT)
        hid = (up + 1.0) * gate * jax.nn.sigmoid(SWIGLU_ALPHA * gate)
        y = _dot(hid.astype(BF16), wd_bf[...]) + bd_ref[0, 0]

        @pl.when(j >= 2)
        def _():
            y_copy(j - 2, s).wait()

        _lanes_to_tiles(ybuf.at[s], y)
        y_copy(j, s).start()
        return c

    lax.fori_loop(0, nb, body, 0)

    @pl.when(nb >= 2)
    def _():
        y_copy(nb - 2, lax.rem(nb, 2)).wait()

    @pl.when(nb >= 1)
    def _():
        y_copy(nb - 1, lax.rem(nb + 1, 2)).wait()

    @pl.when(e == pl.num_programs(0) - 1)
    def _():
        ybuf[0] = jnp.zeros(ybuf.shape[1:], F32)

        def tail(j, start):
            cp = pltpu.make_async_copy(ybuf.at[0], y_ref.at[block(j)], ysem.at[0])
            cp.start() if start else cp.wait()

        first, last = b0 + nb, y_ref.shape[0] // rows
        lax.fori_loop(first, last, lambda j, c: (tail(j, True), c)[1], 0)
        lax.fori_loop(first, last, lambda j, c: (tail(j, False), c)[1], 0)


def _expert_call(blk0, nblk, xs, w_gu, b_gu, w_down, b_down, *, tm, layer):
    nl, ne, d, de2 = w_gu.shape
    pitch = d // LANES
    de = de2 // 2
    grid_spec = pltpu.PrefetchScalarGridSpec(
        num_scalar_prefetch=2,
        grid=(ne,),
        in_specs=[pl.BlockSpec(memory_space=pl.ANY),
                  pl.BlockSpec((1, 1, d, de2), lambda e, b0, nb: (layer, e, 0, 0)),
                  pl.BlockSpec((1, 1, 1, de2), lambda e, b0, nb: (layer, e, 0, 0)),
                  pl.BlockSpec((1, 1, de, d), lambda e, b0, nb: (layer, e, 0, 0)),
                  pl.BlockSpec((1, 1, 1, d), lambda e, b0, nb: (layer, e, 0, 0))],
        out_specs=pl.BlockSpec(memory_space=pl.ANY),
        scratch_shapes=[pltpu.VMEM((2, tm * pitch, LANES), F32), pltpu.VMEM((2, tm * pitch, LANES), F32),
                        pltpu.VMEM((d, de2), BF16), pltpu.VMEM((de, d), BF16),
                        pltpu.SemaphoreType.DMA((2,)), pltpu.SemaphoreType.DMA((2,))])
    return pl.pallas_call(
        functools.partial(_expert_kernel, de=de, tm=tm, pitch=pitch),
        grid_spec=grid_spec,
        out_shape=jax.ShapeDtypeStruct(xs.shape, F32),
        compiler_params=_cparams("arbitrary"),
        name="moe_experts",
    )(blk0, nblk, xs, w_gu, b_gu.reshape(nl, ne, 1, de2), w_down, b_down.reshape(nl, ne, 1, d))


def _combine_kernel(tab_ref, lpos_ref, gate_ref, h_ref, mod_ref, y_ref, o_ref, ybuf, sems, *, d, tm, row_fn):
    i = pl.program_id(0)
    n_tiles = pl.num_programs(0)
    slot = lax.rem(i, 2)
    rows = tm * TOP_K
    pitch = d // LANES

    def fetch(tile, s):
        _segment_copies(tab_ref, tile * _TAB, ybuf, s, y_ref, sems.at[s], tm=tm, pitch=pitch, to_grouped=False)

    @pl.when(i == 0)
    def _():
        fetch(0, 0)

    @pl.when(i + 1 < n_tiles)
    def _():
        fetch(i + 1, 1 - slot)

    pltpu.make_async_copy(y_ref.at[pl.ds(0, rows * pitch)], ybuf.at[slot], sems.at[slot]).wait()
    yb = _tiles_to_lanes(ybuf.at[slot], pitch).astype(BF16)
    lane = lax.broadcasted_iota(I32, (tm, rows), 1)
    lp = lpos_ref[...]
    gates = gate_ref[...]
    g = jnp.zeros((tm, rows), F32)
    for k in range(TOP_K):
        g = jnp.where(lane == lp[:, k:k + 1], gates[:, k:k + 1], g)
    g_hi, g_lo = _split_bf16(g, 2)
    acc = _dot(g_hi, yb) + _dot(g_lo, yb)
    r = row_fn(i)
    o_ref[...] = h_ref[...] + mod_ref[pl.ds(r, 1), 5 * d:6 * d] * acc


def _combine_call(tab, lpos, gates, hn, mod, y_rows, *, tm, row_fn):
    n, d = hn.shape
    pitch = d // LANES
    grid_spec = pltpu.PrefetchScalarGridSpec(
        num_scalar_prefetch=1,
        grid=(n // tm,),
        in_specs=[pl.BlockSpec((tm, E_PAD), lambda i, t: (i, 0)),
                  pl.BlockSpec((tm, E_PAD), lambda i, t: (i, 0)),
                  pl.BlockSpec((tm, d), lambda i, t: (i, 0)),
                  pl.BlockSpec(mod.shape, lambda i, t: (0, 0)),
                  pl.BlockSpec(memory_space=pl.ANY)],
        out_specs=pl.BlockSpec((tm, d), lambda i, t: (i, 0)),
        scratch_shapes=[pltpu.VMEM((2, tm * TOP_K * pitch, LANES), F32), pltpu.SemaphoreType.DMA((2,))])
    return pl.pallas_call(
        functools.partial(_combine_kernel, d=d, tm=tm, row_fn=row_fn),
        grid_spec=grid_spec,
        out_shape=jax.ShapeDtypeStruct((n, d), F32),
        compiler_params=_cparams("arbitrary"),
        name="moe_combine",
    )(tab, lpos, gates, hn, mod, y_rows)


def _moe(f, hn, mod, lpos, gates, ne_t, cb_t, counts, w_gu, b_gu, w_down, b_down, *, tm, row_fn, layer):
    n, _ = f.shape
    n_blocks = -(-n * TOP_K // tm) + N_EXPERTS
    cnt = counts[0, :N_EXPERTS].astype(I32)
    padded = (cnt + tm - 1) // tm * tm
    pad_end = jnp.cumsum(padded)
    pad_start = pad_end - padded
    ne = ne_t[:, 0, :N_EXPERTS].astype(I32)
    cb = cb_t[:, 0, :N_EXPERTS].astype(I32)
    tab = jnp.concatenate([pad_start[None] + cb, ne, jnp.cumsum(ne, axis=1) - ne], axis=1).reshape(-1)
    n_valid = (pad_end[-1] // tm).astype(I32)
    ztab = jnp.concatenate([pad_start + cnt, padded - cnt, 2 * n_valid[None]])
    xs = _dispatch_call(tab, ztab, f, lpos, n_blocks * tm, tm=tm)
    y_rows = _expert_call(pad_start // tm, padded // tm, xs, w_gu, b_gu, w_down, b_down, tm=tm, layer=layer)
    return _combine_call(tab, lpos, gates, hn, mod, y_rows, tm=tm, row_fn=row_fn)


def _att_in_kernel(h_ref, mod_ref, gain_ref, w_ref, qg_gain, kg_gain, dq_gain, dk_gain, bd_ref,
                   cg_ref, sg_ref, cd_ref, sd_ref,
                   qg_ref, kg_ref, vg_ref, qd_ref, kd_ref, vd_ref, *, d, tpb, nct, nb):
    r = _unified_row(pl.program_id(0), tpb, nct, nb)
    a = _norm_mod(h_ref[...], gain_ref[...], mod_ref[pl.ds(r, 1), 0:d], mod_ref[pl.ds(r, 1), d:2 * d])
    p = _dot(a.astype(BF16), w_ref[...])
    cg, sg = cg_ref[...], sg_ref[...]

    def head_norm_rope(x, gain):
        y = x * lax.rsqrt(jnp.mean(x * x, axis=-1, keepdims=True) + NORM_EPS) * gain
        return _rope(y, cg, sg, GQA_HD).astype(BF16)

    c0 = 0
    for hh in range(GQA_HEADS):
        qg_ref[:, hh * GQA_HD:(hh + 1) * GQA_HD] = head_norm_rope(p[:, c0:c0 + GQA_HD], qg_gain[...])
        c0 += GQA_HD
    for hh in range(GQA_KV_HEADS):
        kg_ref[:, hh * GQA_HD:(hh + 1) * GQA_HD] = head_norm_rope(p[:, c0:c0 + GQA_HD], kg_gain[...])
        c0 += GQA_HD
    wv = GQA_KV_HEADS * GQA_HD
    vg_ref[...] = p[:, c0:c0 + wv].astype(BF16)
    c0 += wv

    def group_norm_rope(x, gain):
        ss = sum(_dot(piece, bd_ref[...]) for piece in _split_bf16(x * x, 2))
        y = x * lax.rsqrt(ss * (1.0 / DIFF_HD) + NORM_EPS) * gain
        return _rope(y, cd_ref[...], sd_ref[...], DIFF_HD).astype(BF16)

    qd_ref[...] = group_norm_rope(p[:, c0:c0 + HALF], dq_gain[...])
    c0 += HALF
    kd_ref[...] = group_norm_rope(p[:, c0:c0 + HALF], dk_gain[...])
    c0 += HALF
    vd_ref[...] = p[:, c0:c0 + HALF].astype(BF16)


def _att_in_call(h, mod, gain, w, qg_gain, kg_gain, dq_gain, dk_gain, bd, cg, sg, cd, sd, *, tm, tpb, nct, nb):
    nt, d = h.shape
    const = lambda a: pl.BlockSpec(a.shape, lambda i: (0,) * a.ndim)
    tab = lambda w_: pl.BlockSpec((tm, w_), lambda i: (lax.rem(i, tpb), 0))
    out = lambda w_: pl.BlockSpec((tm, w_), lambda i: (i, 0))
    shp = lambda w_: jax.ShapeDtypeStruct((nt, w_), BF16)
    wkv = GQA_KV_HEADS * GQA_HD
    return pl.pallas_call(
        functools.partial(_att_in_kernel, d=d, tpb=tpb, nct=nct, nb=nb),
        grid=(nt // tm,),
        in_specs=[pl.BlockSpec((tm, d), lambda i: (i, 0)), const(mod), const(gain), const(w),
                  const(qg_gain), const(kg_gain), const(dq_gain), const(dk_gain), const(bd),
                  tab(GQA_HD), tab(GQA_HD), tab(HALF), tab(HALF)],
        out_specs=[out(HALF), out(wkv), out(wkv), out(HALF), out(HALF), out(HALF)],
        out_shape=[shp(HALF), shp(wkv), shp(wkv), shp(HALF), shp(HALF), shp(HALF)],
        compiler_params=_cparams("arbitrary"),
        name="att_in_proj",
    )(h, mod, gain, w, qg_gain, kg_gain, dq_gain, dk_gain, bd, cg, sg, cd, sd)


def _softmax_parts(q, k):
    s = lax.dot_general(q, k, _NT, preferred_element_type=F32)
    e = jnp.exp2(s - jnp.max(s, axis=-1, keepdims=True))
    return e, 1.0 / jnp.sum(e, axis=-1, keepdims=True)


def _gqa_kernel(q_ref, k_ref, v_ref, o_ref, *, group):
    for g in range(group):
        sl = slice(g * GQA_HD, (g + 1) * GQA_HD)
        e, inv = _softmax_parts(q_ref[:, sl], k_ref[...])
        o_ref[:, sl] = (_dot(e.astype(BF16), v_ref[...]) * inv).astype(BF16)


def _gqa_call(qg, kg, vg, *, nb, tq, tot, nqt, lat0):
    group = GQA_HEADS // GQA_KV_HEADS
    upb = tot // tq
    return pl.pallas_call(
        functools.partial(_gqa_kernel, group=group),
        grid=(nb, GQA_KV_HEADS, nqt),
        in_specs=[pl.BlockSpec((tq, group * GQA_HD), lambda b, h, q: (b * upb + lat0 + q, h)),
                  pl.BlockSpec((tot, GQA_HD), lambda b, h, q: (b, h)),
                  pl.BlockSpec((tot, GQA_HD), lambda b, h, q: (b, h))],
        out_specs=pl.BlockSpec((tq, group * GQA_HD), lambda b, h, q: (b * nqt + q, h)),
        out_shape=jax.ShapeDtypeStruct((nb * nqt * tq, HALF), BF16),
        compiler_params=_cparams("arbitrary", "arbitrary", "arbitrary"),
        name="gqa_attention",
    )(qg, kg, vg)


def _diff_kernel(q_ref, k_ref, v_ref, lam_ref, gain_ref, o_ref, *, lam_init):
    lp = lam_ref[...]
    lam = (jnp.exp(jnp.sum(lp[0:1] * lp[1:2], axis=-1, keepdims=True))
           - jnp.exp(jnp.sum(lp[2:3] * lp[3:4], axis=-1, keepdims=True)) + lam_init)
    hd2 = 2 * DIFF_HD
    for j in range(q_ref.shape[1] // hd2):
        sl = slice(j * hd2, (j + 1) * hd2)
        q = q_ref[:, sl]
        k = k_ref[:, sl]
        lane = lax.broadcasted_iota(I32, q.shape, 1)
        zero = jnp.zeros_like(q)
        e1, inv1 = _softmax_parts(jnp.where(lane < DIFF_HD, q, zero), k)
        e2, inv2 = _softmax_parts(jnp.where(lane >= DIFF_HD, q, zero), k)
        a = e1 * inv1 - e2 * (lam * inv2)
        o = _dot(a.astype(BF16), v_ref[:, sl])
        y = o * lax.rsqrt(jnp.mean(o * o, axis=-1, keepdims=True) + NORM_EPS) * gain_ref[...]
        o_ref[:, sl] = (y * (1.0 - lam_init)).astype(BF16)


def _diff_call(qd, kd, vd, lam_params, gain, *, nb, tq, tot, nqt, lat0, lam_init):
    upb = tot // tq
    hd2 = 2 * (2 * DIFF_HD)
    return pl.pallas_call(
        functools.partial(_diff_kernel, lam_init=lam_init),
        grid=(nb, HALF // hd2, nqt),
        in_specs=[pl.BlockSpec((tq, hd2), lambda b, h, q: (b * upb + lat0 + q, h)),
                  pl.BlockSpec((tot, hd2), lambda b, h, q: (b, h)),
                  pl.BlockSpec((tot, hd2), lambda b, h, q: (b, h)),
                  pl.BlockSpec(lam_params.shape, lambda b, h, q: (0, 0)),
                  pl.BlockSpec(gain.shape, lambda b, h, q: (0, 0))],
        out_specs=pl.BlockSpec((tq, hd2), lambda b, h, q: (b * nqt + q, h)),
        out_shape=jax.ShapeDtypeStruct((nb * nqt * tq, HALF), BF16),
        compiler_params=_cparams("arbitrary", "arbitrary", "arbitrary"),
        name="diff_attention",
    )(qd, kd, vd, lam_params, gain)


def _deinterleave(hd):
    return np.concatenate([np.arange(0, hd, 2), np.arange(1, hd, 2)])


def _head_perm(n_heads, hd):
    return np.concatenate([h * hd + _deinterleave(hd) for h in range(n_heads)])


def _rope_tables(n_lat, n_ctx, hd, reps):
    t = np.arange(n_lat)
    row, col = (t // GRID_W).astype(np.float64), (t % GRID_W).astype(np.float64)
    axis_dim = hd // 2
    inv_freq = ROPE_THETA ** (-np.arange(0, axis_dim, 2, dtype=np.float64) / axis_dim)
    ang = np.concatenate([row[:, None] * inv_freq, col[:, None] * inv_freq], axis=-1)
    cos = np.concatenate([np.ones((n_ctx, hd // 2)), np.cos(ang)], axis=0)
    sin = np.concatenate([np.zeros((n_ctx, hd // 2)), np.sin(ang)], axis=0)
    c = np.tile(np.concatenate([cos, cos], axis=1), (1, reps))
    s = np.tile(np.concatenate([-sin, sin], axis=1), (1, reps))
    return jnp.asarray(c, F32), jnp.asarray(s, F32)


def _retention_tables(n):
    gam = 1.0 - 2.0 ** (-5.0 - np.arange(RET_HEADS, dtype=np.float64))
    gams = (gam, gam[::-1])
    c = np.arange(n, dtype=np.float64)
    diff = c[:, None] - c[None, :]
    dmat = np.zeros((2, RET_HEADS, n, n))
    qs = np.zeros((2, n, RET_HEADS * RET_DK))
    ks = np.zeros((2, n, RET_HEADS * RET_DK))
    gpow = np.zeros((2, 8, RET_HEADS * RET_DK))
    for d in range(2):
        for h in range(RET_HEADS):
            g = gams[d][h]
            sl = slice(h * RET_DK, (h + 1) * RET_DK)
            if d == 0:
                dmat[d, h] = np.where(diff >= 0, g ** np.maximum(diff, 0), 0.0)
                qs[d, :, sl] = (g ** (c + 1))[:, None]
                ks[d, :, sl] = (g ** (n - 1 - c))[:, None]
            else:
                dmat[d, h] = np.where(diff <= 0, g ** np.maximum(-diff, 0), 0.0)
                qs[d, :, sl] = (g ** (n - c))[:, None]
                ks[d, :, sl] = (g ** c)[:, None]
            gpow[d, :, sl] = g ** n
    return tuple(jnp.asarray(a, F32) for a in (dmat, qs, ks, gpow))


def _plan(nb, n_lat, n_ctx):
    tm = 256 if (n_ctx % 256 == 0 and n_lat % 256 == 0) else 128
    assert n_ctx % tm == 0 and n_lat % tm == 0 and tm % CHUNK == 0 and n_lat % GRID_W == 0
    tq = tm
    return tm, tq


def kernel(x, c, ctx, c_ctx, norm_mix, norm_ffn, w_ada, b_ada, rec_w_in, rec_lb_logits, rec_w_out, rec_hg_gain,
           rec_ret_gain, att_w_in, att_w_out, att_q_gain, att_k_gain, diff_q_gain, diff_k_gain, diff_lambda,
           diff_gain, w_router, b_router, w_gu, b_gu, w_down, b_down):
    nb, n_lat, d = x.shape
    n_ctx = ctx.shape[1]
    assert w_ada.shape[0] == 2, "two layers: recurrent mixer then attention mixer"
    tm, tq = _plan(nb, n_lat, n_ctx)
    tot = n_ctx + n_lat
    tpb, nct = tot // tm, n_ctx // tm
    nt = nb * tot
    nql = n_lat // tm

    unified_row = functools.partial(_unified_row, tpb=tpb, nct=nct, nb=nb)
    latent_row = lambda i: lax.div(i, nql)
    latent_block = lambda i: lax.div(i, nql) * tpb + nct + lax.rem(i, nql)
    ident = lambda i: i

    n_cond = -(-(nb + 1) // 8) * 8
    cond = jnp.concatenate([c, c_ctx[None], jnp.zeros((n_cond - nb - 1, d), F32)], axis=0)
    mod = _ada_call(cond, w_ada, b_ada)

    wr = jnp.pad(w_router, ((0, 0), (0, 0), (0, E_PAD - N_EXPERTS)))
    br = jnp.pad(b_router, ((0, 0), (0, E_PAD - N_EXPERTS)), constant_values=NEG_BIG)[:, None, :]

    h = jnp.concatenate([ctx, x], axis=1).reshape(nt, d)

    lb = jnp.cumsum(jax.nn.softmax(rec_lb_logits.astype(F32), axis=0), axis=0)[0]
    w_in = rec_w_in[0]
    c_rq = HG_HEADS * HG_DK * 3 + HG_HEADS * HG_DV * 2
    nrk = RET_HEADS * RET_DK
    perm = _head_perm(RET_HEADS, RET_DK)
    w_in = jnp.concatenate([w_in[:, :c_rq], w_in[:, c_rq + perm], w_in[:, c_rq + nrk + perm] * (RET_DK ** -0.5),
                            w_in[:, c_rq + 2 * nrk:]], axis=1).astype(BF16)
    cs_r, sn_r = _rope_tables(n_lat, n_ctx, RET_DK, 2 * RET_HEADS)
    dmat, qs, ks, g64 = _retention_tables(tm)
    p0 = _inproj_call(h, mod[0], norm_mix[0][None], w_in, tm=tm, tpb=tpb, nct=nct, nb=nb)
    ohf, orf, ohb, orb = _rec_call(p0, lb, cs_r, sn_r, dmat, qs, ks, g64, nb=nb, tc=tm, tpb=tpb, nct=nct)
    yh, yr = _rec_merge_call(ohf, orf, ohb, orb, p0, rec_hg_gain[0][None], rec_ret_gain[0][None], tm=tm)
    hn, f, *routing = _post_call(
        yh, yr, h, mod[0], rec_w_out[0].astype(BF16), norm_ffn[0][None], wr[0], br[0],
        tm=tm, n_tiles=nt // tm, y_block=ident, h_block=ident, row_fn=unified_row)
    h = _moe(f, hn, mod[0], *routing, w_gu, b_gu, w_down, b_down, tm=tm, row_fn=unified_row, layer=0)

    pg, pd = _head_perm(GQA_HEADS, GQA_HD), _head_perm(2 * DIFF_HEADS, DIFF_HD)
    w_in = att_w_in[0]
    o_gk = GQA_HEADS * GQA_HD
    o_gv = o_gk + GQA_KV_HEADS * GQA_HD
    o_dq = o_gv + GQA_KV_HEADS * GQA_HD
    o_dk = o_dq + HALF
    o_dv = o_dk + HALF
    w_in = jnp.concatenate([w_in[:, pg], w_in[:, o_gk + pg[:GQA_KV_HEADS * GQA_HD]], w_in[:, o_gv:o_dq],
                            w_in[:, o_dq + pd], w_in[:, o_dk + pd], w_in[:, o_dv:]], axis=1).astype(BF16)
    dg, dd = _deinterleave(GQA_HD), _deinterleave(DIFF_HD)
    cg, sg = _rope_tables(n_lat, n_ctx, GQA_HD, 1)
    cd, sd = _rope_tables(n_lat, n_ctx, DIFF_HD, 2 * DIFF_HEADS)
    bd = jnp.asarray(np.kron(np.eye(2 * DIFF_HEADS), np.ones((DIFF_HD, DIFF_HD))), BF16)
    log2e = math.log2(math.e)
    qg, kg, vg, qd, kd, vd = _att_in_call(
        h, mod[1], norm_mix[1][None], w_in,
        (att_q_gain[0][dg] * (GQA_HD ** -0.5 * log2e))[None], att_k_gain[0][dg][None],
        jnp.tile(diff_q_gain[0][dd] * (DIFF_HD ** -0.5 * log2e), 2 * DIFF_HEADS)[None],
        jnp.tile(diff_k_gain[0][dd], 2 * DIFF_HEADS)[None],
        bd, cg, sg, cd, sd, tm=tm, tpb=tpb, nct=nct, nb=nb)
    nqt = n_lat // tq
    yg = _gqa_call(qg, kg, vg, nb=nb, tq=tq, tot=tot, nqt=nqt, lat0=n_ctx // tq)
    lam_init = 0.8 - 0.6 * math.exp(-0.3 * 1)
    yd = _diff_call(qd, kd, vd, diff_lambda[0], diff_gain[0][None], nb=nb, tq=tq, tot=tot, nqt=nqt,
                    lat0=n_ctx // tq, lam_init=lam_init)
    hn, f, *routing = _post_call(
        yg, yd, h, mod[1], att_w_out[0].astype(BF16), norm_ffn[1][None], wr[1], br[1],
        tm=tm, n_tiles=nb * nql, y_block=ident, h_block=latent_block, row_fn=latent_row)
    out = _moe(f, hn, mod[1], *routing, w_gu, b_gu, w_down, b_down, tm=tm, row_fn=latent_row, layer=1)
    return out.reshape(nb, n_lat, d)
```

```python
import functools
import math

import numpy as np
import jax
import jax.numpy as jnp
from jax import lax
from jax.experimental import pallas as pl
from jax.experimental.pallas import tpu as pltpu

F32 = jnp.float32
BF16 = jnp.bfloat16
I32 = jnp.int32

GRID_W = 64
HG_HEADS, HG_DK, HG_DV = 4, 128, 128
RET_HEADS, RET_DK, RET_DV = 4, 64, 128
GQA_HEADS, GQA_KV_HEADS, GQA_HD = 4, 2, 128
DIFF_HEADS, DIFF_HD = 4, 64
N_EXPERTS, TOP_K = 32, 4
SWIGLU_LIMIT, SWIGLU_ALPHA = 7.0, 1.702
CHUNK = 64
ROPE_THETA = 10000.0
NORM_EPS = 1e-6

LANES = 128
E_PAD = LANES
V7X_VMEM_LIMIT = 56 * 1024 * 1024
HALF = 512

_NT = (((1,), (1,)), ((), ()))
_TN = (((0,), (0,)), ((), ()))


def _cparams(*sem):
    return pltpu.CompilerParams(dimension_semantics=sem, vmem_limit_bytes=V7X_VMEM_LIMIT)


def _silu(x):
    return x * jax.nn.sigmoid(x)


def _dot(a, b):
    return jnp.dot(a, b, preferred_element_type=F32)


def _split_bf16(x, terms):
    out = []
    for _ in range(terms):
        p = x.astype(BF16)
        out.append(p)
        x = x - p.astype(F32)
    return out


def _norm_mod(x, gain, shift, scale):
    y = x * lax.rsqrt(jnp.mean(x * x, axis=-1, keepdims=True) + NORM_EPS) * gain
    return y * (1.0 + scale) + shift


def _unified_row(i, tpb, nct, nb):
    return jnp.where(lax.rem(i, tpb) < nct, nb, lax.div(i, tpb))


def _ada_kernel(cond_ref, w_ref, b_ref, o_ref):
    s = _silu(cond_ref[...]).astype(BF16)
    o_ref[0] = _dot(s, w_ref[0].astype(BF16)) + b_ref[0]


def _ada_call(cond, w_ada, b_ada):
    n_layers, d, d6 = w_ada.shape
    r = cond.shape[0]
    tn = d6 // 4
    return pl.pallas_call(
        _ada_kernel,
        grid=(n_layers, d6 // tn),
        in_specs=[pl.BlockSpec((r, d), lambda l, n: (0, 0)),
                  pl.BlockSpec((1, d, tn), lambda l, n: (l, 0, n)),
                  pl.BlockSpec((1, 1, tn), lambda l, n: (l, 0, n))],
        out_specs=pl.BlockSpec((1, r, tn), lambda l, n: (l, 0, n)),
        out_shape=jax.ShapeDtypeStruct((n_layers, r, d6), F32),
        compiler_params=_cparams("arbitrary", "arbitrary"),
        name="ada_modulation",
    )(cond, w_ada, b_ada.reshape(n_layers, 1, d6))


def _inproj_kernel(h_ref, mod_ref, gain_ref, w_ref, o_ref, *, d, tpb, nct, nb):
    r = _unified_row(pl.program_id(0), tpb, nct, nb)
    a = _norm_mod(h_ref[...], gain_ref[...], mod_ref[pl.ds(r, 1), 0:d], mod_ref[pl.ds(r, 1), d:2 * d])
    o_ref[...] = _dot(a.astype(BF16), w_ref[...])


def _inproj_call(h, mod, gain, w, *, tm, tpb, nct, nb):
    nt, d = h.shape
    n_out = w.shape[1]
    return pl.pallas_call(
        functools.partial(_inproj_kernel, d=d, tpb=tpb, nct=nct, nb=nb),
        grid=(nt // tm,),
        in_specs=[pl.BlockSpec((tm, d), lambda i: (i, 0)),
                  pl.BlockSpec(mod.shape, lambda i: (0, 0)),
                  pl.BlockSpec((1, d), lambda i: (0, 0)),
                  pl.BlockSpec((d, n_out), lambda i: (0, 0))],
        out_specs=pl.BlockSpec((tm, n_out), lambda i: (i, 0)),
        out_shape=jax.ShapeDtypeStruct((nt, n_out), F32),
        compiler_params=_cparams("arbitrary"),
        name="rec_in_proj",
    )(h, mod, gain, w)


def _swap_halves(x, group):
    if group == LANES:
        return pltpu.roll(x, LANES // 2, 1)
    lane = lax.broadcasted_iota(I32, x.shape, 1)
    half = group // 2
    return jnp.where(lax.rem(lane, group) < half, pltpu.roll(x, LANES - half, 1), pltpu.roll(x, half, 1))


def _rope(x, cos, sin, group):
    parts = [_swap_halves(x[:, s:s + LANES], group) for s in range(0, x.shape[1], LANES)]
    rot = parts[0] if len(parts) == 1 else jnp.concatenate(parts, axis=1)
    return x * cos + rot * sin


def _hgrn_chunk(d, r0, hq, hf, hi, lb_ref, ohg, st_hg, tri_bf, tri_mask):
    rows = pl.ds(r0, CHUNK)
    lbv = lb_ref[d:d + 1, :]
    f = lbv + (1.0 - lbv) * jax.nn.sigmoid(hf[rows, :])
    logf = jnp.log(f)
    kk = 1.0 - f
    b = sum(_dot(tri_bf, p) for p in _split_bf16(logf, 3))
    tot = b[CHUNK - 1:CHUNK, :] if d == 0 else b[0:1, :]
    q = _silu(hq[rows, :]) * (HG_DK ** -0.5)
    q_in = (q * jnp.exp(b)).astype(BF16)
    k_in = (kk * jnp.exp(-b)).astype(BF16)
    k_st = (kk * jnp.exp(tot - b)).astype(BF16)
    dec = jnp.exp(tot)
    v = hi[rows, :].astype(BF16)
    for h in range(HG_HEADS):
        sl = slice(h * HG_DK, (h + 1) * HG_DK)
        att = jnp.where(tri_mask, lax.dot_general(q_in[:, sl], k_in[:, sl], _NT, preferred_element_type=F32), 0.0)
        s_t = st_hg[d, h]
        o = _dot(att.astype(BF16), v[:, sl]) + lax.dot_general(
            q_in[:, sl], s_t.astype(BF16), _NT, preferred_element_type=F32)
        ohg[rows, sl] = o
        st_hg[d, h] = s_t * dec[:, sl] + lax.dot_general(v[:, sl], k_st[:, sl], _TN, preferred_element_type=F32)


def _retention_tile(d, rqk, rv, cs, sn, dmat_ref, qs_ref, ks_ref, gpow_ref, oret, st_ret):
    xr = _rope(rqk[...], cs[...], sn[...], RET_DK)
    nq = RET_HEADS * RET_DK
    q_r = xr[:, 0:nq]
    k_r = xr[:, nq:2 * nq]
    q_b = q_r.astype(BF16)
    k_b = k_r.astype(BF16)
    q_sc = (q_r * qs_ref[d]).astype(BF16)
    k_sc = (k_r * ks_ref[d]).astype(BF16)
    vv = rv[...].astype(BF16)
    for h in range(RET_HEADS):
        sk = slice(h * RET_DK, (h + 1) * RET_DK)
        sv = slice(h * RET_DV, (h + 1) * RET_DV)
        att = lax.dot_general(q_b[:, sk], k_b[:, sk], _NT, preferred_element_type=F32) * dmat_ref[d, h]
        s_t = st_ret[d, h]
        o = _dot(att.astype(BF16), vv[:, sv]) + lax.dot_general(
            q_sc[:, sk], s_t.astype(BF16), _NT, preferred_element_type=F32)
        oret[:, sv] = o
        st_ret[d, h] = s_t * gpow_ref[d, 0:1, sk] + lax.dot_general(
            vv[:, sv], k_sc[:, sk], _TN, preferred_element_type=F32)


def _rec_kernel(hq_f, hf_f, hi_f, rqk_f, rv_f, cs_f, sn_f,
                hq_b, hf_b, hi_b, rqk_b, rv_b, cs_b, sn_b,
                lb_ref, dmat_ref, qs_ref, ks_ref, g64_ref,
                ohg_f, oret_f, ohg_b, oret_b, st_hg, st_ret, *, tc):
    @pl.when(pl.program_id(1) == 0)
    def _():
        st_hg[...] = jnp.zeros_like(st_hg)
        st_ret[...] = jnp.zeros_like(st_ret)

    nch = tc // CHUNK
    row = lax.broadcasted_iota(I32, (CHUNK, CHUNK), 0)
    col = lax.broadcasted_iota(I32, (CHUNK, CHUNK), 1)
    mask_f = col <= row
    mask_b = col >= row
    tri_f = mask_f.astype(F32).astype(BF16)
    tri_b = mask_b.astype(F32).astype(BF16)

    ret_tables = (dmat_ref, qs_ref, ks_ref, g64_ref)
    _retention_tile(0, rqk_f, rv_f, cs_f, sn_f, *ret_tables, oret_f, st_ret)
    _retention_tile(1, rqk_b, rv_b, cs_b, sn_b, *ret_tables, oret_b, st_ret)
    for ci in range(nch):
        _hgrn_chunk(0, ci * CHUNK, hq_f, hf_f, hi_f, lb_ref, ohg_f, st_hg, tri_f, mask_f)
        _hgrn_chunk(1, (nch - 1 - ci) * CHUNK, hq_b, hf_b, hi_b, lb_ref, ohg_b, st_hg, tri_b, mask_b)


def _rec_call(p0, lb, cs, sn, dmat, qs, ks, g64, *, nb, tc, tpb, nct):
    nt = p0.shape[0]

    def fwd(b, j):
        return j

    def bwd(b, j):
        return jnp.where(j < nct, nct - 1 - j, tpb + nct - 1 - j)

    def pspec(col, pos):
        return pl.BlockSpec((tc, HALF), lambda b, j: (b * tpb + pos(b, j), col))

    def tspec(pos):
        return pl.BlockSpec((tc, HALF), lambda b, j: (pos(b, j), 0))

    def whole(a):
        return pl.BlockSpec(a.shape, lambda b, j: (0,) * a.ndim)

    in_specs = ([pspec(0, fwd), pspec(1, fwd), pspec(3, fwd), pspec(5, fwd), pspec(6, fwd), tspec(fwd), tspec(fwd)]
                + [pspec(0, bwd), pspec(2, bwd), pspec(3, bwd), pspec(5, bwd), pspec(6, bwd), tspec(bwd), tspec(bwd)]
                + [whole(a) for a in (lb, dmat, qs, ks, g64)])
    out_f = pl.BlockSpec((tc, HALF), lambda b, j: (b * tpb + fwd(b, j), 0))
    out_b = pl.BlockSpec((tc, HALF), lambda b, j: (b * tpb + bwd(b, j), 0))
    o_shape = jax.ShapeDtypeStruct((nt, HALF), F32)
    return pl.pallas_call(
        functools.partial(_rec_kernel, tc=tc),
        grid=(nb, tpb),
        in_specs=in_specs,
        out_specs=[out_f, out_f, out_b, out_b],
        out_shape=[o_shape] * 4,
        scratch_shapes=[pltpu.VMEM((2, HG_HEADS, HG_DV, HG_DK), F32),
                        pltpu.VMEM((2, RET_HEADS, RET_DV, RET_DK), F32)],
        compiler_params=_cparams("arbitrary", "arbitrary"),
        name="rec_scan",
    )(p0, p0, p0, p0, p0, cs, sn, p0, p0, p0, p0, p0, cs, sn, lb, dmat, qs, ks, g64)


def _rec_merge_kernel(ohf, ohb, orf, orb, hg_ref, rg_ref, hgain, rgain, yh_ref, yr_ref):
    oh = ohf[...] + ohb[...]
    orr = orf[...] + orb[...]
    hg = hg_ref[...]
    rg = rg_ref[...]
    for h in range(HG_HEADS):
        sl = slice(h * LANES, (h + 1) * LANES)
        o = oh[:, sl]
        y = o * lax.rsqrt(jnp.mean(o * o, axis=-1, keepdims=True) + NORM_EPS) * hgain[...]
        yh_ref[:, sl] = (y * _silu(hg[:, sl])).astype(BF16)
        o = orr[:, sl]
        oc = o - jnp.mean(o, axis=-1, keepdims=True)
        y = oc * lax.rsqrt(jnp.mean(oc * oc, axis=-1, keepdims=True) + NORM_EPS) * rgain[...]
        yr_ref[:, sl] = (y * _silu(rg[:, sl])).astype(BF16)


def _rec_merge_call(ohf, orf, ohb, orb, p0, hgain, rgain, *, tm):
    nt = p0.shape[0]
    blk = pl.BlockSpec((tm, HALF), lambda i: (i, 0))
    vec = pl.BlockSpec((1, LANES), lambda i: (0, 0))
    y_shape = jax.ShapeDtypeStruct((nt, HALF), BF16)
    return pl.pallas_call(
        _rec_merge_kernel,
        grid=(nt // tm,),
        in_specs=[blk, blk, blk, blk,
                  pl.BlockSpec((tm, HALF), lambda i: (i, 4)), pl.BlockSpec((tm, HALF), lambda i: (i, 7)), vec, vec],
        out_specs=[blk, blk],
        out_shape=[y_shape, y_shape],
        compiler_params=_cparams("arbitrary"),
        name="rec_merge",
    )(ohf, ohb, orf, orb, p0, p0, hgain, rgain)


def _post_kernel(ya_ref, yb_ref, h_ref, mod_ref, wout_ref, gain_ref, wr_ref, br_ref,
                 hn_ref, f_ref, lpos_ref, gate_ref, ne_ref, cb_ref, cnt_ref, cnt_sc, *, d, tm, row_fn):
    i = pl.program_id(0)

    @pl.when(i == 0)
    def _():
        cnt_sc[...] = jnp.zeros_like(cnt_sc)

    r = row_fn(i)
    w_hi, w_lo = _split_bf16(wr_ref[...], 2)
    nt_dot = lambda a, b: lax.dot_general(a, b, _NT, preferred_element_type=F32)
    n_part = 2 if tm % (2 * LANES) == 0 else 1
    logit_parts = []
    for part in range(n_part):
        rs = slice(part * tm // n_part, (part + 1) * tm // n_part)
        y = jnp.concatenate([ya_ref[rs, :], yb_ref[rs, :]], axis=1)
        hn = h_ref[rs, :] + mod_ref[pl.ds(r, 1), 2 * d:3 * d] * _dot(y, wout_ref[...])
        hn_ref[rs, :] = hn
        f = _norm_mod(hn, gain_ref[...], mod_ref[pl.ds(r, 1), 3 * d:4 * d], mod_ref[pl.ds(r, 1), 4 * d:5 * d])
        f_ref[rs, :] = f
        f_hi, f_lo = _split_bf16(f, 2)
        logit_parts.append((nt_dot(w_hi, f_hi) + nt_dot(w_lo, f_hi) + nt_dot(w_hi, f_lo))[0:N_EXPERTS])
    logits = (logit_parts[0] if n_part == 1 else jnp.concatenate(logit_parts, axis=1)) + br_ref[...]
    row = lax.broadcasted_iota(I32, (N_EXPERTS, tm), 0)
    vals, idxs = [], []
    work = logits
    for _ in range(TOP_K):
        m = jnp.max(work, axis=0, keepdims=True)
        sel = jnp.min(jnp.where(work == m, row, N_EXPERTS), axis=0, keepdims=True)
        vals.append(m)
        idxs.append(sel)
        work = jnp.where(row == sel, -jnp.inf, work)
    exps = [jnp.exp(v - vals[0]) for v in vals]
    inv = 1.0 / sum(exps)
    onehot = sum((row == s).astype(F32) for s in idxs)
    n_e = jnp.broadcast_to(jnp.sum(onehot, axis=1, keepdims=True), (N_EXPERTS, LANES))
    rr = lax.broadcasted_iota(I32, (tm, tm), 0)
    cc = lax.broadcasted_iota(I32, (tm, tm), 1)
    within = _dot(onehot.astype(BF16), (rr < cc).astype(F32).astype(BF16))
    ea = lax.broadcasted_iota(I32, (N_EXPERTS, N_EXPERTS), 0)
    eb = lax.broadcasted_iota(I32, (N_EXPERTS, N_EXPERTS), 1)
    before = _dot((eb < ea).astype(F32).astype(BF16), n_e.astype(BF16))[:, 0:1]
    base = within + before
    sub = lax.broadcasted_iota(I32, (8, tm), 0)
    lpos_slab = jnp.zeros((8, tm), F32)
    gate_slab = jnp.zeros((8, tm), F32)
    for k in range(TOP_K):
        gate_slab = jnp.where(sub == k, exps[k] * inv, gate_slab)
        lp = jnp.sum(jnp.where(row == idxs[k], base, 0.0), axis=0, keepdims=True)
        lpos_slab = jnp.where(sub == k, lp, lpos_slab)
    lpos_ref[...] = lpos_slab.astype(I32)
    gate_ref[...] = gate_slab
    ne_ref[0] = n_e
    cb_ref[0] = cnt_sc[...]
    cnt_sc[...] = cnt_sc[...] + n_e
    cnt_ref[...] = cnt_sc[...]


def _post_call(ya, yb, h, mod, wout, gain, wr, br, *, tm, n_tiles, y_block, h_block, row_fn):
    d = h.shape[1]
    n = n_tiles * tm
    tile = lambda w: pl.BlockSpec((tm, w), lambda i: (i, 0))
    const = lambda a: pl.BlockSpec(a.shape, lambda i: (0,) * a.ndim)
    routing = pl.BlockSpec((8, tm), lambda i: (i, 0))
    counts = pl.BlockSpec((1, N_EXPERTS, LANES), lambda i: (i, 0, 0))
    return pl.pallas_call(
        functools.partial(_post_kernel, d=d, tm=tm, row_fn=row_fn),
        grid=(n_tiles,),
        in_specs=[pl.BlockSpec((tm, HALF), lambda i: (y_block(i), 0)),
                  pl.BlockSpec((tm, HALF), lambda i: (y_block(i), 0)),
                  pl.BlockSpec((tm, d), lambda i: (h_block(i), 0)),
                  const(mod), const(wout), const(gain), const(wr), const(br)],
        out_specs=[tile(d), tile(d), routing, routing, counts, counts,
                   pl.BlockSpec((N_EXPERTS, LANES), lambda i: (0, 0))],
        out_shape=[jax.ShapeDtypeStruct((n, d), F32), jax.ShapeDtypeStruct((n, d), F32),
                   jax.ShapeDtypeStruct((n_tiles * 8, tm), I32), jax.ShapeDtypeStruct((n_tiles * 8, tm), F32),
                   jax.ShapeDtypeStruct((n_tiles, N_EXPERTS, LANES), F32),
                   jax.ShapeDtypeStruct((n_tiles, N_EXPERTS, LANES), F32),
                   jax.ShapeDtypeStruct((N_EXPERTS, LANES), F32)],
        scratch_shapes=[pltpu.VMEM((N_EXPERTS, LANES), F32)],
        compiler_params=_cparams("arbitrary"),
        name="post_mixer",
    )(ya, yb, h, mod, wout, gain, wr, br)


_TAB = 3 * N_EXPERTS
ROW_DMA_PRIORITY = 1


def _chunk_sizes(limit):
    return [1 << b for b in range(limit.bit_length() - 1, -1, -1)]


def _token_rows(start, count, pitch):
    return pl.ds(pl.multiple_of(start * pitch, pitch), count * pitch)


def _segment_copies(tab_ref, base, local, slot, grouped, sem, *, tm, pitch, to_grouped):
    for e in range(N_EXPERTS):
        g0 = tab_ref[base + e]
        n = tab_ref[base + N_EXPERTS + e]
        l0 = tab_ref[base + 2 * N_EXPERTS + e]
        for size in _chunk_sizes(tm):
            done = jnp.bitwise_and(n, -2 * size)

            @pl.when(jnp.bitwise_and(n, size) != 0)
            def _():
                loc = local.at[slot, _token_rows(l0 + done, size, pitch)]
                grp = grouped.at[_token_rows(g0 + done, size, pitch)]
                (pltpu.make_async_copy(loc, grp, sem) if to_grouped else pltpu.make_async_copy(grp, loc, sem)).start()


def _pad_zero_copies(ztab_ref, zeros, grouped, sem, *, tm, pitch, wait):
    for e in range(N_EXPERTS):
        g0 = ztab_ref[e]
        n = ztab_ref[N_EXPERTS + e]
        for size in _chunk_sizes(tm // 2):
            done = jnp.bitwise_and(n, -2 * size)

            @pl.when(jnp.bitwise_and(n, size) != 0)
            def _():
                cp = pltpu.make_async_copy(zeros.at[pl.ds(0, size * pitch)],
                                           grouped.at[_token_rows(g0 + done, size, pitch)], sem)
                cp.wait() if wait else cp.start()


def _lanes_to_tiles(ref, x):
    rows, pitch = x.shape[0], x.shape[1] // LANES
    for c in range(pitch):
        ref[pl.ds(c, rows, stride=pitch), :] = x[:, c * LANES:(c + 1) * LANES]


def _tiles_to_lanes(ref, pitch):
    rows = ref.shape[0] // pitch
    return jnp.concatenate([ref[pl.ds(c, rows, stride=pitch), :] for c in range(pitch)], axis=1)


def _dispatch_kernel(tab_ref, ztab_ref, f_ref, lpos_ref, xs_ref, zbuf, zeros, sems, zsem, *, tm, pitch):
    i = pl.program_id(0)
    n_tiles = pl.num_programs(0)
    slot = lax.rem(i, 2)
    rows = tm * TOP_K

    def tile_bytes(s):
        return pltpu.make_async_copy(zbuf.at[s], xs_ref.at[pl.ds(0, rows * pitch)], sems.at[s])

    @pl.when(i >= 2)
    def _():
        tile_bytes(slot).wait()

    pos = lax.broadcasted_iota(I32, (rows, tm), 0)
    lp = lpos_ref[...]
    sel = pos == lp[0:1, :]
    for k in range(1, TOP_K):
        sel = jnp.logical_or(sel, pos == lp[k:k + 1, :])
    z = _dot(jnp.where(sel, 1.0, 0.0).astype(BF16), f_ref[...].astype(BF16))
    _lanes_to_tiles(zbuf.at[slot], z)
    _segment_copies(tab_ref, i * _TAB, zbuf, slot, xs_ref, sems.at[slot], tm=tm, pitch=pitch, to_grouped=True)

    @pl.when(i == n_tiles - 1)
    def _():
        zeros[...] = jnp.zeros_like(zeros)
        _pad_zero_copies(ztab_ref, zeros, xs_ref, zsem, tm=tm, pitch=pitch, wait=False)
        _pad_zero_copies(ztab_ref, zeros, xs_ref, zsem, tm=tm, pitch=pitch, wait=True)
        half = tm // 2

        def tail(j, start):
            cp = pltpu.make_async_copy(zeros, xs_ref.at[_token_rows(j * half, half, pitch)], zsem)
            cp.start() if start else cp.wait()

        first, last = ztab_ref[2 * N_EXPERTS], xs_ref.shape[0] // (half * pitch)
        lax.fori_loop(first, last, lambda j, c: (tail(j, True), c)[1], 0)
        lax.fori_loop(first, last, lambda j, c: (tail(j, False), c)[1], 0)
        tile_bytes(slot).wait()
        tile_bytes(1 - slot).wait()


def _dispatch_call(tab, ztab, f, lpos, n_rows, *, tm):
    n, d = f.shape
    assert n // tm >= 2 and tm & (tm - 1) == 0
    pitch = d // LANES
    grid_spec = pltpu.PrefetchScalarGridSpec(
        num_scalar_prefetch=2,
        grid=(n // tm,),
        in_specs=[pl.BlockSpec((tm, d), lambda i, t, z: (i, 0)),
                  pl.BlockSpec((8, tm), lambda i, t, z: (i, 0))],
        out_specs=pl.BlockSpec(memory_space=pl.ANY),
        scratch_shapes=[pltpu.VMEM((2, tm * TOP_K * pitch, LANES), F32), pltpu.VMEM((tm // 2 * pitch, LANES), F32),
                        pltpu.SemaphoreType.DMA((2,)), pltpu.SemaphoreType.DMA(())])
    return pl.pallas_call(
        functools.partial(_dispatch_kernel, tm=tm, pitch=pitch),
        grid_spec=grid_spec,
        out_shape=jax.ShapeDtypeStruct((n_rows * pitch, LANES), F32),
        compiler_params=_cparams("arbitrary"),
        name="moe_dispatch",
    )(tab, ztab, f, lpos)


def _expert_kernel(b0_ref, nb_ref, xs_ref, wgu_ref, bgu_ref, wd_ref, bd_ref, y_ref,
                   xbuf, ybuf, wgu_bf, wd_bf, xsem, ysem, *, de, tm, pitch):
    e = pl.program_id(0)
    nb = nb_ref[e]
    b0 = b0_ref[e]
    rows = tm * pitch

    def block(j):
        return pl.ds(pl.multiple_of(j * rows, rows), rows)

    def x_copy(j, s):
        return pltpu.make_async_copy(xs_ref.at[block(b0 + j)], xbuf.at[s], xsem.at[s])

    def y_copy(j, s):
        return pltpu.make_async_copy(ybuf.at[s], y_ref.at[block(b0 + j)], ysem.at[s])

    @pl.when(nb > 0)
    def _():
        x_copy(0, 0).start(priority=ROW_DMA_PRIORITY)
        wgu_bf[...] = wgu_ref[0, 0].astype(BF16)
        wd_bf[...] = wd_ref[0, 0].astype(BF16)

    def body(j, c):
        s = lax.rem(j, 2)
        x_copy(j, s).wait()

        @pl.when(j + 1 < nb)
        def _():
            x_copy(j + 1, 1 - s).start(priority=ROW_DMA_PRIORITY)

        gu = _dot(_tiles_to_lanes(xbuf.at[s], pitch).astype(BF16), wgu_bf[...]) + bgu_ref[0, 0]
        gate = jnp.minimum(gu[:, 0:de], SWIGLU_LIMIT)
        up = jnp.clip(gu[:, de:2 * de], -SWIGLU_LIMIT, SWIGLU_LIMIT)
        hid = (up + 1.0) * gate * jax.nn.sigmoid(SWIGLU_ALPHA * gate)
        y = _dot(hid.astype(BF16), wd_bf[...]) + bd_ref[0, 0]

        @pl.when(j >= 2)
        def _():
            y_copy(j - 2, s).wait()

        _lanes_to_tiles(ybuf.at[s], y)
        y_copy(j, s).start(priority=ROW_DMA_PRIORITY)
        return c

    lax.fori_loop(0, nb, body, 0)

    @pl.when(nb >= 2)
    def _():
        y_copy(nb - 2, lax.rem(nb, 2)).wait()

    @pl.when(nb >= 1)
    def _():
        y_copy(nb - 1, lax.rem(nb + 1, 2)).wait()

    @pl.when(e == pl.num_programs(0) - 1)
    def _():
        ybuf[0] = jnp.zeros(ybuf.shape[1:], F32)

        def tail(j, start):
            cp = pltpu.make_async_copy(ybuf.at[0], y_ref.at[block(j)], ysem.at[0])
            cp.start() if start else cp.wait()

        first, last = b0 + nb, y_ref.shape[0] // rows
        lax.fori_loop(first, last, lambda j, c: (tail(j, True), c)[1], 0)
        lax.fori_loop(first, last, lambda j, c: (tail(j, False), c)[1], 0)


def _expert_call(blk0, nblk, xs, w_gu, b_gu, w_down, b_down, *, tm, layer):
    nl, ne, d, de2 = w_gu.shape
    pitch = d // LANES
    de = de2 // 2
    grid_spec = pltpu.PrefetchScalarGridSpec(
        num_scalar_prefetch=2,
        grid=(ne,),
        in_specs=[pl.BlockSpec(memory_space=pl.ANY),
                  pl.BlockSpec((1, 1, d, de2), lambda e, b0, nb: (layer, e, 0, 0)),
                  pl.BlockSpec((1, 1, 1, de2), lambda e, b0, nb: (layer, e, 0, 0)),
                  pl.BlockSpec((1, 1, de, d), lambda e, b0, nb: (layer, e, 0, 0)),
                  pl.BlockSpec((1, 1, 1, d), lambda e, b0, nb: (layer, e, 0, 0))],
        out_specs=pl.BlockSpec(memory_space=pl.ANY),
        scratch_shapes=[pltpu.VMEM((2, tm * pitch, LANES), F32), pltpu.VMEM((2, tm * pitch, LANES), F32),
                        pltpu.VMEM((d, de2), BF16), pltpu.VMEM((de, d), BF16),
                        pltpu.SemaphoreType.DMA((2,)), pltpu.SemaphoreType.DMA((2,))])
    return pl.pallas_call(
        functools.partial(_expert_kernel, de=de, tm=tm, pitch=pitch),
        grid_spec=grid_spec,
        out_shape=jax.ShapeDtypeStruct(xs.shape, F32),
        compiler_params=_cparams("arbitrary"),
        name="moe_experts",
    )(blk0, nblk, xs, w_gu, b_gu.reshape(nl, ne, 1, de2), w_down, b_down.reshape(nl, ne, 1, d))


def _combine_kernel(tab_ref, lpos_ref, gate_ref, h_ref, mod_ref, y_ref, o_ref, ybuf, sems, *, d, tm, row_fn):
    i = pl.program_id(0)
    n_tiles = pl.num_programs(0)
    slot = lax.rem(i, 2)
    rows = tm * TOP_K
    pitch = d // LANES

    def fetch(tile, s):
        _segment_copies(tab_ref, tile * _TAB, ybuf, s, y_ref, sems.at[s], tm=tm, pitch=pitch, to_grouped=False)

    @pl.when(i == 0)
    def _():
        fetch(0, 0)

    @pl.when(i + 1 < n_tiles)
    def _():
        fetch(i + 1, 1 - slot)

    pltpu.make_async_copy(y_ref.at[pl.ds(0, rows * pitch)], ybuf.at[slot], sems.at[slot]).wait()
    yb = _tiles_to_lanes(ybuf.at[slot], pitch).astype(BF16)
    pos = lax.broadcasted_iota(I32, (rows, tm), 0)
    lp = lpos_ref[...]
    gates = gate_ref[...]
    g = jnp.zeros((rows, tm), F32)
    for k in range(TOP_K):
        g = jnp.where(pos == lp[k:k + 1, :], gates[k:k + 1, :], g)
    g_hi, g_lo = _split_bf16(g, 2)
    acc = (lax.dot_general(g_hi, yb, _TN, preferred_element_type=F32)
           + lax.dot_general(g_lo, yb, _TN, preferred_element_type=F32))
    r = row_fn(i)
    o_ref[...] = h_ref[...] + mod_ref[pl.ds(r, 1), 5 * d:6 * d] * acc


def _combine_call(tab, lpos, gates, hn, mod, y_rows, *, tm, row_fn):
    n, d = hn.shape
    pitch = d // LANES
    grid_spec = pltpu.PrefetchScalarGridSpec(
        num_scalar_prefetch=1,
        grid=(n // tm,),
        in_specs=[pl.BlockSpec((8, tm), lambda i, t: (i, 0)),
                  pl.BlockSpec((8, tm), lambda i, t: (i, 0)),
                  pl.BlockSpec((tm, d), lambda i, t: (i, 0)),
                  pl.BlockSpec(mod.shape, lambda i, t: (0, 0)),
                  pl.BlockSpec(memory_space=pl.ANY)],
        out_specs=pl.BlockSpec((tm, d), lambda i, t: (i, 0)),
        scratch_shapes=[pltpu.VMEM((2, tm * TOP_K * pitch, LANES), F32), pltpu.SemaphoreType.DMA((2,))])
    return pl.pallas_call(
        functools.partial(_combine_kernel, d=d, tm=tm, row_fn=row_fn),
        grid_spec=grid_spec,
        out_shape=jax.ShapeDtypeStruct((n, d), F32),
        compiler_params=_cparams("arbitrary"),
        name="moe_combine",
    )(tab, lpos, gates, hn, mod, y_rows)


def _moe(f, hn, mod, lpos, gates, ne_t, cb_t, counts, w_gu, b_gu, w_down, b_down, *, tm, row_fn, layer):
    n, _ = f.shape
    n_blocks = -(-n * TOP_K // tm) + N_EXPERTS
    cnt = counts[:, 0].astype(I32)
    padded = (cnt + tm - 1) // tm * tm
    pad_end = jnp.cumsum(padded)
    pad_start = pad_end - padded
    ne = ne_t[:, :, 0].astype(I32)
    cb = cb_t[:, :, 0].astype(I32)
    tab = jnp.concatenate([pad_start[None] + cb, ne, jnp.cumsum(ne, axis=1) - ne], axis=1).reshape(-1)
    n_valid = (pad_end[-1] // tm).astype(I32)
    ztab = jnp.concatenate([pad_start + cnt, padded - cnt, 2 * n_valid[None]])
    xs = _dispatch_call(tab, ztab, f, lpos, n_blocks * tm, tm=tm)
    y_rows = _expert_call(pad_start // tm, padded // tm, xs, w_gu, b_gu, w_down, b_down, tm=tm, layer=layer)
    return _combine_call(tab, lpos, gates, hn, mod, y_rows, tm=tm, row_fn=row_fn)


def _att_in_kernel(h_ref, mod_ref, gain_ref, w_ref, qg_gain, kg_gain, dq_gain, dk_gain, bd_ref,
                   cg_ref, sg_ref, cd_ref, sd_ref,
                   qg_ref, kg_ref, vg_ref, qd_ref, kd_ref, vd_ref, *, d, tpb, nct, nb):
    r = _unified_row(pl.program_id(0), tpb, nct, nb)
    a = _norm_mod(h_ref[...], gain_ref[...], mod_ref[pl.ds(r, 1), 0:d], mod_ref[pl.ds(r, 1), d:2 * d])
    p = _dot(a.astype(BF16), w_ref[...])
    cg, sg = cg_ref[...], sg_ref[...]

    def head_norm_rope(x, gain):
        y = x * lax.rsqrt(jnp.mean(x * x, axis=-1, keepdims=True) + NORM_EPS) * gain
        return _rope(y, cg, sg, GQA_HD).astype(BF16)

    def group_norm_rope(x, gain):
        ss = sum(_dot(piece, bd_ref[...]) for piece in _split_bf16(x * x, 2))
        y = x * lax.rsqrt(ss * (1.0 / DIFF_HD) + NORM_EPS) * gain
        return _rope(y, cd_ref[...], sd_ref[...], DIFF_HD).astype(BF16)

    c0 = 0
    for hh in range(GQA_HEADS):
        qg_ref[:, hh * GQA_HD:(hh + 1) * GQA_HD] = head_norm_rope(p[:, c0:c0 + GQA_HD], qg_gain[...])
        c0 += GQA_HD
    for hh in range(GQA_KV_HEADS):
        kg_ref[:, hh * GQA_HD:(hh + 1) * GQA_HD] = head_norm_rope(p[:, c0:c0 + GQA_HD], kg_gain[...])
        c0 += GQA_HD
    wv = GQA_KV_HEADS * GQA_HD
    vg_ref[...] = p[:, c0:c0 + wv].astype(BF16)
    c0 += wv
    qd_ref[...] = group_norm_rope(p[:, c0:c0 + HALF], dq_gain[...])
    c0 += HALF
    kd_ref[...] = group_norm_rope(p[:, c0:c0 + HALF], dk_gain[...])
    c0 += HALF
    vd_ref[...] = p[:, c0:c0 + HALF].astype(BF16)


def _att_in_call(h, mod, gain, w, qg_gain, kg_gain, dq_gain, dk_gain, bd, cg, sg, cd, sd, *, tm, tpb, nct, nb):
    nt, d = h.shape
    const = lambda a: pl.BlockSpec(a.shape, lambda i: (0,) * a.ndim)
    tab = lambda w_: pl.BlockSpec((tm, w_), lambda i: (lax.rem(i, tpb), 0))
    out = lambda w_: pl.BlockSpec((tm, w_), lambda i: (i, 0))
    shp = lambda w_: jax.ShapeDtypeStruct((nt, w_), BF16)
    wkv = GQA_KV_HEADS * GQA_HD
    return pl.pallas_call(
        functools.partial(_att_in_kernel, d=d, tpb=tpb, nct=nct, nb=nb),
        grid=(nt // tm,),
        in_specs=[pl.BlockSpec((tm, d), lambda i: (i, 0)), const(mod), const(gain), const(w),
                  const(qg_gain), const(kg_gain), const(dq_gain), const(dk_gain), const(bd),
                  tab(GQA_HD), tab(GQA_HD), tab(HALF), tab(HALF)],
        out_specs=[out(HALF), out(wkv), out(wkv), out(HALF), out(HALF), out(HALF)],
        out_shape=[shp(HALF), shp(wkv), shp(wkv), shp(HALF), shp(HALF), shp(HALF)],
        compiler_params=_cparams("arbitrary"),
        name="att_in_proj",
    )(h, mod, gain, w, qg_gain, kg_gain, dq_gain, dk_gain, bd, cg, sg, cd, sd)


def _softmax_parts(q, k):
    s = lax.dot_general(q, k, _NT, preferred_element_type=F32)
    e = jnp.exp2(s - jnp.max(s, axis=-1, keepdims=True))
    return e, 1.0 / jnp.sum(e, axis=-1, keepdims=True)


def _gqa_kernel(q_ref, k_ref, v_ref, o_ref, *, group):
    for g in range(group):
        sl = slice(g * GQA_HD, (g + 1) * GQA_HD)
        e, inv = _softmax_parts(q_ref[:, sl], k_ref[...])
        o_ref[:, sl] = (_dot(e.astype(BF16), v_ref[...]) * inv).astype(BF16)


def _gqa_call(qg, kg, vg, *, nb, tq, tot, nqt, lat0):
    group = GQA_HEADS // GQA_KV_HEADS
    upb = tot // tq
    return pl.pallas_call(
        functools.partial(_gqa_kernel, group=group),
        grid=(nb, GQA_KV_HEADS, nqt),
        in_specs=[pl.BlockSpec((tq, group * GQA_HD), lambda b, h, q: (b * upb + lat0 + q, h)),
                  pl.BlockSpec((tot, GQA_HD), lambda b, h, q: (b, h)),
                  pl.BlockSpec((tot, GQA_HD), lambda b, h, q: (b, h))],
        out_specs=pl.BlockSpec((tq, group * GQA_HD), lambda b, h, q: (b * nqt + q, h)),
        out_shape=jax.ShapeDtypeStruct((nb * nqt * tq, HALF), BF16),
        compiler_params=_cparams("arbitrary", "arbitrary", "arbitrary"),
        name="gqa_attention",
    )(qg, kg, vg)


def _diff_kernel(q_ref, k_ref, v_ref, lam_ref, gain_ref, o_ref, *, lam_init):
    lp = lam_ref[...]
    lam = (jnp.exp(jnp.sum(lp[0:1] * lp[1:2], axis=-1, keepdims=True))
           - jnp.exp(jnp.sum(lp[2:3] * lp[3:4], axis=-1, keepdims=True)) + lam_init)
    hd2 = 2 * DIFF_HD
    for j in range(q_ref.shape[1] // hd2):
        sl = slice(j * hd2, (j + 1) * hd2)
        q = q_ref[:, sl]
        k = k_ref[:, sl]
        lane = lax.broadcasted_iota(I32, q.shape, 1)
        zero = jnp.zeros_like(q)
        e1, inv1 = _softmax_parts(jnp.where(lane < DIFF_HD, q, zero), k)
        e2, inv2 = _softmax_parts(jnp.where(lane >= DIFF_HD, q, zero), k)
        a = e1 * inv1 - e2 * (lam * inv2)
        o = _dot(a.astype(BF16), v_ref[:, sl])
        y = o * lax.rsqrt(jnp.mean(o * o, axis=-1, keepdims=True) + NORM_EPS) * gain_ref[...]
        o_ref[:, sl] = (y * (1.0 - lam_init)).astype(BF16)


def _diff_call(qd, kd, vd, lam_params, gain, *, nb, tq, tot, nqt, lat0, lam_init):
    upb = tot // tq
    hd2 = DIFF_HEADS * (2 * DIFF_HD)
    return pl.pallas_call(
        functools.partial(_diff_kernel, lam_init=lam_init),
        grid=(nb, HALF // hd2, nqt),
        in_specs=[pl.BlockSpec((tq, hd2), lambda b, h, q: (b * upb + lat0 + q, h)),
                  pl.BlockSpec((tot, hd2), lambda b, h, q: (b, h)),
                  pl.BlockSpec((tot, hd2), lambda b, h, q: (b, h)),
                  pl.BlockSpec(lam_params.shape, lambda b, h, q: (0, 0)),
                  pl.BlockSpec(gain.shape, lambda b, h, q: (0, 0))],
        out_specs=pl.BlockSpec((tq, hd2), lambda b, h, q: (b * nqt + q, h)),
        out_shape=jax.ShapeDtypeStruct((nb * nqt * tq, HALF), BF16),
        compiler_params=_cparams("arbitrary", "arbitrary", "arbitrary"),
        name="diff_attention",
    )(qd, kd, vd, lam_params, gain)


def _deinterleave(hd):
    return np.concatenate([np.arange(0, hd, 2), np.arange(1, hd, 2)])


def _head_perm(n_heads, hd):
    return np.concatenate([h * hd + _deinterleave(hd) for h in range(n_heads)])


def _rope_tables(n_lat, n_ctx, hd, reps):
    t = np.arange(n_lat)
    row, col = (t // GRID_W).astype(np.float64), (t % GRID_W).astype(np.float64)
    axis_dim = hd // 2
    inv_freq = ROPE_THETA ** (-np.arange(0, axis_dim, 2, dtype=np.float64) / axis_dim)
    ang = np.concatenate([row[:, None] * inv_freq, col[:, None] * inv_freq], axis=-1)
    cos = np.concatenate([np.ones((n_ctx, hd // 2)), np.cos(ang)], axis=0)
    sin = np.concatenate([np.zeros((n_ctx, hd // 2)), np.sin(ang)], axis=0)
    c = np.tile(np.concatenate([cos, cos], axis=1), (1, reps))
    s = np.tile(np.concatenate([-sin, sin], axis=1), (1, reps))
    return jnp.asarray(c, F32), jnp.asarray(s, F32)


def _retention_tables(n):
    gam = 1.0 - 2.0 ** (-5.0 - np.arange(RET_HEADS, dtype=np.float64))
    gams = (gam, gam[::-1])
    c = np.arange(n, dtype=np.float64)
    diff = c[:, None] - c[None, :]
    dmat = np.zeros((2, RET_HEADS, n, n))
    qs = np.zeros((2, n, RET_HEADS * RET_DK))
    ks = np.zeros((2, n, RET_HEADS * RET_DK))
    gpow = np.zeros((2, 8, RET_HEADS * RET_DK))
    for d in range(2):
        for h in range(RET_HEADS):
            g = gams[d][h]
            sl = slice(h * RET_DK, (h + 1) * RET_DK)
            if d == 0:
                dmat[d, h] = np.where(diff >= 0, g ** np.maximum(diff, 0), 0.0)
                qs[d, :, sl] = (g ** (c + 1))[:, None]
                ks[d, :, sl] = (g ** (n - 1 - c))[:, None]
            else:
                dmat[d, h] = np.where(diff <= 0, g ** np.maximum(-diff, 0), 0.0)
                qs[d, :, sl] = (g ** (n - c))[:, None]
                ks[d, :, sl] = (g ** c)[:, None]
            gpow[d, :, sl] = g ** n
    return tuple(jnp.asarray(a, F32) for a in (dmat, qs, ks, gpow))


def _plan(nb, n_lat, n_ctx):
    tm = 256 if (n_ctx % 256 == 0 and n_lat % 256 == 0) else 128
    assert n_ctx % tm == 0 and n_lat % tm == 0 and tm % CHUNK == 0 and n_lat % GRID_W == 0
    tq = tm
    return tm, tq


def kernel(x, c, ctx, c_ctx, norm_mix, norm_ffn, w_ada, b_ada, rec_w_in, rec_lb_logits, rec_w_out, rec_hg_gain,
           rec_ret_gain, att_w_in, att_w_out, att_q_gain, att_k_gain, diff_q_gain, diff_k_gain, diff_lambda,
           diff_gain, w_router, b_router, w_gu, b_gu, w_down, b_down):
    nb, n_lat, d = x.shape
    n_ctx = ctx.shape[1]
    assert w_ada.shape[0] == 2, "two layers: recurrent mixer then attention mixer"
    tm, tq = _plan(nb, n_lat, n_ctx)
    tot = n_ctx + n_lat
    tpb, nct = tot // tm, n_ctx // tm
    nt = nb * tot
    nql = n_lat // tm

    unified_row = functools.partial(_unified_row, tpb=tpb, nct=nct, nb=nb)
    latent_row = lambda i: lax.div(i, nql)
    latent_block = lambda i: lax.div(i, nql) * tpb + nct + lax.rem(i, nql)
    ident = lambda i: i

    n_cond = -(-(nb + 1) // 8) * 8
    cond = jnp.concatenate([c, c_ctx[None], jnp.zeros((n_cond - nb - 1, d), F32)], axis=0)
    mod = _ada_call(cond, w_ada, b_ada)

    wr = jnp.pad(jnp.swapaxes(w_router, 1, 2), ((0, 0), (0, E_PAD - N_EXPERTS), (0, 0)))
    br = b_router[:, :, None]

    h = jnp.concatenate([ctx, x], axis=1).reshape(nt, d)

    lb = jnp.cumsum(jax.nn.softmax(rec_lb_logits.astype(F32), axis=0), axis=0)[0]
    w_in = rec_w_in[0]
    c_rq = HG_HEADS * HG_DK * 3 + HG_HEADS * HG_DV * 2
    nrk = RET_HEADS * RET_DK
    perm = _head_perm(RET_HEADS, RET_DK)
    w_in = jnp.concatenate([w_in[:, :c_rq], w_in[:, c_rq + perm], w_in[:, c_rq + nrk + perm] * (RET_DK ** -0.5),
                            w_in[:, c_rq + 2 * nrk:]], axis=1).astype(BF16)
    cs_r, sn_r = _rope_tables(n_lat, n_ctx, RET_DK, 2 * RET_HEADS)
    dmat, qs, ks, g64 = _retention_tables(tm)
    p0 = _inproj_call(h, mod[0], norm_mix[0][None], w_in, tm=tm, tpb=tpb, nct=nct, nb=nb)
    ohf, orf, ohb, orb = _rec_call(p0, lb, cs_r, sn_r, dmat, qs, ks, g64, nb=nb, tc=tm, tpb=tpb, nct=nct)
    yh, yr = _rec_merge_call(ohf, orf, ohb, orb, p0, rec_hg_gain[0][None], rec_ret_gain[0][None], tm=tm)
    hn, f, *routing = _post_call(
        yh, yr, h, mod[0], rec_w_out[0].astype(BF16), norm_ffn[0][None], wr[0], br[0],
        tm=tm, n_tiles=nt // tm, y_block=ident, h_block=ident, row_fn=unified_row)
    h = _moe(f, hn, mod[0], *routing, w_gu, b_gu, w_down, b_down, tm=tm, row_fn=unified_row, layer=0)

    pg, pd = _head_perm(GQA_HEADS, GQA_HD), _head_perm(2 * DIFF_HEADS, DIFF_HD)
    w_in = att_w_in[0]
    o_gk = GQA_HEADS * GQA_HD
    o_gv = o_gk + GQA_KV_HEADS * GQA_HD
    o_dq = o_gv + GQA_KV_HEADS * GQA_HD
    o_dk = o_dq + HALF
    o_dv = o_dk + HALF
    w_in = jnp.concatenate([w_in[:, pg], w_in[:, o_gk + pg[:GQA_KV_HEADS * GQA_HD]], w_in[:, o_gv:o_dq],
                            w_in[:, o_dq + pd], w_in[:, o_dk + pd], w_in[:, o_dv:]], axis=1).astype(BF16)
    dg, dd = _deinterleave(GQA_HD), _deinterleave(DIFF_HD)
    cg, sg = _rope_tables(n_lat, n_ctx, GQA_HD, 1)
    cd, sd = _rope_tables(n_lat, n_ctx, DIFF_HD, 2 * DIFF_HEADS)
    bd = jnp.asarray(np.kron(np.eye(2 * DIFF_HEADS), np.ones((DIFF_HD, DIFF_HD))), BF16)
    log2e = math.log2(math.e)
    qg, kg, vg, qd, kd, vd = _att_in_call(
        h, mod[1], norm_mix[1][None], w_in,
        (att_q_gain[0][dg] * (GQA_HD ** -0.5 * log2e))[None], att_k_gain[0][dg][None],
        jnp.tile(diff_q_gain[0][dd] * (DIFF_HD ** -0.5 * log2e), 2 * DIFF_HEADS)[None],
        jnp.tile(diff_k_gain[0][dd], 2 * DIFF_HEADS)[None],
        bd, cg, sg, cd, sd, tm=tm, tpb=tpb, nct=nct, nb=nb)
    nqt = n_lat // tq
    yg = _gqa_call(qg, kg, vg, nb=nb, tq=tq, tot=tot, nqt=nqt, lat0=n_ctx // tq)
    lam_init = 0.8 - 0.6 * math.exp(-0.3 * 1)
    yd = _diff_call(qd, kd, vd, diff_lambda[0], diff_gain[0][None], nb=nb, tq=tq, tot=tot, nqt=nqt,
                    lat0=n_ctx // tq, lam_init=lam_init)
    hn, f, *routing = _post_call(
        yg, yd, h, mod[1], att_w_out[0].astype(BF16), norm_ffn[1][None], wr[1], br[1],
        tm=tm, n_tiles=nb * nql, y_block=ident, h_block=latent_block, row_fn=latent_row)
    out = _moe(f, hn, mod[1], *routing, w_gu, b_gu, w_down, b_down, tm=tm, row_fn=latent_row, layer=1)
    return out.reshape(nb, n_lat, d)
```

```python
import functools
import math

import numpy as np
import jax
import jax.numpy as jnp
from jax import lax
from jax.experimental import pallas as pl
from jax.experimental.pallas import tpu as pltpu

F32 = jnp.float32
BF16 = jnp.bfloat16
I32 = jnp.int32

GRID_W = 64
HG_HEADS, HG_DK, HG_DV = 4, 128, 128
RET_HEADS, RET_DK, RET_DV = 4, 64, 128
GQA_HEADS, GQA_KV_HEADS, GQA_HD = 4, 2, 128
DIFF_HEADS, DIFF_HD = 4, 64
N_EXPERTS, TOP_K = 32, 4
SWIGLU_LIMIT, SWIGLU_ALPHA = 7.0, 1.702
CHUNK = 64
ROPE_THETA = 10000.0
NORM_EPS = 1e-6

LANES = 128
E_PAD = LANES
V7X_VMEM_LIMIT = 56 * 1024 * 1024
HALF = 512

_NT = (((1,), (1,)), ((), ()))
_TN = (((0,), (0,)), ((), ()))


def _cparams(*sem):
    return pltpu.CompilerParams(dimension_semantics=sem, vmem_limit_bytes=V7X_VMEM_LIMIT)


def _silu(x):
    return x * jax.nn.sigmoid(x)


def _dot(a, b):
    return jnp.dot(a, b, preferred_element_type=F32)


def _split_bf16(x, terms):
    out = []
    for _ in range(terms):
        p = x.astype(BF16)
        out.append(p)
        x = x - p.astype(F32)
    return out


def _norm_mod(x, gain, shift, scale):
    y = x * lax.rsqrt(jnp.mean(x * x, axis=-1, keepdims=True) + NORM_EPS) * gain
    return y * (1.0 + scale) + shift


def _unified_row(i, tpb, nct, nb):
    return jnp.where(lax.rem(i, tpb) < nct, nb, lax.div(i, tpb))


def _ada_kernel(cond_ref, w_ref, b_ref, o_ref):
    s = _silu(cond_ref[...]).astype(BF16)
    o_ref[0] = _dot(s, w_ref[0].astype(BF16)) + b_ref[0]


def _ada_call(cond, w_ada, b_ada):
    n_layers, d, d6 = w_ada.shape
    r = cond.shape[0]
    tn = d6 // 4
    return pl.pallas_call(
        _ada_kernel,
        grid=(n_layers, d6 // tn),
        in_specs=[pl.BlockSpec((r, d), lambda l, n: (0, 0)),
                  pl.BlockSpec((1, d, tn), lambda l, n: (l, 0, n)),
                  pl.BlockSpec((1, 1, tn), lambda l, n: (l, 0, n))],
        out_specs=pl.BlockSpec((1, r, tn), lambda l, n: (l, 0, n)),
        out_shape=jax.ShapeDtypeStruct((n_layers, r, d6), F32),
        compiler_params=_cparams("arbitrary", "arbitrary"),
        name="ada_modulation",
    )(cond, w_ada, b_ada.reshape(n_layers, 1, d6))


def _inproj_kernel(h_ref, mod_ref, gain_ref, w_ref, o_ref, *, d, tpb, nct, nb):
    r = _unified_row(pl.program_id(0), tpb, nct, nb)
    a = _norm_mod(h_ref[...], gain_ref[...], mod_ref[pl.ds(r, 1), 0:d], mod_ref[pl.ds(r, 1), d:2 * d])
    o_ref[...] = _dot(a.astype(BF16), w_ref[...])


def _inproj_call(h, mod, gain, w, *, tm, tpb, nct, nb):
    nt, d = h.shape
    n_out = w.shape[1]
    return pl.pallas_call(
        functools.partial(_inproj_kernel, d=d, tpb=tpb, nct=nct, nb=nb),
        grid=(nt // tm,),
        in_specs=[pl.BlockSpec((tm, d), lambda i: (i, 0)),
                  pl.BlockSpec(mod.shape, lambda i: (0, 0)),
                  pl.BlockSpec((1, d), lambda i: (0, 0)),
                  pl.BlockSpec((d, n_out), lambda i: (0, 0))],
        out_specs=pl.BlockSpec((tm, n_out), lambda i: (i, 0)),
        out_shape=jax.ShapeDtypeStruct((nt, n_out), F32),
        compiler_params=_cparams("arbitrary"),
        name="rec_in_proj",
    )(h, mod, gain, w)


def _swap_halves(x, group):
    if group == LANES:
        return pltpu.roll(x, LANES // 2, 1)
    lane = lax.broadcasted_iota(I32, x.shape, 1)
    half = group // 2
    return jnp.where(lax.rem(lane, group) < half, pltpu.roll(x, LANES - half, 1), pltpu.roll(x, half, 1))


def _rope(x, cos, sin, group):
    parts = [_swap_halves(x[:, s:s + LANES], group) for s in range(0, x.shape[1], LANES)]
    rot = parts[0] if len(parts) == 1 else jnp.concatenate(parts, axis=1)
    return x * cos + rot * sin


def _hgrn_chunk(d, r0, hq, hf, hi, lb_ref, ohg, st_hg, tri_bf, tri_mask):
    rows = pl.ds(r0, CHUNK)
    lbv = lb_ref[d:d + 1, :]
    f = lbv + (1.0 - lbv) * jax.nn.sigmoid(hf[rows, :])
    logf = jnp.log(f)
    kk = 1.0 - f
    b = sum(_dot(tri_bf, p) for p in _split_bf16(logf, 3))
    tot = b[CHUNK - 1:CHUNK, :] if d == 0 else b[0:1, :]
    q = _silu(hq[rows, :]) * (HG_DK ** -0.5)
    q_in = (q * jnp.exp(b)).astype(BF16)
    k_in = (kk * jnp.exp(-b)).astype(BF16)
    k_st = (kk * jnp.exp(tot - b)).astype(BF16)
    dec = jnp.exp(tot)
    v = hi[rows, :].astype(BF16)
    for h in range(HG_HEADS):
        sl = slice(h * HG_DK, (h + 1) * HG_DK)
        att = jnp.where(tri_mask, lax.dot_general(q_in[:, sl], k_in[:, sl], _NT, preferred_element_type=F32), 0.0)
        s_t = st_hg[d, h]
        o = _dot(att.astype(BF16), v[:, sl]) + lax.dot_general(
            q_in[:, sl], s_t.astype(BF16), _NT, preferred_element_type=F32)
        ohg[rows, sl] = o
        st_hg[d, h] = s_t * dec[:, sl] + lax.dot_general(v[:, sl], k_st[:, sl], _TN, preferred_element_type=F32)


def _retention_tile(d, rqk, rv, cs, sn, dmat_ref, qs_ref, ks_ref, gpow_ref, oret, st_ret):
    xr = _rope(rqk[...], cs[...], sn[...], RET_DK)
    nq = RET_HEADS * RET_DK
    q_r = xr[:, 0:nq]
    k_r = xr[:, nq:2 * nq]
    q_b = q_r.astype(BF16)
    k_b = k_r.astype(BF16)
    q_sc = (q_r * qs_ref[d]).astype(BF16)
    k_sc = (k_r * ks_ref[d]).astype(BF16)
    vv = rv[...].astype(BF16)
    for h in range(RET_HEADS):
        sk = slice(h * RET_DK, (h + 1) * RET_DK)
        sv = slice(h * RET_DV, (h + 1) * RET_DV)
        att = lax.dot_general(q_b[:, sk], k_b[:, sk], _NT, preferred_element_type=F32) * dmat_ref[d, h]
        s_t = st_ret[d, h]
        o = _dot(att.astype(BF16), vv[:, sv]) + lax.dot_general(
            q_sc[:, sk], s_t.astype(BF16), _NT, preferred_element_type=F32)
        oret[:, sv] = o
        st_ret[d, h] = s_t * gpow_ref[d, 0:1, sk] + lax.dot_general(
            vv[:, sv], k_sc[:, sk], _TN, preferred_element_type=F32)


def _rec_kernel(hq_f, hf_f, hi_f, rqk_f, rv_f, cs_f, sn_f,
                hq_b, hf_b, hi_b, rqk_b, rv_b, cs_b, sn_b,
                lb_ref, dmat_ref, qs_ref, ks_ref, g64_ref,
                ohg_f, oret_f, ohg_b, oret_b, st_hg, st_ret, *, tc):
    @pl.when(pl.program_id(1) == 0)
    def _():
        st_hg[...] = jnp.zeros_like(st_hg)
        st_ret[...] = jnp.zeros_like(st_ret)

    nch = tc // CHUNK
    row = lax.broadcasted_iota(I32, (CHUNK, CHUNK), 0)
    col = lax.broadcasted_iota(I32, (CHUNK, CHUNK), 1)
    mask_f = col <= row
    mask_b = col >= row
    tri_f = mask_f.astype(F32).astype(BF16)
    tri_b = mask_b.astype(F32).astype(BF16)

    ret_tables = (dmat_ref, qs_ref, ks_ref, g64_ref)
    _retention_tile(0, rqk_f, rv_f, cs_f, sn_f, *ret_tables, oret_f, st_ret)
    _retention_tile(1, rqk_b, rv_b, cs_b, sn_b, *ret_tables, oret_b, st_ret)
    for ci in range(nch):
        _hgrn_chunk(0, ci * CHUNK, hq_f, hf_f, hi_f, lb_ref, ohg_f, st_hg, tri_f, mask_f)
        _hgrn_chunk(1, (nch - 1 - ci) * CHUNK, hq_b, hf_b, hi_b, lb_ref, ohg_b, st_hg, tri_b, mask_b)


def _rec_call(p0, lb, cs, sn, dmat, qs, ks, g64, *, nb, tc, tpb, nct):
    nt = p0.shape[0]

    def fwd(b, j):
        return j

    def bwd(b, j):
        return jnp.where(j < nct, nct - 1 - j, tpb + nct - 1 - j)

    def pspec(col, pos):
        return pl.BlockSpec((tc, HALF), lambda b, j: (b * tpb + pos(b, j), col))

    def tspec(pos):
        return pl.BlockSpec((tc, HALF), lambda b, j: (pos(b, j), 0))

    def whole(a):
        return pl.BlockSpec(a.shape, lambda b, j: (0,) * a.ndim)

    in_specs = ([pspec(0, fwd), pspec(1, fwd), pspec(3, fwd), pspec(5, fwd), pspec(6, fwd), tspec(fwd), tspec(fwd)]
                + [pspec(0, bwd), pspec(2, bwd), pspec(3, bwd), pspec(5, bwd), pspec(6, bwd), tspec(bwd), tspec(bwd)]
                + [whole(a) for a in (lb, dmat, qs, ks, g64)])
    out_f = pl.BlockSpec((tc, HALF), lambda b, j: (b * tpb + fwd(b, j), 0))
    out_b = pl.BlockSpec((tc, HALF), lambda b, j: (b * tpb + bwd(b, j), 0))
    o_shape = jax.ShapeDtypeStruct((nt, HALF), F32)
    return pl.pallas_call(
        functools.partial(_rec_kernel, tc=tc),
        grid=(nb, tpb),
        in_specs=in_specs,
        out_specs=[out_f, out_f, out_b, out_b],
        out_shape=[o_shape] * 4,
        scratch_shapes=[pltpu.VMEM((2, HG_HEADS, HG_DV, HG_DK), F32),
                        pltpu.VMEM((2, RET_HEADS, RET_DV, RET_DK), F32)],
        compiler_params=_cparams("arbitrary", "arbitrary"),
        name="rec_scan",
    )(p0, p0, p0, p0, p0, cs, sn, p0, p0, p0, p0, p0, cs, sn, lb, dmat, qs, ks, g64)


def _rec_merge_kernel(ohf, ohb, orf, orb, hg_ref, rg_ref, hgain, rgain, yh_ref, yr_ref):
    oh = ohf[...] + ohb[...]
    orr = orf[...] + orb[...]
    hg = hg_ref[...]
    rg = rg_ref[...]
    for h in range(HG_HEADS):
        sl = slice(h * LANES, (h + 1) * LANES)
        o = oh[:, sl]
        y = o * lax.rsqrt(jnp.mean(o * o, axis=-1, keepdims=True) + NORM_EPS) * hgain[...]
        yh_ref[:, sl] = (y * _silu(hg[:, sl])).astype(BF16)
        o = orr[:, sl]
        oc = o - jnp.mean(o, axis=-1, keepdims=True)
        y = oc * lax.rsqrt(jnp.mean(oc * oc, axis=-1, keepdims=True) + NORM_EPS) * rgain[...]
        yr_ref[:, sl] = (y * _silu(rg[:, sl])).astype(BF16)


def _rec_merge_call(ohf, orf, ohb, orb, p0, hgain, rgain, *, tm):
    nt = p0.shape[0]
    blk = pl.BlockSpec((tm, HALF), lambda i: (i, 0))
    vec = pl.BlockSpec((1, LANES), lambda i: (0, 0))
    y_shape = jax.ShapeDtypeStruct((nt, HALF), BF16)
    return pl.pallas_call(
        _rec_merge_kernel,
        grid=(nt // tm,),
        in_specs=[blk, blk, blk, blk,
                  pl.BlockSpec((tm, HALF), lambda i: (i, 4)), pl.BlockSpec((tm, HALF), lambda i: (i, 7)), vec, vec],
        out_specs=[blk, blk],
        out_shape=[y_shape, y_shape],
        compiler_params=_cparams("arbitrary"),
        name="rec_merge",
    )(ohf, ohb, orf, orb, p0, p0, hgain, rgain)


def _post_kernel(ya_ref, yb_ref, h_ref, mod_ref, wout_ref, gain_ref, wr_ref, br_ref,
                 hn_ref, f_ref, lpos_ref, gate_ref, ne_ref, cb_ref, cnt_ref, cnt_sc, *, d, tm, row_fn):
    i = pl.program_id(0)

    @pl.when(i == 0)
    def _():
        cnt_sc[...] = jnp.zeros_like(cnt_sc)

    r = row_fn(i)
    w_hi, w_lo = _split_bf16(wr_ref[...], 2)
    nt_dot = lambda a, b: lax.dot_general(a, b, _NT, preferred_element_type=F32)
    n_part = 2 if tm % (2 * LANES) == 0 else 1
    logit_parts = []
    for part in range(n_part):
        rs = slice(part * tm // n_part, (part + 1) * tm // n_part)
        y = jnp.concatenate([ya_ref[rs, :], yb_ref[rs, :]], axis=1)
        hn = h_ref[rs, :] + mod_ref[pl.ds(r, 1), 2 * d:3 * d] * _dot(y, wout_ref[...])
        hn_ref[rs, :] = hn
        f = _norm_mod(hn, gain_ref[...], mod_ref[pl.ds(r, 1), 3 * d:4 * d], mod_ref[pl.ds(r, 1), 4 * d:5 * d])
        f_ref[rs, :] = f
        f_hi, f_lo = _split_bf16(f, 2)
        logit_parts.append((nt_dot(w_hi, f_hi) + nt_dot(w_lo, f_hi) + nt_dot(w_hi, f_lo))[0:N_EXPERTS])
    logits = (logit_parts[0] if n_part == 1 else jnp.concatenate(logit_parts, axis=1)) + br_ref[...]
    row = lax.broadcasted_iota(I32, (N_EXPERTS, tm), 0)
    vals, idxs = [], []
    work = logits
    for _ in range(TOP_K):
        m = jnp.max(work, axis=0, keepdims=True)
        sel = jnp.min(jnp.where(work == m, row, N_EXPERTS), axis=0, keepdims=True)
        vals.append(m)
        idxs.append(sel)
        work = jnp.where(row == sel, -jnp.inf, work)
    exps = [jnp.exp(v - vals[0]) for v in vals]
    inv = 1.0 / sum(exps)
    onehot = sum((row == s).astype(F32) for s in idxs)
    n_e = jnp.broadcast_to(jnp.sum(onehot, axis=1, keepdims=True), (N_EXPERTS, LANES))
    rr = lax.broadcasted_iota(I32, (tm, tm), 0)
    cc = lax.broadcasted_iota(I32, (tm, tm), 1)
    within = _dot(onehot.astype(BF16), (rr < cc).astype(F32).astype(BF16))
    ea = lax.broadcasted_iota(I32, (N_EXPERTS, N_EXPERTS), 0)
    eb = lax.broadcasted_iota(I32, (N_EXPERTS, N_EXPERTS), 1)
    before = _dot((eb < ea).astype(F32).astype(BF16), n_e.astype(BF16))[:, 0:1]
    base = within + before
    sub = lax.broadcasted_iota(I32, (8, tm), 0)
    lpos_slab = jnp.zeros((8, tm), F32)
    gate_slab = jnp.zeros((8, tm), F32)
    for k in range(TOP_K):
        gate_slab = jnp.where(sub == k, exps[k] * inv, gate_slab)
        lp = jnp.sum(jnp.where(row == idxs[k], base, 0.0), axis=0, keepdims=True)
        lpos_slab = jnp.where(sub == k, lp, lpos_slab)
    lpos_ref[...] = lpos_slab.astype(I32)
    gate_ref[...] = gate_slab
    ne_ref[0] = n_e
    cb_ref[0] = cnt_sc[...]
    cnt_sc[...] = cnt_sc[...] + n_e
    cnt_ref[...] = cnt_sc[...]


def _post_call(ya, yb, h, mod, wout, gain, wr, br, *, tm, n_tiles, y_block, h_block, row_fn):
    d = h.shape[1]
    n = n_tiles * tm
    tile = lambda w: pl.BlockSpec((tm, w), lambda i: (i, 0))
    const = lambda a: pl.BlockSpec(a.shape, lambda i: (0,) * a.ndim)
    routing = pl.BlockSpec((8, tm), lambda i: (i, 0))
    counts = pl.BlockSpec((1, N_EXPERTS, LANES), lambda i: (i, 0, 0))
    return pl.pallas_call(
        functools.partial(_post_kernel, d=d, tm=tm, row_fn=row_fn),
        grid=(n_tiles,),
        in_specs=[pl.BlockSpec((tm, HALF), lambda i: (y_block(i), 0)),
                  pl.BlockSpec((tm, HALF), lambda i: (y_block(i), 0)),
                  pl.BlockSpec((tm, d), lambda i: (h_block(i), 0)),
                  const(mod), const(wout), const(gain), const(wr), const(br)],
        out_specs=[tile(d), tile(d), routing, routing, counts, counts,
                   pl.BlockSpec((N_EXPERTS, LANES), lambda i: (0, 0))],
        out_shape=[jax.ShapeDtypeStruct((n, d), F32), jax.ShapeDtypeStruct((n, d), F32),
                   jax.ShapeDtypeStruct((n_tiles * 8, tm), I32), jax.ShapeDtypeStruct((n_tiles * 8, tm), F32),
                   jax.ShapeDtypeStruct((n_tiles, N_EXPERTS, LANES), F32),
                   jax.ShapeDtypeStruct((n_tiles, N_EXPERTS, LANES), F32),
                   jax.ShapeDtypeStruct((N_EXPERTS, LANES), F32)],
        scratch_shapes=[pltpu.VMEM((N_EXPERTS, LANES), F32)],
        compiler_params=_cparams("arbitrary"),
        name="post_mixer",
    )(ya, yb, h, mod, wout, gain, wr, br)


_TAB = 3 * N_EXPERTS
ROW_DMA_PRIORITY = 1


def _chunk_sizes(limit):
    return [1 << b for b in range(limit.bit_length() - 1, -1, -1)]


def _token_rows(start, count, pitch):
    return pl.ds(pl.multiple_of(start * pitch, pitch), count * pitch)


def _segment_copies(tab_ref, base, local, slot, grouped, sem, *, tm, pitch, to_grouped):
    for e in range(N_EXPERTS):
        g0 = tab_ref[base + e]
        n = tab_ref[base + N_EXPERTS + e]
        l0 = tab_ref[base + 2 * N_EXPERTS + e]
        for size in _chunk_sizes(tm):
            done = jnp.bitwise_and(n, -2 * size)

            @pl.when(jnp.bitwise_and(n, size) != 0)
            def _():
                loc = local.at[slot, _token_rows(l0 + done, size, pitch)]
                grp = grouped.at[_token_rows(g0 + done, size, pitch)]
                (pltpu.make_async_copy(loc, grp, sem) if to_grouped else pltpu.make_async_copy(grp, loc, sem)).start()


def _pad_zero_copies(ztab_ref, zeros, grouped, sem, *, tm, pitch, wait):
    for e in range(N_EXPERTS):
        g0 = ztab_ref[e]
        n = ztab_ref[N_EXPERTS + e]
        for size in _chunk_sizes(tm // 2):
            done = jnp.bitwise_and(n, -2 * size)

            @pl.when(jnp.bitwise_and(n, size) != 0)
            def _():
                cp = pltpu.make_async_copy(zeros.at[pl.ds(0, size * pitch)],
                                           grouped.at[_token_rows(g0 + done, size, pitch)], sem)
                cp.wait() if wait else cp.start()


def _lanes_to_tiles(ref, x):
    rows, pitch = x.shape[0], x.shape[1] // LANES
    for c in range(pitch):
        ref[pl.ds(c, rows, stride=pitch), :] = x[:, c * LANES:(c + 1) * LANES]


def _tiles_to_lanes(ref, pitch):
    rows = ref.shape[0] // pitch
    return jnp.concatenate([ref[pl.ds(c, rows, stride=pitch), :] for c in range(pitch)], axis=1)


def _dispatch_kernel(tab_ref, ztab_ref, f_ref, lpos_ref, xs_ref, zbuf, zeros, sems, zsem, *, tm, pitch):
    i = pl.program_id(0)
    n_tiles = pl.num_programs(0)
    slot = lax.rem(i, 2)
    rows = tm * TOP_K

    def tile_bytes(s):
        return pltpu.make_async_copy(zbuf.at[s], xs_ref.at[pl.ds(0, rows * pitch)], sems.at[s])

    @pl.when(i >= 2)
    def _():
        tile_bytes(slot).wait()

    pos = lax.broadcasted_iota(I32, (rows, tm), 0)
    lp = lpos_ref[...]
    sel = pos == lp[0:1, :]
    for k in range(1, TOP_K):
        sel = jnp.logical_or(sel, pos == lp[k:k + 1, :])
    z = _dot(jnp.where(sel, 1.0, 0.0).astype(BF16), f_ref[...].astype(BF16))
    _lanes_to_tiles(zbuf.at[slot], z)
    _segment_copies(tab_ref, i * _TAB, zbuf, slot, xs_ref, sems.at[slot], tm=tm, pitch=pitch, to_grouped=True)

    @pl.when(i == n_tiles - 1)
    def _():
        zeros[...] = jnp.zeros_like(zeros)
        _pad_zero_copies(ztab_ref, zeros, xs_ref, zsem, tm=tm, pitch=pitch, wait=False)
        _pad_zero_copies(ztab_ref, zeros, xs_ref, zsem, tm=tm, pitch=pitch, wait=True)
        half = tm // 2

        def tail(j, start):
            cp = pltpu.make_async_copy(zeros, xs_ref.at[_token_rows(j * half, half, pitch)], zsem)
            cp.start() if start else cp.wait()

        first, last = ztab_ref[2 * N_EXPERTS], xs_ref.shape[0] // (half * pitch)
        lax.fori_loop(first, last, lambda j, c: (tail(j, True), c)[1], 0)
        lax.fori_loop(first, last, lambda j, c: (tail(j, False), c)[1], 0)
        tile_bytes(slot).wait()
        tile_bytes(1 - slot).wait()


def _dispatch_call(tab, ztab, f, lpos, n_rows, *, tm):
    n, d = f.shape
    assert n // tm >= 2 and tm & (tm - 1) == 0
    pitch = d // LANES
    grid_spec = pltpu.PrefetchScalarGridSpec(
        num_scalar_prefetch=2,
        grid=(n // tm,),
        in_specs=[pl.BlockSpec((tm, d), lambda i, t, z: (i, 0)),
                  pl.BlockSpec((8, tm), lambda i, t, z: (i, 0))],
        out_specs=pl.BlockSpec(memory_space=pl.ANY),
        scratch_shapes=[pltpu.VMEM((2, tm * TOP_K * pitch, LANES), F32), pltpu.VMEM((tm // 2 * pitch, LANES), F32),
                        pltpu.SemaphoreType.DMA((2,)), pltpu.SemaphoreType.DMA(())])
    return pl.pallas_call(
        functools.partial(_dispatch_kernel, tm=tm, pitch=pitch),
        grid_spec=grid_spec,
        out_shape=jax.ShapeDtypeStruct((n_rows * pitch, LANES), F32),
        compiler_params=_cparams("arbitrary"),
        name="moe_dispatch",
    )(tab, ztab, f, lpos)


def _expert_kernel(b0_ref, nb_ref, xs_ref, wgu_ref, bgu_ref, wd_ref, bd_ref, y_ref,
                   xbuf, ybuf, wgu_bf, wd_bf, xsem, ysem, *, de, tm, pitch):
    e = pl.program_id(0)
    last_e = pl.num_programs(0) - 1
    nb = nb_ref[e]
    b0 = b0_ref[e]
    n_used = b0_ref[last_e] + nb_ref[last_e]
    rows = tm * pitch

    def block(g):
        return pl.ds(pl.multiple_of(g * rows, rows), rows)

    def x_copy(g, s):
        return pltpu.make_async_copy(xs_ref.at[block(g)], xbuf.at[s], xsem.at[s])

    def y_copy(g, s):
        return pltpu.make_async_copy(ybuf.at[s], y_ref.at[block(g)], ysem.at[s])

    @pl.when(jnp.logical_and(e == 0, n_used > 0))
    def _():
        x_copy(0, 0).start(priority=ROW_DMA_PRIORITY)

    @pl.when(nb > 0)
    def _():
        wgu_bf[...] = wgu_ref[0, 0].astype(BF16)
        wd_bf[...] = wd_ref[0, 0].astype(BF16)

    def body(j, c):
        g = b0 + j
        s = lax.rem(g, 2)
        x_copy(g, s).wait()

        @pl.when(g >= 2)
        def _():
            y_copy(g - 2, s).wait()

        @pl.when(g + 1 < n_used)
        def _():
            x_copy(g + 1, 1 - s).start(priority=ROW_DMA_PRIORITY)

        gu = _dot(_tiles_to_lanes(xbuf.at[s], pitch).astype(BF16), wgu_bf[...]) + bgu_ref[0, 0]
        gate = jnp.minimum(gu[:, 0:de], SWIGLU_LIMIT)
        up = jnp.clip(gu[:, de:2 * de], -SWIGLU_LIMIT, SWIGLU_LIMIT)
        hid = (up + 1.0) * gate * jax.nn.sigmoid(SWIGLU_ALPHA * gate)
        _lanes_to_tiles(ybuf.at[s], _dot(hid.astype(BF16), wd_bf[...]) + bd_ref[0, 0])
        y_copy(g, s).start(priority=ROW_DMA_PRIORITY)
        return c

    lax.fori_loop(0, nb, body, 0)

    @pl.when(e == last_e)
    def _():
        @pl.when(n_used >= 2)
        def _():
            y_copy(n_used - 2, lax.rem(n_used, 2)).wait()

        @pl.when(n_used >= 1)
        def _():
            y_copy(n_used - 1, lax.rem(n_used + 1, 2)).wait()

        ybuf[0] = jnp.zeros(ybuf.shape[1:], F32)

        def tail(g, start):
            cp = pltpu.make_async_copy(ybuf.at[0], y_ref.at[block(g)], ysem.at[0])
            cp.start() if start else cp.wait()

        n_blocks = y_ref.shape[0] // rows
        lax.fori_loop(n_used, n_blocks, lambda g, c: (tail(g, True), c)[1], 0)
        lax.fori_loop(n_used, n_blocks, lambda g, c: (tail(g, False), c)[1], 0)


def _expert_call(blk0, nblk, xs, w_gu, b_gu, w_down, b_down, *, tm, layer):
    nl, ne, d, de2 = w_gu.shape
    pitch = d // LANES
    de = de2 // 2
    grid_spec = pltpu.PrefetchScalarGridSpec(
        num_scalar_prefetch=2,
        grid=(ne,),
        in_specs=[pl.BlockSpec(memory_space=pl.ANY),
                  pl.BlockSpec((1, 1, d, de2), lambda e, b0, nb: (layer, e, 0, 0)),
                  pl.BlockSpec((1, 1, 1, de2), lambda e, b0, nb: (layer, e, 0, 0)),
                  pl.BlockSpec((1, 1, de, d), lambda e, b0, nb: (layer, e, 0, 0)),
                  pl.BlockSpec((1, 1, 1, d), lambda e, b0, nb: (layer, e, 0, 0))],
        out_specs=pl.BlockSpec(memory_space=pl.ANY),
        scratch_shapes=[pltpu.VMEM((2, tm * pitch, LANES), F32), pltpu.VMEM((2, tm * pitch, LANES), F32),
                        pltpu.VMEM((d, de2), BF16), pltpu.VMEM((de, d), BF16),
                        pltpu.SemaphoreType.DMA((2,)), pltpu.SemaphoreType.DMA((2,))])
    return pl.pallas_call(
        functools.partial(_expert_kernel, de=de, tm=tm, pitch=pitch),
        grid_spec=grid_spec,
        out_shape=jax.ShapeDtypeStruct(xs.shape, F32),
        compiler_params=_cparams("arbitrary"),
        name="moe_experts",
    )(blk0, nblk, xs, w_gu, b_gu.reshape(nl, ne, 1, de2), w_down, b_down.reshape(nl, ne, 1, d))


def _combine_kernel(tab_ref, lpos_ref, gate_ref, h_ref, mod_ref, y_ref, o_ref, ybuf, sems, *, d, tm, row_fn):
    i = pl.program_id(0)
    n_tiles = pl.num_programs(0)
    slot = lax.rem(i, 2)
    rows = tm * TOP_K
    pitch = d // LANES

    def fetch(tile, s):
        _segment_copies(tab_ref, tile * _TAB, ybuf, s, y_ref, sems.at[s], tm=tm, pitch=pitch, to_grouped=False)

    @pl.when(i == 0)
    def _():
        fetch(0, 0)

    @pl.when(i + 1 < n_tiles)
    def _():
        fetch(i + 1, 1 - slot)

    pltpu.make_async_copy(y_ref.at[pl.ds(0, rows * pitch)], ybuf.at[slot], sems.at[slot]).wait()
    yb = _tiles_to_lanes(ybuf.at[slot], pitch).astype(BF16)
    pos = lax.broadcasted_iota(I32, (rows, tm), 0)
    lp = lpos_ref[...]
    gates = gate_ref[...]
    g = jnp.zeros((rows, tm), F32)
    for k in range(TOP_K):
        g = jnp.where(pos == lp[k:k + 1, :], gates[k:k + 1, :], g)
    g_hi, g_lo = _split_bf16(g, 2)
    acc = (lax.dot_general(g_hi, yb, _TN, preferred_element_type=F32)
           + lax.dot_general(g_lo, yb, _TN, preferred_element_type=F32))
    r = row_fn(i)
    o_ref[...] = h_ref[...] + mod_ref[pl.ds(r, 1), 5 * d:6 * d] * acc


def _combine_call(tab, lpos, gates, hn, mod, y_rows, *, tm, row_fn):
    n, d = hn.shape
    pitch = d // LANES
    grid_spec = pltpu.PrefetchScalarGridSpec(
        num_scalar_prefetch=1,
        grid=(n // tm,),
        in_specs=[pl.BlockSpec((8, tm), lambda i, t: (i, 0)),
                  pl.BlockSpec((8, tm), lambda i, t: (i, 0)),
                  pl.BlockSpec((tm, d), lambda i, t: (i, 0)),
                  pl.BlockSpec(mod.shape, lambda i, t: (0, 0)),
                  pl.BlockSpec(memory_space=pl.ANY)],
        out_specs=pl.BlockSpec((tm, d), lambda i, t: (i, 0)),
        scratch_shapes=[pltpu.VMEM((2, tm * TOP_K * pitch, LANES), F32), pltpu.SemaphoreType.DMA((2,))])
    return pl.pallas_call(
        functools.partial(_combine_kernel, d=d, tm=tm, row_fn=row_fn),
        grid_spec=grid_spec,
        out_shape=jax.ShapeDtypeStruct((n, d), F32),
        compiler_params=_cparams("arbitrary"),
        name="moe_combine",
    )(tab, lpos, gates, hn, mod, y_rows)


def _moe(f, hn, mod, lpos, gates, ne_t, cb_t, counts, w_gu, b_gu, w_down, b_down, *, tm, row_fn, layer):
    n, _ = f.shape
    n_blocks = -(-n * TOP_K // tm) + N_EXPERTS
    cnt = counts[:, 0].astype(I32)
    padded = (cnt + tm - 1) // tm * tm
    pad_end = jnp.cumsum(padded)
    pad_start = pad_end - padded
    ne = ne_t[:, :, 0].astype(I32)
    cb = cb_t[:, :, 0].astype(I32)
    tab = jnp.concatenate([pad_start[None] + cb, ne, jnp.cumsum(ne, axis=1) - ne], axis=1).reshape(-1)
    n_valid = (pad_end[-1] // tm).astype(I32)
    ztab = jnp.concatenate([pad_start + cnt, padded - cnt, 2 * n_valid[None]])
    xs = _dispatch_call(tab, ztab, f, lpos, n_blocks * tm, tm=tm)
    y_rows = _expert_call(pad_start // tm, padded // tm, xs, w_gu, b_gu, w_down, b_down, tm=tm, layer=layer)
    return _combine_call(tab, lpos, gates, hn, mod, y_rows, tm=tm, row_fn=row_fn)


def _att_in_kernel(h_ref, mod_ref, gain_ref, w_ref, qg_gain, kg_gain, dq_gain, dk_gain, bd_ref,
                   cg_ref, sg_ref, cd_ref, sd_ref,
                   qg_ref, kg_ref, vg_ref, qd_ref, kd_ref, vd_ref, *, d, tpb, nct, nb):
    r = _unified_row(pl.program_id(0), tpb, nct, nb)
    a = _norm_mod(h_ref[...], gain_ref[...], mod_ref[pl.ds(r, 1), 0:d], mod_ref[pl.ds(r, 1), d:2 * d])
    p = _dot(a.astype(BF16), w_ref[...])
    cg, sg = cg_ref[...], sg_ref[...]

    def head_norm_rope(x, gain):
        y = x * lax.rsqrt(jnp.mean(x * x, axis=-1, keepdims=True) + NORM_EPS) * gain
        return _rope(y, cg, sg, GQA_HD).astype(BF16)

    def group_norm_rope(x, gain):
        ss = sum(_dot(piece, bd_ref[...]) for piece in _split_bf16(x * x, 2))
        y = x * lax.rsqrt(ss * (1.0 / DIFF_HD) + NORM_EPS) * gain
        return _rope(y, cd_ref[...], sd_ref[...], DIFF_HD).astype(BF16)

    c0 = 0
    for hh in range(GQA_HEADS):
        qg_ref[:, hh * GQA_HD:(hh + 1) * GQA_HD] = head_norm_rope(p[:, c0:c0 + GQA_HD], qg_gain[...])
        c0 += GQA_HD
    for hh in range(GQA_KV_HEADS):
        kg_ref[:, hh * GQA_HD:(hh + 1) * GQA_HD] = head_norm_rope(p[:, c0:c0 + GQA_HD], kg_gain[...])
        c0 += GQA_HD
    wv = GQA_KV_HEADS * GQA_HD
    vg_ref[...] = p[:, c0:c0 + wv].astype(BF16)
    c0 += wv
    qd_ref[...] = group_norm_rope(p[:, c0:c0 + HALF], dq_gain[...])
    c0 += HALF
    kd_ref[...] = group_norm_rope(p[:, c0:c0 + HALF], dk_gain[...])
    c0 += HALF
    vd_ref[...] = p[:, c0:c0 + HALF].astype(BF16)


def _att_in_call(h, mod, gain, w, qg_gain, kg_gain, dq_gain, dk_gain, bd, cg, sg, cd, sd, *, tm, tpb, nct, nb):
    nt, d = h.shape
    const = lambda a: pl.BlockSpec(a.shape, lambda i: (0,) * a.ndim)
    tab = lambda w_: pl.BlockSpec((tm, w_), lambda i: (lax.rem(i, tpb), 0))
    out = lambda w_: pl.BlockSpec((tm, w_), lambda i: (i, 0))
    shp = lambda w_: jax.ShapeDtypeStruct((nt, w_), BF16)
    wkv = GQA_KV_HEADS * GQA_HD
    return pl.pallas_call(
        functools.partial(_att_in_kernel, d=d, tpb=tpb, nct=nct, nb=nb),
        grid=(nt // tm,),
        in_specs=[pl.BlockSpec((tm, d), lambda i: (i, 0)), const(mod), const(gain), const(w),
                  const(qg_gain), const(kg_gain), const(dq_gain), const(dk_gain), const(bd),
                  tab(GQA_HD), tab(GQA_HD), tab(HALF), tab(HALF)],
        out_specs=[out(HALF), out(wkv), out(wkv), out(HALF), out(HALF), out(HALF)],
        out_shape=[shp(HALF), shp(wkv), shp(wkv), shp(HALF), shp(HALF), shp(HALF)],
        compiler_params=_cparams("arbitrary"),
        name="att_in_proj",
    )(h, mod, gain, w, qg_gain, kg_gain, dq_gain, dk_gain, bd, cg, sg, cd, sd)


def _softmax_parts(q, k):
    s = lax.dot_general(q, k, _NT, preferred_element_type=F32)
    e = jnp.exp2(s - jnp.max(s, axis=-1, keepdims=True))
    return e, 1.0 / jnp.sum(e, axis=-1, keepdims=True)


def _gqa_kernel(q_ref, k_ref, v_ref, o_ref, *, group):
    for g in range(group):
        sl = slice(g * GQA_HD, (g + 1) * GQA_HD)
        e, inv = _softmax_parts(q_ref[:, sl], k_ref[...])
        o_ref[:, sl] = (_dot(e.astype(BF16), v_ref[...]) * inv).astype(BF16)


def _gqa_call(qg, kg, vg, *, nb, tq, tot, nqt, lat0):
    group = GQA_HEADS // GQA_KV_HEADS
    upb = tot // tq
    return pl.pallas_call(
        functools.partial(_gqa_kernel, group=group),
        grid=(nb, GQA_KV_HEADS, nqt),
        in_specs=[pl.BlockSpec((tq, group * GQA_HD), lambda b, h, q: (b * upb + lat0 + q, h)),
                  pl.BlockSpec((tot, GQA_HD), lambda b, h, q: (b, h)),
                  pl.BlockSpec((tot, GQA_HD), lambda b, h, q: (b, h))],
        out_specs=pl.BlockSpec((tq, group * GQA_HD), lambda b, h, q: (b * nqt + q, h)),
        out_shape=jax.ShapeDtypeStruct((nb * nqt * tq, HALF), BF16),
        compiler_params=_cparams("arbitrary", "arbitrary", "arbitrary"),
        name="gqa_attention",
    )(qg, kg, vg)


def _diff_kernel(q_ref, k_ref, v_ref, lam_ref, gain_ref, o_ref, *, lam_init):
    lp = lam_ref[...]
    lam = (jnp.exp(jnp.sum(lp[0:1] * lp[1:2], axis=-1, keepdims=True))
           - jnp.exp(jnp.sum(lp[2:3] * lp[3:4], axis=-1, keepdims=True)) + lam_init)
    hd2 = 2 * DIFF_HD
    for j in range(q_ref.shape[1] // hd2):
        sl = slice(j * hd2, (j + 1) * hd2)
        q = q_ref[:, sl]
        k = k_ref[:, sl]
        lane = lax.broadcasted_iota(I32, q.shape, 1)
        zero = jnp.zeros_like(q)
        e1, inv1 = _softmax_parts(jnp.where(lane < DIFF_HD, q, zero), k)
        e2, inv2 = _softmax_parts(jnp.where(lane >= DIFF_HD, q, zero), k)
        a = e1 * inv1 - e2 * (lam * inv2)
        o = _dot(a.astype(BF16), v_ref[:, sl])
        y = o * lax.rsqrt(jnp.mean(o * o, axis=-1, keepdims=True) + NORM_EPS) * gain_ref[...]
        o_ref[:, sl] = (y * (1.0 - lam_init)).astype(BF16)


def _diff_call(qd, kd, vd, lam_params, gain, *, nb, tq, tot, nqt, lat0, lam_init):
    upb = tot // tq
    hd2 = DIFF_HEADS * (2 * DIFF_HD)
    return pl.pallas_call(
        functools.partial(_diff_kernel, lam_init=lam_init),
        grid=(nb, HALF // hd2, nqt),
        in_specs=[pl.BlockSpec((tq, hd2), lambda b, h, q: (b * upb + lat0 + q, h)),
                  pl.BlockSpec((tot, hd2), lambda b, h, q: (b, h)),
                  pl.BlockSpec((tot, hd2), lambda b, h, q: (b, h)),
                  pl.BlockSpec(lam_params.shape, lambda b, h, q: (0, 0)),
                  pl.BlockSpec(gain.shape, lambda b, h, q: (0, 0))],
        out_specs=pl.BlockSpec((tq, hd2), lambda b, h, q: (b * nqt + q, h)),
        out_shape=jax.ShapeDtypeStruct((nb * nqt * tq, HALF), BF16),
        compiler_params=_cparams("arbitrary", "arbitrary", "arbitrary"),
        name="diff_attention",
    )(qd, kd, vd, lam_params, gain)


def _deinterleave(hd):
    return np.concatenate([np.arange(0, hd, 2), np.arange(1, hd, 2)])


def _head_perm(n_heads, hd):
    return np.concatenate([h * hd + _deinterleave(hd) for h in range(n_heads)])


def _rope_tables(n_lat, n_ctx, hd, reps):
    t = np.arange(n_lat)
    row, col = (t // GRID_W).astype(np.float64), (t % GRID_W).astype(np.float64)
    axis_dim = hd // 2
    inv_freq = ROPE_THETA ** (-np.arange(0, axis_dim, 2, dtype=np.float64) / axis_dim)
    ang = np.concatenate([row[:, None] * inv_freq, col[:, None] * inv_freq], axis=-1)
    cos = np.concatenate([np.ones((n_ctx, hd // 2)), np.cos(ang)], axis=0)
    sin = np.concatenate([np.zeros((n_ctx, hd // 2)), np.sin(ang)], axis=0)
    c = np.tile(np.concatenate([cos, cos], axis=1), (1, reps))
    s = np.tile(np.concatenate([-sin, sin], axis=1), (1, reps))
    return jnp.asarray(c, F32), jnp.asarray(s, F32)


def _retention_tables(n):
    gam = 1.0 - 2.0 ** (-5.0 - np.arange(RET_HEADS, dtype=np.float64))
    gams = (gam, gam[::-1])
    c = np.arange(n, dtype=np.float64)
    diff = c[:, None] - c[None, :]
    dmat = np.zeros((2, RET_HEADS, n, n))
    qs = np.zeros((2, n, RET_HEADS * RET_DK))
    ks = np.zeros((2, n, RET_HEADS * RET_DK))
    gpow = np.zeros((2, 8, RET_HEADS * RET_DK))
    for d in range(2):
        for h in range(RET_HEADS):
            g = gams[d][h]
            sl = slice(h * RET_DK, (h + 1) * RET_DK)
            if d == 0:
                dmat[d, h] = np.where(diff >= 0, g ** np.maximum(diff, 0), 0.0)
                qs[d, :, sl] = (g ** (c + 1))[:, None]
                ks[d, :, sl] = (g ** (n - 1 - c))[:, None]
            else:
                dmat[d, h] = np.where(diff <= 0, g ** np.maximum(-diff, 0), 0.0)
                qs[d, :, sl] = (g ** (n - c))[:, None]
                ks[d, :, sl] = (g ** c)[:, None]
            gpow[d, :, sl] = g ** n
    return tuple(jnp.asarray(a, F32) for a in (dmat, qs, ks, gpow))


def _plan(nb, n_lat, n_ctx):
    tm = 256 if (n_ctx % 256 == 0 and n_lat % 256 == 0) else 128
    assert n_ctx % tm == 0 and n_lat % tm == 0 and tm % CHUNK == 0 and n_lat % GRID_W == 0
    tq = tm
    return tm, tq


def kernel(x, c, ctx, c_ctx, norm_mix, norm_ffn, w_ada, b_ada, rec_w_in, rec_lb_logits, rec_w_out, rec_hg_gain,
           rec_ret_gain, att_w_in, att_w_out, att_q_gain, att_k_gain, diff_q_gain, diff_k_gain, diff_lambda,
           diff_gain, w_router, b_router, w_gu, b_gu, w_down, b_down):
    nb, n_lat, d = x.shape
    n_ctx = ctx.shape[1]
    assert w_ada.shape[0] == 2, "two layers: recurrent mixer then attention mixer"
    tm, tq = _plan(nb, n_lat, n_ctx)
    tot = n_ctx + n_lat
    tpb, nct = tot // tm, n_ctx // tm
    nt = nb * tot
    nql = n_lat // tm

    unified_row = functools.partial(_unified_row, tpb=tpb, nct=nct, nb=nb)
    latent_row = lambda i: lax.div(i, nql)
    latent_block = lambda i: lax.div(i, nql) * tpb + nct + lax.rem(i, nql)
    ident = lambda i: i

    n_cond = -(-(nb + 1) // 8) * 8
    cond = jnp.concatenate([c, c_ctx[None], jnp.zeros((n_cond - nb - 1, d), F32)], axis=0)
    mod = _ada_call(cond, w_ada, b_ada)

    wr = jnp.pad(jnp.swapaxes(w_router, 1, 2), ((0, 0), (0, E_PAD - N_EXPERTS), (0, 0)))
    br = b_router[:, :, None]

    h = jnp.concatenate([ctx, x], axis=1).reshape(nt, d)

    lb = jnp.cumsum(jax.nn.softmax(rec_lb_logits.astype(F32), axis=0), axis=0)[0]
    w_in = rec_w_in[0]
    c_rq = HG_HEADS * HG_DK * 3 + HG_HEADS * HG_DV * 2
    nrk = RET_HEADS * RET_DK
    perm = _head_perm(RET_HEADS, RET_DK)
    w_in = jnp.concatenate([w_in[:, :c_rq], w_in[:, c_rq + perm], w_in[:, c_rq + nrk + perm] * (RET_DK ** -0.5),
                            w_in[:, c_rq + 2 * nrk:]], axis=1).astype(BF16)
    cs_r, sn_r = _rope_tables(n_lat, n_ctx, RET_DK, 2 * RET_HEADS)
    dmat, qs, ks, g64 = _retention_tables(tm)
    p0 = _inproj_call(h, mod[0], norm_mix[0][None], w_in, tm=tm, tpb=tpb, nct=nct, nb=nb)
    ohf, orf, ohb, orb = _rec_call(p0, lb, cs_r, sn_r, dmat, qs, ks, g64, nb=nb, tc=tm, tpb=tpb, nct=nct)
    yh, yr = _rec_merge_call(ohf, orf, ohb, orb, p0, rec_hg_gain[0][None], rec_ret_gain[0][None], tm=tm)
    hn, f, *routing = _post_call(
        yh, yr, h, mod[0], rec_w_out[0].astype(BF16), norm_ffn[0][None], wr[0], br[0],
        tm=tm, n_tiles=nt // tm, y_block=ident, h_block=ident, row_fn=unified_row)
    h = _moe(f, hn, mod[0], *routing, w_gu, b_gu, w_down, b_down, tm=tm, row_fn=unified_row, layer=0)

    pg, pd = _head_perm(GQA_HEADS, GQA_HD), _head_perm(2 * DIFF_HEADS, DIFF_HD)
    w_in = att_w_in[0]
    o_gk = GQA_HEADS * GQA_HD
    o_gv = o_gk + GQA_KV_HEADS * GQA_HD
    o_dq = o_gv + GQA_KV_HEADS * GQA_HD
    o_dk = o_dq + HALF
    o_dv = o_dk + HALF
    w_in = jnp.concatenate([w_in[:, pg], w_in[:, o_gk + pg[:GQA_KV_HEADS * GQA_HD]], w_in[:, o_gv:o_dq],
                            w_in[:, o_dq + pd], w_in[:, o_dk + pd], w_in[:, o_dv:]], axis=1).astype(BF16)
    dg, dd = _deinterleave(GQA_HD), _deinterleave(DIFF_HD)
    cg, sg = _rope_tables(n_lat, n_ctx, GQA_HD, 1)
    cd, sd = _rope_tables(n_lat, n_ctx, DIFF_HD, 2 * DIFF_HEADS)
    bd = jnp.asarray(np.kron(np.eye(2 * DIFF_HEADS), np.ones((DIFF_HD, DIFF_HD))), BF16)
    log2e = math.log2(math.e)
    qg, kg, vg, qd, kd, vd = _att_in_call(
        h, mod[1], norm_mix[1][None], w_in,
        (att_q_gain[0][dg] * (GQA_HD ** -0.5 * log2e))[None], att_k_gain[0][dg][None],
        jnp.tile(diff_q_gain[0][dd] * (DIFF_HD ** -0.5 * log2e), 2 * DIFF_HEADS)[None],
        jnp.tile(diff_k_gain[0][dd], 2 * DIFF_HEADS)[None],
        bd, cg, sg, cd, sd, tm=tm, tpb=tpb, nct=nct, nb=nb)
    nqt = n_lat // tq
    yg = _gqa_call(qg, kg, vg, nb=nb, tq=tq, tot=tot, nqt=nqt, lat0=n_ctx // tq)
    lam_init = 0.8 - 0.6 * math.exp(-0.3 * 1)
    yd = _diff_call(qd, kd, vd, diff_lambda[0], diff_gain[0][None], nb=nb, tq=tq, tot=tot, nqt=nqt,
                    lat0=n_ctx // tq, lam_init=lam_init)
    hn, f, *routing = _post_call(
        yg, yd, h, mod[1], att_w_out[0].astype(BF16), norm_ffn[1][None], wr[1], br[1],
        tm=tm, n_tiles=nb * nql, y_block=ident, h_block=latent_block, row_fn=latent_row)
    out = _moe(f, hn, mod[1], *routing, w_gu, b_gu, w_down, b_down, tm=tm, row_fn=latent_row, layer=1)
    return out.reshape(nb, n_lat, d)
```

```python
import functools
import math

import numpy as np
import jax
import jax.numpy as jnp
from jax import lax
from jax.experimental import pallas as pl
from jax.experimental.pallas import tpu as pltpu

F32 = jnp.float32
BF16 = jnp.bfloat16
I32 = jnp.int32

GRID_W = 64
HG_HEADS, HG_DK, HG_DV = 4, 128, 128
RET_HEADS, RET_DK, RET_DV = 4, 64, 128
GQA_HEADS, GQA_KV_HEADS, GQA_HD = 4, 2, 128
DIFF_HEADS, DIFF_HD = 4, 64
N_EXPERTS, TOP_K = 32, 4
SWIGLU_LIMIT, SWIGLU_ALPHA = 7.0, 1.702
CHUNK = 64
ROPE_THETA = 10000.0
NORM_EPS = 1e-6

LANES = 128
E_PAD = LANES
V7X_VMEM_LIMIT = 56 * 1024 * 1024
HALF = 512

_NT = (((1,), (1,)), ((), ()))
_TN = (((0,), (0,)), ((), ()))


def _cparams(*sem):
    return pltpu.CompilerParams(dimension_semantics=sem, vmem_limit_bytes=V7X_VMEM_LIMIT)


def _silu(x):
    return x * jax.nn.sigmoid(x)


def _dot(a, b):
    return jnp.dot(a, b, preferred_element_type=F32)


def _split_bf16(x, terms):
    out = []
    for _ in range(terms):
        p = x.astype(BF16)
        out.append(p)
        x = x - p.astype(F32)
    return out


def _norm_mod(x, gain, shift, scale):
    y = x * lax.rsqrt(jnp.mean(x * x, axis=-1, keepdims=True) + NORM_EPS) * gain
    return y * (1.0 + scale) + shift


def _unified_row(i, tpb, nct, nb):
    return jnp.where(lax.rem(i, tpb) < nct, nb, lax.div(i, tpb))


def _ada_kernel(cond_ref, w_ref, b_ref, o_ref):
    s = _silu(cond_ref[...]).astype(BF16)
    o_ref[0] = _dot(s, w_ref[0].astype(BF16)) + b_ref[0]


def _ada_call(cond, w_ada, b_ada):
    n_layers, d, d6 = w_ada.shape
    r = cond.shape[0]
    tn = d6 // 4
    return pl.pallas_call(
        _ada_kernel,
        grid=(n_layers, d6 // tn),
        in_specs=[pl.BlockSpec((r, d), lambda l, n: (0, 0)),
                  pl.BlockSpec((1, d, tn), lambda l, n: (l, 0, n)),
                  pl.BlockSpec((1, 1, tn), lambda l, n: (l, 0, n))],
        out_specs=pl.BlockSpec((1, r, tn), lambda l, n: (l, 0, n)),
        out_shape=jax.ShapeDtypeStruct((n_layers, r, d6), F32),
        compiler_params=_cparams("arbitrary", "arbitrary"),
        name="ada_modulation",
    )(cond, w_ada, b_ada.reshape(n_layers, 1, d6))


def _inproj_kernel(h_ref, mod_ref, gain_ref, w_ref, o_ref, *, d, tpb, nct, nb):
    r = _unified_row(pl.program_id(0), tpb, nct, nb)
    a = _norm_mod(h_ref[...], gain_ref[...], mod_ref[pl.ds(r, 1), 0:d], mod_ref[pl.ds(r, 1), d:2 * d])
    o_ref[...] = _dot(a.astype(BF16), w_ref[...])


def _inproj_call(h, mod, gain, w, *, tm, tpb, nct, nb):
    nt, d = h.shape
    n_out = w.shape[1]
    return pl.pallas_call(
        functools.partial(_inproj_kernel, d=d, tpb=tpb, nct=nct, nb=nb),
        grid=(nt // tm,),
        in_specs=[pl.BlockSpec((tm, d), lambda i: (i, 0)),
                  pl.BlockSpec(mod.shape, lambda i: (0, 0)),
                  pl.BlockSpec((1, d), lambda i: (0, 0)),
                  pl.BlockSpec((d, n_out), lambda i: (0, 0))],
        out_specs=pl.BlockSpec((tm, n_out), lambda i: (i, 0)),
        out_shape=jax.ShapeDtypeStruct((nt, n_out), F32),
        compiler_params=_cparams("arbitrary"),
        name="rec_in_proj",
    )(h, mod, gain, w)


def _swap_halves(x, group):
    if group == LANES:
        return pltpu.roll(x, LANES // 2, 1)
    lane = lax.broadcasted_iota(I32, x.shape, 1)
    half = group // 2
    return jnp.where(lax.rem(lane, group) < half, pltpu.roll(x, LANES - half, 1), pltpu.roll(x, half, 1))


def _rope(x, cos, sin, group):
    parts = [_swap_halves(x[:, s:s + LANES], group) for s in range(0, x.shape[1], LANES)]
    rot = parts[0] if len(parts) == 1 else jnp.concatenate(parts, axis=1)
    return x * cos + rot * sin


def _hgrn_chunk(d, r0, hq, hf, hi, lb_ref, ohg, st_hg, tri_bf, tri_mask):
    rows = pl.ds(r0, CHUNK)
    lbv = lb_ref[d:d + 1, :]
    f = lbv + (1.0 - lbv) * jax.nn.sigmoid(hf[rows, :])
    logf = jnp.log(f)
    kk = 1.0 - f
    b = sum(_dot(tri_bf, p) for p in _split_bf16(logf, 3))
    tot = b[CHUNK - 1:CHUNK, :] if d == 0 else b[0:1, :]
    q = _silu(hq[rows, :]) * (HG_DK ** -0.5)
    q_in = (q * jnp.exp(b)).astype(BF16)
    k_in = (kk * jnp.exp(-b)).astype(BF16)
    k_st = (kk * jnp.exp(tot - b)).astype(BF16)
    dec = jnp.exp(tot)
    v = hi[rows, :].astype(BF16)
    for h in range(HG_HEADS):
        sl = slice(h * HG_DK, (h + 1) * HG_DK)
        att = jnp.where(tri_mask, lax.dot_general(q_in[:, sl], k_in[:, sl], _NT, preferred_element_type=F32), 0.0)
        s_t = st_hg[d, h]
        o = _dot(att.astype(BF16), v[:, sl]) + lax.dot_general(
            q_in[:, sl], s_t.astype(BF16), _NT, preferred_element_type=F32)
        ohg[rows, sl] = o
        st_hg[d, h] = s_t * dec[:, sl] + lax.dot_general(v[:, sl], k_st[:, sl], _TN, preferred_element_type=F32)


def _retention_tile(d, rqk, rv, cs, sn, dmat_ref, qs_ref, ks_ref, gpow_ref, oret, st_ret):
    xr = _rope(rqk[...], cs[...], sn[...], RET_DK)
    nq = RET_HEADS * RET_DK
    q_r = xr[:, 0:nq]
    k_r = xr[:, nq:2 * nq]
    q_b = q_r.astype(BF16)
    k_b = k_r.astype(BF16)
    q_sc = (q_r * qs_ref[d]).astype(BF16)
    k_sc = (k_r * ks_ref[d]).astype(BF16)
    vv = rv[...].astype(BF16)
    for h in range(RET_HEADS):
        sk = slice(h * RET_DK, (h + 1) * RET_DK)
        sv = slice(h * RET_DV, (h + 1) * RET_DV)
        att = lax.dot_general(q_b[:, sk], k_b[:, sk], _NT, preferred_element_type=F32) * dmat_ref[d, h]
        s_t = st_ret[d, h]
        o = _dot(att.astype(BF16), vv[:, sv]) + lax.dot_general(
            q_sc[:, sk], s_t.astype(BF16), _NT, preferred_element_type=F32)
        oret[:, sv] = o
        st_ret[d, h] = s_t * gpow_ref[d, 0:1, sk] + lax.dot_general(
            vv[:, sv], k_sc[:, sk], _TN, preferred_element_type=F32)


def _rec_kernel(hq_f, hf_f, hi_f, rqk_f, rv_f, cs_f, sn_f,
                hq_b, hf_b, hi_b, rqk_b, rv_b, cs_b, sn_b,
                lb_ref, dmat_ref, qs_ref, ks_ref, g64_ref,
                ohg_f, oret_f, ohg_b, oret_b, st_hg, st_ret, *, tc):
    @pl.when(pl.program_id(1) == 0)
    def _():
        st_hg[...] = jnp.zeros_like(st_hg)
        st_ret[...] = jnp.zeros_like(st_ret)

    nch = tc // CHUNK
    row = lax.broadcasted_iota(I32, (CHUNK, CHUNK), 0)
    col = lax.broadcasted_iota(I32, (CHUNK, CHUNK), 1)
    mask_f = col <= row
    mask_b = col >= row
    tri_f = mask_f.astype(F32).astype(BF16)
    tri_b = mask_b.astype(F32).astype(BF16)

    ret_tables = (dmat_ref, qs_ref, ks_ref, g64_ref)
    _retention_tile(0, rqk_f, rv_f, cs_f, sn_f, *ret_tables, oret_f, st_ret)
    _retention_tile(1, rqk_b, rv_b, cs_b, sn_b, *ret_tables, oret_b, st_ret)
    for ci in range(nch):
        _hgrn_chunk(0, ci * CHUNK, hq_f, hf_f, hi_f, lb_ref, ohg_f, st_hg, tri_f, mask_f)
        _hgrn_chunk(1, (nch - 1 - ci) * CHUNK, hq_b, hf_b, hi_b, lb_ref, ohg_b, st_hg, tri_b, mask_b)


def _rec_call(p0, lb, cs, sn, dmat, qs, ks, g64, *, nb, tc, tpb, nct):
    nt = p0.shape[0]

    def fwd(b, j):
        return j

    def bwd(b, j):
        return jnp.where(j < nct, nct - 1 - j, tpb + nct - 1 - j)

    def pspec(col, pos):
        return pl.BlockSpec((tc, HALF), lambda b, j: (b * tpb + pos(b, j), col))

    def tspec(pos):
        return pl.BlockSpec((tc, HALF), lambda b, j: (pos(b, j), 0))

    def whole(a):
        return pl.BlockSpec(a.shape, lambda b, j: (0,) * a.ndim)

    in_specs = ([pspec(0, fwd), pspec(1, fwd), pspec(3, fwd), pspec(5, fwd), pspec(6, fwd), tspec(fwd), tspec(fwd)]
                + [pspec(0, bwd), pspec(2, bwd), pspec(3, bwd), pspec(5, bwd), pspec(6, bwd), tspec(bwd), tspec(bwd)]
                + [whole(a) for a in (lb, dmat, qs, ks, g64)])
    out_f = pl.BlockSpec((tc, HALF), lambda b, j: (b * tpb + fwd(b, j), 0))
    out_b = pl.BlockSpec((tc, HALF), lambda b, j: (b * tpb + bwd(b, j), 0))
    o_shape = jax.ShapeDtypeStruct((nt, HALF), F32)
    return pl.pallas_call(
        functools.partial(_rec_kernel, tc=tc),
        grid=(nb, tpb),
        in_specs=in_specs,
        out_specs=[out_f, out_f, out_b, out_b],
        out_shape=[o_shape] * 4,
        scratch_shapes=[pltpu.VMEM((2, HG_HEADS, HG_DV, HG_DK), F32),
                        pltpu.VMEM((2, RET_HEADS, RET_DV, RET_DK), F32)],
        compiler_params=_cparams("arbitrary", "arbitrary"),
        name="rec_scan",
    )(p0, p0, p0, p0, p0, cs, sn, p0, p0, p0, p0, p0, cs, sn, lb, dmat, qs, ks, g64)


def _rec_merge_kernel(ohf, ohb, orf, orb, hg_ref, rg_ref, hgain, rgain, yh_ref, yr_ref):
    oh = ohf[...] + ohb[...]
    orr = orf[...] + orb[...]
    hg = hg_ref[...]
    rg = rg_ref[...]
    for h in range(HG_HEADS):
        sl = slice(h * LANES, (h + 1) * LANES)
        o = oh[:, sl]
        y = o * lax.rsqrt(jnp.mean(o * o, axis=-1, keepdims=True) + NORM_EPS) * hgain[...]
        yh_ref[:, sl] = (y * _silu(hg[:, sl])).astype(BF16)
        o = orr[:, sl]
        oc = o - jnp.mean(o, axis=-1, keepdims=True)
        y = oc * lax.rsqrt(jnp.mean(oc * oc, axis=-1, keepdims=True) + NORM_EPS) * rgain[...]
        yr_ref[:, sl] = (y * _silu(rg[:, sl])).astype(BF16)


def _rec_merge_call(ohf, orf, ohb, orb, p0, hgain, rgain, *, tm):
    nt = p0.shape[0]
    blk = pl.BlockSpec((tm, HALF), lambda i: (i, 0))
    vec = pl.BlockSpec((1, LANES), lambda i: (0, 0))
    y_shape = jax.ShapeDtypeStruct((nt, HALF), BF16)
    return pl.pallas_call(
        _rec_merge_kernel,
        grid=(nt // tm,),
        in_specs=[blk, blk, blk, blk,
                  pl.BlockSpec((tm, HALF), lambda i: (i, 4)), pl.BlockSpec((tm, HALF), lambda i: (i, 7)), vec, vec],
        out_specs=[blk, blk],
        out_shape=[y_shape, y_shape],
        compiler_params=_cparams("arbitrary"),
        name="rec_merge",
    )(ohf, ohb, orf, orb, p0, p0, hgain, rgain)


def _post_kernel(ya_ref, yb_ref, h_ref, mod_ref, wout_ref, gain_ref, wr_ref, br_ref,
                 hn_ref, f_ref, lpos_ref, gate_ref, ne_ref, cb_ref, cnt_ref, cnt_sc, *, d, tm, row_fn):
    i = pl.program_id(0)

    @pl.when(i == 0)
    def _():
        cnt_sc[...] = jnp.zeros_like(cnt_sc)

    r = row_fn(i)
    w_hi, w_lo = _split_bf16(wr_ref[...], 2)
    nt_dot = lambda a, b: lax.dot_general(a, b, _NT, preferred_element_type=F32)
    n_part = 2 if tm % (2 * LANES) == 0 else 1
    logit_parts = []
    for part in range(n_part):
        rs = slice(part * tm // n_part, (part + 1) * tm // n_part)
        y = jnp.concatenate([ya_ref[rs, :], yb_ref[rs, :]], axis=1)
        hn = h_ref[rs, :] + mod_ref[pl.ds(r, 1), 2 * d:3 * d] * _dot(y, wout_ref[...])
        hn_ref[rs, :] = hn
        f = _norm_mod(hn, gain_ref[...], mod_ref[pl.ds(r, 1), 3 * d:4 * d], mod_ref[pl.ds(r, 1), 4 * d:5 * d])
        f_ref[rs, :] = f
        f_hi, f_lo = _split_bf16(f, 2)
        logit_parts.append((nt_dot(w_hi, f_hi) + nt_dot(w_lo, f_hi) + nt_dot(w_hi, f_lo))[0:N_EXPERTS])
    logits = (logit_parts[0] if n_part == 1 else jnp.concatenate(logit_parts, axis=1)) + br_ref[...]
    row = lax.broadcasted_iota(I32, (N_EXPERTS, tm), 0)
    vals, idxs = [], []
    work = logits
    for _ in range(TOP_K):
        m = jnp.max(work, axis=0, keepdims=True)
        sel = jnp.min(jnp.where(work == m, row, N_EXPERTS), axis=0, keepdims=True)
        vals.append(m)
        idxs.append(sel)
        work = jnp.where(row == sel, -jnp.inf, work)
    exps = [jnp.exp(v - vals[0]) for v in vals]
    inv = 1.0 / sum(exps)
    onehot = sum((row == s).astype(F32) for s in idxs)
    n_e = jnp.broadcast_to(jnp.sum(onehot, axis=1, keepdims=True), (N_EXPERTS, LANES))
    rr = lax.broadcasted_iota(I32, (tm, tm), 0)
    cc = lax.broadcasted_iota(I32, (tm, tm), 1)
    within = _dot(onehot.astype(BF16), (rr < cc).astype(F32).astype(BF16))
    ea = lax.broadcasted_iota(I32, (N_EXPERTS, N_EXPERTS), 0)
    eb = lax.broadcasted_iota(I32, (N_EXPERTS, N_EXPERTS), 1)
    before = _dot((eb < ea).astype(F32).astype(BF16), n_e.astype(BF16))[:, 0:1]
    base = within + before
    sub = lax.broadcasted_iota(I32, (8, tm), 0)
    lpos_slab = jnp.zeros((8, tm), F32)
    gate_slab = jnp.zeros((8, tm), F32)
    for k in range(TOP_K):
        gate_slab = jnp.where(sub == k, exps[k] * inv, gate_slab)
        lp = jnp.sum(jnp.where(row == idxs[k], base, 0.0), axis=0, keepdims=True)
        lpos_slab = jnp.where(sub == k, lp, lpos_slab)
    lpos_ref[...] = lpos_slab.astype(I32)
    gate_ref[...] = gate_slab
    ne_ref[0] = n_e
    cb_ref[0] = cnt_sc[...]
    cnt_sc[...] = cnt_sc[...] + n_e
    cnt_ref[...] = cnt_sc[...]


def _post_call(ya, yb, h, mod, wout, gain, wr, br, *, tm, n_tiles, y_block, h_block, row_fn):
    d = h.shape[1]
    n = n_tiles * tm
    tile = lambda w: pl.BlockSpec((tm, w), lambda i: (i, 0))
    const = lambda a: pl.BlockSpec(a.shape, lambda i: (0,) * a.ndim)
    routing = pl.BlockSpec((8, tm), lambda i: (i, 0))
    counts = pl.BlockSpec((1, N_EXPERTS, LANES), lambda i: (i, 0, 0))
    return pl.pallas_call(
        functools.partial(_post_kernel, d=d, tm=tm, row_fn=row_fn),
        grid=(n_tiles,),
        in_specs=[pl.BlockSpec((tm, HALF), lambda i: (y_block(i), 0)),
                  pl.BlockSpec((tm, HALF), lambda i: (y_block(i), 0)),
                  pl.BlockSpec((tm, d), lambda i: (h_block(i), 0)),
                  const(mod), const(wout), const(gain), const(wr), const(br)],
        out_specs=[tile(d), tile(d), routing, routing, counts, counts,
                   pl.BlockSpec((N_EXPERTS, LANES), lambda i: (0, 0))],
        out_shape=[jax.ShapeDtypeStruct((n, d), F32), jax.ShapeDtypeStruct((n, d), F32),
                   jax.ShapeDtypeStruct((n_tiles * 8, tm), I32), jax.ShapeDtypeStruct((n_tiles * 8, tm), F32),
                   jax.ShapeDtypeStruct((n_tiles, N_EXPERTS, LANES), F32),
                   jax.ShapeDtypeStruct((n_tiles, N_EXPERTS, LANES), F32),
                   jax.ShapeDtypeStruct((N_EXPERTS, LANES), F32)],
        scratch_shapes=[pltpu.VMEM((N_EXPERTS, LANES), F32)],
        compiler_params=_cparams("arbitrary"),
        name="post_mixer",
    )(ya, yb, h, mod, wout, gain, wr, br)


SITE_TABLE_WIDTH = 1024
_SITE_HEAD = 16
_SITE_CLASS = 2 * N_EXPERTS


def _chunk_sizes(limit):
    return [1 << b for b in range(limit.bit_length() - 1, -1, -1)]


def _token_rows(start, count, pitch):
    return pl.ds(pl.multiple_of(start * pitch, pitch), count * pitch)


def _site_table(local_start, grouped_start, length, *, tm, pitch):
    sizes = jnp.asarray(_chunk_sizes(tm), I32)
    n_tiles, n_cls = length.shape[0], sizes.shape[0]
    assert n_cls <= _SITE_HEAD and _SITE_HEAD + n_cls * _SITE_CLASS <= SITE_TABLE_WIDTH
    n = length[..., None]
    done = jnp.bitwise_and(n, -2 * sizes)
    valid = jnp.bitwise_and(n, sizes) != 0
    loc = (local_start[..., None] + done) * pitch
    grp = (grouped_start[..., None] + done) * pitch
    rank = jnp.cumsum(valid.astype(I32), axis=1) - 1
    slot_of = jnp.logical_and(valid[..., None], rank[..., None] == jnp.arange(N_EXPERTS, dtype=I32))
    pick = lambda a: jnp.sum(jnp.where(slot_of, a[..., None], 0), axis=1)
    pairs = jnp.stack([pick(loc), pick(grp)], axis=-1).reshape(n_tiles, n_cls * _SITE_CLASS)
    counts = jnp.pad(jnp.sum(valid.astype(I32), axis=1), ((0, 0), (0, _SITE_HEAD - n_cls)))
    table = jnp.concatenate([counts, pairs], axis=1)
    return jnp.pad(table, ((0, 0), (0, SITE_TABLE_WIDTH - table.shape[1])))[:, None, :]


def _segment_copies(site_ref, local, slot, grouped, sem, *, tm, pitch, to_grouped):
    for k, size in enumerate(_chunk_sizes(tm)):
        base = _SITE_HEAD + k * _SITE_CLASS

        def body(j, c, base=base, size=size):
            loc0 = site_ref[0, 0, base + 2 * j]
            grp0 = site_ref[0, 0, base + 2 * j + 1]
            loc = local.at[slot, pl.ds(pl.multiple_of(loc0, pitch), size * pitch)]
            grp = grouped.at[pl.ds(pl.multiple_of(grp0, pitch), size * pitch)]
            (pltpu.make_async_copy(loc, grp, sem) if to_grouped else pltpu.make_async_copy(grp, loc, sem)).start()
            return c

        lax.fori_loop(0, site_ref[0, 0, k], body, 0)


def _pad_zero_copies(ztab_ref, zeros, grouped, sem, *, tm, pitch, wait):
    for e in range(N_EXPERTS):
        g0 = ztab_ref[e]
        n = ztab_ref[N_EXPERTS + e]
        for size in _chunk_sizes(tm // 2):
            done = jnp.bitwise_and(n, -2 * size)

            @pl.when(jnp.bitwise_and(n, size) != 0)
            def _():
                cp = pltpu.make_async_copy(zeros.at[pl.ds(0, size * pitch)],
                                           grouped.at[_token_rows(g0 + done, size, pitch)], sem)
                cp.wait() if wait else cp.start()


def _lanes_to_tiles(ref, x):
    rows, pitch = x.shape[0], x.shape[1] // LANES
    for c in range(pitch):
        ref[pl.ds(c, rows, stride=pitch), :] = x[:, c * LANES:(c + 1) * LANES]


def _tiles_to_lanes(ref, pitch):
    rows = ref.shape[0] // pitch
    return jnp.concatenate([ref[pl.ds(c, rows, stride=pitch), :] for c in range(pitch)], axis=1)


def _dispatch_kernel(ztab_ref, site_ref, f_ref, lpos_ref, xs_ref, zbuf, zeros, sems, zsem, *, tm, pitch):
    i = pl.program_id(0)
    n_tiles = pl.num_programs(0)
    slot = lax.rem(i, 2)
    rows = tm * TOP_K

    def tile_bytes(s):
        return pltpu.make_async_copy(zbuf.at[s], xs_ref.at[pl.ds(0, rows * pitch)], sems.at[s])

    @pl.when(i >= 2)
    def _():
        tile_bytes(slot).wait()

    pos = lax.broadcasted_iota(I32, (rows, tm), 0)
    lp = lpos_ref[...]
    sel = pos == lp[0:1, :]
    for k in range(1, TOP_K):
        sel = jnp.logical_or(sel, pos == lp[k:k + 1, :])
    z = _dot(jnp.where(sel, 1.0, 0.0).astype(BF16), f_ref[...].astype(BF16))
    _lanes_to_tiles(zbuf.at[slot], z)
    _segment_copies(site_ref, zbuf, slot, xs_ref, sems.at[slot], tm=tm, pitch=pitch, to_grouped=True)

    @pl.when(i == n_tiles - 1)
    def _():
        zeros[...] = jnp.zeros_like(zeros)
        _pad_zero_copies(ztab_ref, zeros, xs_ref, zsem, tm=tm, pitch=pitch, wait=False)
        _pad_zero_copies(ztab_ref, zeros, xs_ref, zsem, tm=tm, pitch=pitch, wait=True)
        half = tm // 2

        def tail(j, start):
            cp = pltpu.make_async_copy(zeros, xs_ref.at[_token_rows(j * half, half, pitch)], zsem)
            cp.start() if start else cp.wait()

        first, last = ztab_ref[2 * N_EXPERTS], xs_ref.shape[0] // (half * pitch)
        lax.fori_loop(first, last, lambda j, c: (tail(j, True), c)[1], 0)
        lax.fori_loop(first, last, lambda j, c: (tail(j, False), c)[1], 0)
        tile_bytes(slot).wait()
        tile_bytes(1 - slot).wait()


def _dispatch_call(ztab, sites, f, lpos, n_rows, *, tm):
    n, d = f.shape
    assert n // tm >= 2 and tm & (tm - 1) == 0
    pitch = d // LANES
    grid_spec = pltpu.PrefetchScalarGridSpec(
        num_scalar_prefetch=1,
        grid=(n // tm,),
        in_specs=[pl.BlockSpec((1, 1, SITE_TABLE_WIDTH), lambda i, z: (i, 0, 0), memory_space=pltpu.SMEM),
                  pl.BlockSpec((tm, d), lambda i, z: (i, 0)),
                  pl.BlockSpec((8, tm), lambda i, z: (i, 0))],
        out_specs=pl.BlockSpec(memory_space=pl.ANY),
        scratch_shapes=[pltpu.VMEM((2, tm * TOP_K * pitch, LANES), F32), pltpu.VMEM((tm // 2 * pitch, LANES), F32),
                        pltpu.SemaphoreType.DMA((2,)), pltpu.SemaphoreType.DMA(())])
    return pl.pallas_call(
        functools.partial(_dispatch_kernel, tm=tm, pitch=pitch),
        grid_spec=grid_spec,
        out_shape=jax.ShapeDtypeStruct((n_rows * pitch, LANES), F32),
        compiler_params=_cparams("arbitrary"),
        name="moe_dispatch",
    )(ztab, sites, f, lpos)


def _expert_kernel(b0_ref, nb_ref, xs_ref, wgu_ref, bgu_ref, wd_ref, bd_ref, y_ref,
                   xbuf, ybuf, wgu_bf, wd_bf, xsem, ysem, *, de, tm, pitch):
    e = pl.program_id(0)
    last_e = pl.num_programs(0) - 1
    nb = nb_ref[e]
    b0 = b0_ref[e]
    n_used = b0_ref[last_e] + nb_ref[last_e]
    rows = tm * pitch

    def block(g):
        return pl.ds(pl.multiple_of(g * rows, rows), rows)

    def x_copy(g, s):
        return pltpu.make_async_copy(xs_ref.at[block(g)], xbuf.at[s], xsem.at[s])

    def y_copy(g, s):
        return pltpu.make_async_copy(ybuf.at[s], y_ref.at[block(g)], ysem.at[s])

    @pl.when(jnp.logical_and(e == 0, n_used > 0))
    def _():
        x_copy(0, 0).start()

    @pl.when(nb > 0)
    def _():
        wgu_bf[...] = wgu_ref[0, 0].astype(BF16)
        wd_bf[...] = wd_ref[0, 0].astype(BF16)

    def body(j, c):
        g = b0 + j
        s = lax.rem(g, 2)
        x_copy(g, s).wait()

        @pl.when(g >= 2)
        def _():
            y_copy(g - 2, s).wait()

        @pl.when(g + 1 < n_used)
        def _():
            x_copy(g + 1, 1 - s).start()

        gu = _dot(_tiles_to_lanes(xbuf.at[s], pitch).astype(BF16), wgu_bf[...]) + bgu_ref[0, 0]
        gate = jnp.minimum(gu[:, 0:de], SWIGLU_LIMIT)
        up = jnp.clip(gu[:, de:2 * de], -SWIGLU_LIMIT, SWIGLU_LIMIT)
        hid = (up + 1.0) * gate * jax.nn.sigmoid(SWIGLU_ALPHA * gate)
        _lanes_to_tiles(ybuf.at[s], _dot(hid.astype(BF16), wd_bf[...]) + bd_ref[0, 0])
        y_copy(g, s).start()
        return c

    lax.fori_loop(0, nb, body, 0)

    @pl.when(e == last_e)
    def _():
        @pl.when(n_used >= 2)
        def _():
            y_copy(n_used - 2, lax.rem(n_used, 2)).wait()

        @pl.when(n_used >= 1)
        def _():
            y_copy(n_used - 1, lax.rem(n_used + 1, 2)).wait()

        ybuf[0] = jnp.zeros(ybuf.shape[1:], F32)

        def tail(g, start):
            cp = pltpu.make_async_copy(ybuf.at[0], y_ref.at[block(g)], ysem.at[0])
            cp.start() if start else cp.wait()

        n_blocks = y_ref.shape[0] // rows
        lax.fori_loop(n_used, n_blocks, lambda g, c: (tail(g, True), c)[1], 0)
        lax.fori_loop(n_used, n_blocks, lambda g, c: (tail(g, False), c)[1], 0)


def _expert_call(blk0, nblk, xs, w_gu, b_gu, w_down, b_down, *, tm, layer):
    nl, ne, d, de2 = w_gu.shape
    pitch = d // LANES
    de = de2 // 2
    grid_spec = pltpu.PrefetchScalarGridSpec(
        num_scalar_prefetch=2,
        grid=(ne,),
        in_specs=[pl.BlockSpec(memory_space=pl.ANY),
                  pl.BlockSpec((1, 1, d, de2), lambda e, b0, nb: (layer, e, 0, 0)),
                  pl.BlockSpec((1, 1, 1, de2), lambda e, b0, nb: (layer, e, 0, 0)),
                  pl.BlockSpec((1, 1, de, d), lambda e, b0, nb: (layer, e, 0, 0)),
                  pl.BlockSpec((1, 1, 1, d), lambda e, b0, nb: (layer, e, 0, 0))],
        out_specs=pl.BlockSpec(memory_space=pl.ANY),
        scratch_shapes=[pltpu.VMEM((2, tm * pitch, LANES), F32), pltpu.VMEM((2, tm * pitch, LANES), F32),
                        pltpu.VMEM((d, de2), BF16), pltpu.VMEM((de, d), BF16),
                        pltpu.SemaphoreType.DMA((2,)), pltpu.SemaphoreType.DMA((2,))])
    return pl.pallas_call(
        functools.partial(_expert_kernel, de=de, tm=tm, pitch=pitch),
        grid_spec=grid_spec,
        out_shape=jax.ShapeDtypeStruct(xs.shape, F32),
        compiler_params=_cparams("arbitrary"),
        name="moe_experts",
    )(blk0, nblk, xs, w_gu, b_gu.reshape(nl, ne, 1, de2), w_down, b_down.reshape(nl, ne, 1, d))


def _combine_kernel(site0_ref, site1_ref, lpos_ref, gate_ref, h_ref, mod_ref, y_ref, o_ref, ybuf, sems,
                    *, d, tm, row_fn):
    i = pl.program_id(0)
    n_tiles = pl.num_programs(0)
    slot = lax.rem(i, 2)
    rows = tm * TOP_K
    pitch = d // LANES

    def fetch(site_ref, s):
        _segment_copies(site_ref, ybuf, s, y_ref, sems.at[s], tm=tm, pitch=pitch, to_grouped=False)

    @pl.when(i == 0)
    def _():
        fetch(site0_ref, 0)

    @pl.when(i + 1 < n_tiles)
    def _():
        fetch(site1_ref, 1 - slot)

    pltpu.make_async_copy(y_ref.at[pl.ds(0, rows * pitch)], ybuf.at[slot], sems.at[slot]).wait()
    yb = _tiles_to_lanes(ybuf.at[slot], pitch).astype(BF16)
    pos = lax.broadcasted_iota(I32, (rows, tm), 0)
    lp = lpos_ref[...]
    gates = gate_ref[...]
    g = jnp.zeros((rows, tm), F32)
    for k in range(TOP_K):
        g = jnp.where(pos == lp[k:k + 1, :], gates[k:k + 1, :], g)
    g_hi, g_lo = _split_bf16(g, 2)
    acc = (lax.dot_general(g_hi, yb, _TN, preferred_element_type=F32)
           + lax.dot_general(g_lo, yb, _TN, preferred_element_type=F32))
    r = row_fn(i)
    o_ref[...] = h_ref[...] + mod_ref[pl.ds(r, 1), 5 * d:6 * d] * acc


def _combine_call(sites, lpos, gates, hn, mod, y_rows, *, tm, row_fn):
    n, d = hn.shape
    pitch = d // LANES
    n_tiles = n // tm
    site_block = (1, 1, SITE_TABLE_WIDTH)
    return pl.pallas_call(
        functools.partial(_combine_kernel, d=d, tm=tm, row_fn=row_fn),
        grid=(n_tiles,),
        in_specs=[pl.BlockSpec(site_block, lambda i: (0, 0, 0), memory_space=pltpu.SMEM),
                  pl.BlockSpec(site_block, lambda i: (jnp.minimum(i + 1, n_tiles - 1), 0, 0), memory_space=pltpu.SMEM),
                  pl.BlockSpec((8, tm), lambda i: (i, 0)),
                  pl.BlockSpec((8, tm), lambda i: (i, 0)),
                  pl.BlockSpec((tm, d), lambda i: (i, 0)),
                  pl.BlockSpec(mod.shape, lambda i: (0, 0)),
                  pl.BlockSpec(memory_space=pl.ANY)],
        out_specs=pl.BlockSpec((tm, d), lambda i: (i, 0)),
        out_shape=jax.ShapeDtypeStruct((n, d), F32),
        scratch_shapes=[pltpu.VMEM((2, tm * TOP_K * pitch, LANES), F32), pltpu.SemaphoreType.DMA((2,))],
        compiler_params=_cparams("arbitrary"),
        name="moe_combine",
    )(sites, sites, lpos, gates, hn, mod, y_rows)


def _moe(f, hn, mod, lpos, gates, ne_t, cb_t, counts, w_gu, b_gu, w_down, b_down, *, tm, row_fn, layer):
    n, _ = f.shape
    n_blocks = -(-n * TOP_K // tm) + N_EXPERTS
    cnt = counts[:, 0].astype(I32)
    padded = (cnt + tm - 1) // tm * tm
    pad_end = jnp.cumsum(padded)
    pad_start = pad_end - padded
    ne = ne_t[:, :, 0].astype(I32)
    cb = cb_t[:, :, 0].astype(I32)
    sites = _site_table(jnp.cumsum(ne, axis=1) - ne, pad_start[None] + cb, ne, tm=tm, pitch=f.shape[1] // LANES)
    n_valid = (pad_end[-1] // tm).astype(I32)
    ztab = jnp.concatenate([pad_start + cnt, padded - cnt, 2 * n_valid[None]])
    xs = _dispatch_call(ztab, sites, f, lpos, n_blocks * tm, tm=tm)
    y_rows = _expert_call(pad_start // tm, padded // tm, xs, w_gu, b_gu, w_down, b_down, tm=tm, layer=layer)
    return _combine_call(sites, lpos, gates, hn, mod, y_rows, tm=tm, row_fn=row_fn)


def _att_in_kernel(h_ref, mod_ref, gain_ref, w_ref, qg_gain, kg_gain, dq_gain, dk_gain, bd_ref,
                   cg_ref, sg_ref, cd_ref, sd_ref,
                   qg_ref, kg_ref, vg_ref, qd_ref, kd_ref, vd_ref, *, d, tpb, nct, nb):
    r = _unified_row(pl.program_id(0), tpb, nct, nb)
    a = _norm_mod(h_ref[...], gain_ref[...], mod_ref[pl.ds(r, 1), 0:d], mod_ref[pl.ds(r, 1), d:2 * d])
    p = _dot(a.astype(BF16), w_ref[...])
    cg, sg = cg_ref[...], sg_ref[...]

    def head_norm_rope(x, gain):
        y = x * lax.rsqrt(jnp.mean(x * x, axis=-1, keepdims=True) + NORM_EPS) * gain
        return _rope(y, cg, sg, GQA_HD).astype(BF16)

    def group_norm_rope(x, gain):
        ss = sum(_dot(piece, bd_ref[...]) for piece in _split_bf16(x * x, 2))
        y = x * lax.rsqrt(ss * (1.0 / DIFF_HD) + NORM_EPS) * gain
        return _rope(y, cd_ref[...], sd_ref[...], DIFF_HD).astype(BF16)

    c0 = 0
    for hh in range(GQA_HEADS):
        qg_ref[:, hh * GQA_HD:(hh + 1) * GQA_HD] = head_norm_rope(p[:, c0:c0 + GQA_HD], qg_gain[...])
        c0 += GQA_HD
    for hh in range(GQA_KV_HEADS):
        kg_ref[:, hh * GQA_HD:(hh + 1) * GQA_HD] = head_norm_rope(p[:, c0:c0 + GQA_HD], kg_gain[...])
        c0 += GQA_HD
    wv = GQA_KV_HEADS * GQA_HD
    vg_ref[...] = p[:, c0:c0 + wv].astype(BF16)
    c0 += wv
    qd_ref[...] = group_norm_rope(p[:, c0:c0 + HALF], dq_gain[...])
    c0 += HALF
    kd_ref[...] = group_norm_rope(p[:, c0:c0 + HALF], dk_gain[...])
    c0 += HALF
    vd_ref[...] = p[:, c0:c0 + HALF].astype(BF16)


def _att_in_call(h, mod, gain, w, qg_gain, kg_gain, dq_gain, dk_gain, bd, cg, sg, cd, sd, *, tm, tpb, nct, nb):
    nt, d = h.shape
    const = lambda a: pl.BlockSpec(a.shape, lambda i: (0,) * a.ndim)
    tab = lambda w_: pl.BlockSpec((tm, w_), lambda i: (lax.rem(i, tpb), 0))
    out = lambda w_: pl.BlockSpec((tm, w_), lambda i: (i, 0))
    shp = lambda w_: jax.ShapeDtypeStruct((nt, w_), BF16)
    wkv = GQA_KV_HEADS * GQA_HD
    return pl.pallas_call(
        functools.partial(_att_in_kernel, d=d, tpb=tpb, nct=nct, nb=nb),
        grid=(nt // tm,),
        in_specs=[pl.BlockSpec((tm, d), lambda i: (i, 0)), const(mod), const(gain), const(w),
                  const(qg_gain), const(kg_gain), const(dq_gain), const(dk_gain), const(bd),
                  tab(GQA_HD), tab(GQA_HD), tab(HALF), tab(HALF)],
        out_specs=[out(HALF), out(wkv), out(wkv), out(HALF), out(HALF), out(HALF)],
        out_shape=[shp(HALF), shp(wkv), shp(wkv), shp(HALF), shp(HALF), shp(HALF)],
        compiler_params=_cparams("arbitrary"),
        name="att_in_proj",
    )(h, mod, gain, w, qg_gain, kg_gain, dq_gain, dk_gain, bd, cg, sg, cd, sd)


def _softmax_parts(q, k):
    s = lax.dot_general(q, k, _NT, preferred_element_type=F32)
    e = jnp.exp2(s - jnp.max(s, axis=-1, keepdims=True))
    return e, 1.0 / jnp.sum(e, axis=-1, keepdims=True)


def _gqa_kernel(q_ref, k_ref, v_ref, o_ref, *, group):
    for g in range(group):
        sl = slice(g * GQA_HD, (g + 1) * GQA_HD)
        e, inv = _softmax_parts(q_ref[:, sl], k_ref[...])
        o_ref[:, sl] = (_dot(e.astype(BF16), v_ref[...]) * inv).astype(BF16)


def _gqa_call(qg, kg, vg, *, nb, tq, tot, nqt, lat0):
    group = GQA_HEADS // GQA_KV_HEADS
    upb = tot // tq
    return pl.pallas_call(
        functools.partial(_gqa_kernel, group=group),
        grid=(nb, GQA_KV_HEADS, nqt),
        in_specs=[pl.BlockSpec((tq, group * GQA_HD), lambda b, h, q: (b * upb + lat0 + q, h)),
                  pl.BlockSpec((tot, GQA_HD), lambda b, h, q: (b, h)),
                  pl.BlockSpec((tot, GQA_HD), lambda b, h, q: (b, h))],
        out_specs=pl.BlockSpec((tq, group * GQA_HD), lambda b, h, q: (b * nqt + q, h)),
        out_shape=jax.ShapeDtypeStruct((nb * nqt * tq, HALF), BF16),
        compiler_params=_cparams("arbitrary", "arbitrary", "arbitrary"),
        name="gqa_attention",
    )(qg, kg, vg)


def _diff_kernel(q_ref, k_ref, v_ref, lam_ref, gain_ref, o_ref, *, lam_init):
    lp = lam_ref[...]
    lam = (jnp.exp(jnp.sum(lp[0:1] * lp[1:2], axis=-1, keepdims=True))
           - jnp.exp(jnp.sum(lp[2:3] * lp[3:4], axis=-1, keepdims=True)) + lam_init)
    hd2 = 2 * DIFF_HD
    for j in range(q_ref.shape[1] // hd2):
        sl = slice(j * hd2, (j + 1) * hd2)
        q = q_ref[:, sl]
        k = k_ref[:, sl]
        lane = lax.broadcasted_iota(I32, q.shape, 1)
        zero = jnp.zeros_like(q)
        e1, inv1 = _softmax_parts(jnp.where(lane < DIFF_HD, q, zero), k)
        e2, inv2 = _softmax_parts(jnp.where(lane >= DIFF_HD, q, zero), k)
        v = v_ref[:, sl]
        o = _dot(e1.astype(BF16), v) * inv1 - _dot(e2.astype(BF16), v) * (lam * inv2)
        y = o * lax.rsqrt(jnp.mean(o * o, axis=-1, keepdims=True) + NORM_EPS) * gain_ref[...]
        o_ref[:, sl] = (y * (1.0 - lam_init)).astype(BF16)


def _diff_call(qd, kd, vd, lam_params, gain, *, nb, tq, tot, nqt, lat0, lam_init):
    upb = tot // tq
    hd2 = DIFF_HEADS * (2 * DIFF_HD)
    return pl.pallas_call(
        functools.partial(_diff_kernel, lam_init=lam_init),
        grid=(nb, HALF // hd2, nqt),
        in_specs=[pl.BlockSpec((tq, hd2), lambda b, h, q: (b * upb + lat0 + q, h)),
                  pl.BlockSpec((tot, hd2), lambda b, h, q: (b, h)),
                  pl.BlockSpec((tot, hd2), lambda b, h, q: (b, h)),
                  pl.BlockSpec(lam_params.shape, lambda b, h, q: (0, 0)),
                  pl.BlockSpec(gain.shape, lambda b, h, q: (0, 0))],
        out_specs=pl.BlockSpec((tq, hd2), lambda b, h, q: (b * nqt + q, h)),
        out_shape=jax.ShapeDtypeStruct((nb * nqt * tq, HALF), BF16),
        compiler_params=_cparams("arbitrary", "arbitrary", "arbitrary"),
        name="diff_attention",
    )(qd, kd, vd, lam_params, gain)


def _deinterleave(hd):
    return np.concatenate([np.arange(0, hd, 2), np.arange(1, hd, 2)])


def _head_perm(n_heads, hd):
    return np.concatenate([h * hd + _deinterleave(hd) for h in range(n_heads)])


def _rope_tables(n_lat, n_ctx, hd, reps):
    t = np.arange(n_lat)
    row, col = (t // GRID_W).astype(np.float64), (t % GRID_W).astype(np.float64)
    axis_dim = hd // 2
    inv_freq = ROPE_THETA ** (-np.arange(0, axis_dim, 2, dtype=np.float64) / axis_dim)
    ang = np.concatenate([row[:, None] * inv_freq, col[:, None] * inv_freq], axis=-1)
    cos = np.concatenate([np.ones((n_ctx, hd // 2)), np.cos(ang)], axis=0)
    sin = np.concatenate([np.zeros((n_ctx, hd // 2)), np.sin(ang)], axis=0)
    c = np.tile(np.concatenate([cos, cos], axis=1), (1, reps))
    s = np.tile(np.concatenate([-sin, sin], axis=1), (1, reps))
    return jnp.asarray(c, F32), jnp.asarray(s, F32)


def _retention_tables(n):
    gam = 1.0 - 2.0 ** (-5.0 - np.arange(RET_HEADS, dtype=np.float64))
    gams = (gam, gam[::-1])
    c = np.arange(n, dtype=np.float64)
    diff = c[:, None] - c[None, :]
    dmat = np.zeros((2, RET_HEADS, n, n))
    qs = np.zeros((2, n, RET_HEADS * RET_DK))
    ks = np.zeros((2, n, RET_HEADS * RET_DK))
    gpow = np.zeros((2, 8, RET_HEADS * RET_DK))
    for d in range(2):
        for h in range(RET_HEADS):
            g = gams[d][h]
            sl = slice(h * RET_DK, (h + 1) * RET_DK)
            if d == 0:
                dmat[d, h] = np.where(diff >= 0, g ** np.maximum(diff, 0), 0.0)
                qs[d, :, sl] = (g ** (c + 1))[:, None]
                ks[d, :, sl] = (g ** (n - 1 - c))[:, None]
            else:
                dmat[d, h] = np.where(diff <= 0, g ** np.maximum(-diff, 0), 0.0)
                qs[d, :, sl] = (g ** (n - c))[:, None]
                ks[d, :, sl] = (g ** c)[:, None]
            gpow[d, :, sl] = g ** n
    return tuple(jnp.asarray(a, F32) for a in (dmat, qs, ks, gpow))


def _plan(nb, n_lat, n_ctx):
    tm = 256 if (n_ctx % 256 == 0 and n_lat % 256 == 0) else 128
    assert n_ctx % tm == 0 and n_lat % tm == 0 and tm % CHUNK == 0 and n_lat % GRID_W == 0
    tq = tm
    return tm, tq


def kernel(x, c, ctx, c_ctx, norm_mix, norm_ffn, w_ada, b_ada, rec_w_in, rec_lb_logits, rec_w_out, rec_hg_gain,
           rec_ret_gain, att_w_in, att_w_out, att_q_gain, att_k_gain, diff_q_gain, diff_k_gain, diff_lambda,
           diff_gain, w_router, b_router, w_gu, b_gu, w_down, b_down):
    nb, n_lat, d = x.shape
    n_ctx = ctx.shape[1]
    assert w_ada.shape[0] == 2, "two layers: recurrent mixer then attention mixer"
    tm, tq = _plan(nb, n_lat, n_ctx)
    tot = n_ctx + n_lat
    tpb, nct = tot // tm, n_ctx // tm
    nt = nb * tot
    nql = n_lat // tm

    unified_row = functools.partial(_unified_row, tpb=tpb, nct=nct, nb=nb)
    latent_row = lambda i: lax.div(i, nql)
    latent_block = lambda i: lax.div(i, nql) * tpb + nct + lax.rem(i, nql)
    ident = lambda i: i

    n_cond = -(-(nb + 1) // 8) * 8
    cond = jnp.concatenate([c, c_ctx[None], jnp.zeros((n_cond - nb - 1, d), F32)], axis=0)
    mod = _ada_call(cond, w_ada, b_ada)

    wr = jnp.pad(jnp.swapaxes(w_router, 1, 2), ((0, 0), (0, E_PAD - N_EXPERTS), (0, 0)))
    br = b_router[:, :, None]

    h = jnp.concatenate([ctx, x], axis=1).reshape(nt, d)

    lb = jnp.cumsum(jax.nn.softmax(rec_lb_logits.astype(F32), axis=0), axis=0)[0]
    w_in = rec_w_in[0]
    c_rq = HG_HEADS * HG_DK * 3 + HG_HEADS * HG_DV * 2
    nrk = RET_HEADS * RET_DK
    perm = _head_perm(RET_HEADS, RET_DK)
    w_in = jnp.concatenate([w_in[:, :c_rq], w_in[:, c_rq + perm], w_in[:, c_rq + nrk + perm] * (RET_DK ** -0.5),
                            w_in[:, c_rq + 2 * nrk:]], axis=1).astype(BF16)
    cs_r, sn_r = _rope_tables(n_lat, n_ctx, RET_DK, 2 * RET_HEADS)
    dmat, qs, ks, g64 = _retention_tables(tm)
    p0 = _inproj_call(h, mod[0], norm_mix[0][None], w_in, tm=tm, tpb=tpb, nct=nct, nb=nb)
    ohf, orf, ohb, orb = _rec_call(p0, lb, cs_r, sn_r, dmat, qs, ks, g64, nb=nb, tc=tm, tpb=tpb, nct=nct)
    yh, yr = _rec_merge_call(ohf, orf, ohb, orb, p0, rec_hg_gain[0][None], rec_ret_gain[0][None], tm=tm)
    hn, f, *routing = _post_call(
        yh, yr, h, mod[0], rec_w_out[0].astype(BF16), norm_ffn[0][None], wr[0], br[0],
        tm=tm, n_tiles=nt // tm, y_block=ident, h_block=ident, row_fn=unified_row)
    h = _moe(f, hn, mod[0], *routing, w_gu, b_gu, w_down, b_down, tm=tm, row_fn=unified_row, layer=0)

    pg, pd = _head_perm(GQA_HEADS, GQA_HD), _head_perm(2 * DIFF_HEADS, DIFF_HD)
    w_in = att_w_in[0]
    o_gk = GQA_HEADS * GQA_HD
    o_gv = o_gk + GQA_KV_HEADS * GQA_HD
    o_dq = o_gv + GQA_KV_HEADS * GQA_HD
    o_dk = o_dq + HALF
    o_dv = o_dk + HALF
    w_in = jnp.concatenate([w_in[:, pg], w_in[:, o_gk + pg[:GQA_KV_HEADS * GQA_HD]], w_in[:, o_gv:o_dq],
                            w_in[:, o_dq + pd], w_in[:, o_dk + pd], w_in[:, o_dv:]], axis=1).astype(BF16)
    dg, dd = _deinterleave(GQA_HD), _deinterleave(DIFF_HD)
    cg, sg = _rope_tables(n_lat, n_ctx, GQA_HD, 1)
    cd, sd = _rope_tables(n_lat, n_ctx, DIFF_HD, 2 * DIFF_HEADS)
    bd = jnp.asarray(np.kron(np.eye(2 * DIFF_HEADS), np.ones((DIFF_HD, DIFF_HD))), BF16)
    log2e = math.log2(math.e)
    qg, kg, vg, qd, kd, vd = _att_in_call(
        h, mod[1], norm_mix[1][None], w_in,
        (att_q_gain[0][dg] * (GQA_HD ** -0.5 * log2e))[None], att_k_gain[0][dg][None],
        jnp.tile(diff_q_gain[0][dd] * (DIFF_HD ** -0.5 * log2e), 2 * DIFF_HEADS)[None],
        jnp.tile(diff_k_gain[0][dd], 2 * DIFF_HEADS)[None],
        bd, cg, sg, cd, sd, tm=tm, tpb=tpb, nct=nct, nb=nb)
    nqt = n_lat // tq
    yg = _gqa_call(qg, kg, vg, nb=nb, tq=tq, tot=tot, nqt=nqt, lat0=n_ctx // tq)
    lam_init = 0.8 - 0.6 * math.exp(-0.3 * 1)
    yd = _diff_call(qd, kd, vd, diff_lambda[0], diff_gain[0][None], nb=nb, tq=tq, tot=tot, nqt=nqt,
                    lat0=n_ctx // tq, lam_init=lam_init)
    hn, f, *routing = _post_call(
        yg, yd, h, mod[1], att_w_out[0].astype(BF16), norm_ffn[1][None], wr[1], br[1],
        tm=tm, n_tiles=nb * nql, y_block=ident, h_block=latent_block, row_fn=latent_row)
    out = _moe(f, hn, mod[1], *routing, w_gu, b_gu, w_down, b_down, tm=tm, row_fn=latent_row, layer=1)
    return out.reshape(nb, n_lat, d)
```

```python
import functools
import math

import numpy as np
import jax
import jax.numpy as jnp
from jax import lax
from jax.experimental import pallas as pl
from jax.experimental.pallas import tpu as pltpu

F32 = jnp.float32
BF16 = jnp.bfloat16
I32 = jnp.int32

GRID_W = 64
HG_HEADS, HG_DK, HG_DV = 4, 128, 128
RET_HEADS, RET_DK, RET_DV = 4, 64, 128
GQA_HEADS, GQA_KV_HEADS, GQA_HD = 4, 2, 128
DIFF_HEADS, DIFF_HD = 4, 64
N_EXPERTS, TOP_K = 32, 4
SWIGLU_LIMIT, SWIGLU_ALPHA = 7.0, 1.702
CHUNK = 64
ROPE_THETA = 10000.0
NORM_EPS = 1e-6

LANES = 128
E_PAD = LANES
V7X_VMEM_LIMIT = 56 * 1024 * 1024
HALF = 512

_NT = (((1,), (1,)), ((), ()))
_TN = (((0,), (0,)), ((), ()))


def _cparams(*sem):
    return pltpu.CompilerParams(dimension_semantics=sem, vmem_limit_bytes=V7X_VMEM_LIMIT)


def _silu(x):
    return x * jax.nn.sigmoid(x)


def _dot(a, b):
    return jnp.dot(a, b, preferred_element_type=F32)


def _split_bf16(x, terms):
    out = []
    for _ in range(terms):
        p = x.astype(BF16)
        out.append(p)
        x = x - p.astype(F32)
    return out


def _norm_mod(x, gain, shift, scale):
    y = x * lax.rsqrt(jnp.mean(x * x, axis=-1, keepdims=True) + NORM_EPS) * gain
    return y * (1.0 + scale) + shift


def _unified_row(i, tpb, nct, nb):
    return jnp.where(lax.rem(i, tpb) < nct, nb, lax.div(i, tpb))


def _ada_kernel(cond_ref, w_ref, b_ref, o_ref):
    s = _silu(cond_ref[...]).astype(BF16)
    o_ref[0] = _dot(s, w_ref[0].astype(BF16)) + b_ref[0]


def _ada_call(cond, w_ada, b_ada):
    n_layers, d, d6 = w_ada.shape
    r = cond.shape[0]
    tn = d6 // 4
    return pl.pallas_call(
        _ada_kernel,
        grid=(n_layers, d6 // tn),
        in_specs=[pl.BlockSpec((r, d), lambda l, n: (0, 0)),
                  pl.BlockSpec((1, d, tn), lambda l, n: (l, 0, n)),
                  pl.BlockSpec((1, 1, tn), lambda l, n: (l, 0, n))],
        out_specs=pl.BlockSpec((1, r, tn), lambda l, n: (l, 0, n)),
        out_shape=jax.ShapeDtypeStruct((n_layers, r, d6), F32),
        compiler_params=_cparams("arbitrary", "arbitrary"),
        name="ada_modulation",
    )(cond, w_ada, b_ada.reshape(n_layers, 1, d6))


REC_COL_Q = 0
REC_COL_RQK = 5


def _inproj_kernel(h_ref, mod_ref, gain_ref, w_ref, cs_ref, sn_ref, o_ref, *, d, tpb, nct, nb):
    r = _unified_row(pl.program_id(0), tpb, nct, nb)
    a = _norm_mod(h_ref[...], gain_ref[...], mod_ref[pl.ds(r, 1), 0:d], mod_ref[pl.ds(r, 1), d:2 * d])
    p = _dot(a.astype(BF16), w_ref[...])
    for blk in range(p.shape[1] // HALF):
        sl = slice(blk * HALF, (blk + 1) * HALF)
        x = p[:, sl]
        if blk == REC_COL_Q:
            x = _silu(x) * (HG_DK ** -0.5)
        elif blk == REC_COL_RQK:
            x = _rope(x, cs_ref[...], sn_ref[...], RET_DK)
        o_ref[:, sl] = x


def _inproj_call(h, mod, gain, w, cs, sn, *, tm, tpb, nct, nb):
    nt, d = h.shape
    n_out = w.shape[1]
    table = pl.BlockSpec((tm, HALF), lambda i: (lax.rem(i, tpb), 0))
    return pl.pallas_call(
        functools.partial(_inproj_kernel, d=d, tpb=tpb, nct=nct, nb=nb),
        grid=(nt // tm,),
        in_specs=[pl.BlockSpec((tm, d), lambda i: (i, 0)),
                  pl.BlockSpec(mod.shape, lambda i: (0, 0)),
                  pl.BlockSpec((1, d), lambda i: (0, 0)),
                  pl.BlockSpec((d, n_out), lambda i: (0, 0)), table, table],
        out_specs=pl.BlockSpec((tm, n_out), lambda i: (i, 0)),
        out_shape=jax.ShapeDtypeStruct((nt, n_out), F32),
        compiler_params=_cparams("arbitrary"),
        name="rec_in_proj",
    )(h, mod, gain, w, cs, sn)


def _swap_halves(x, group):
    if group == LANES:
        return pltpu.roll(x, LANES // 2, 1)
    lane = lax.broadcasted_iota(I32, x.shape, 1)
    half = group // 2
    return jnp.where(lax.rem(lane, group) < half, pltpu.roll(x, LANES - half, 1), pltpu.roll(x, half, 1))


def _rope(x, cos, sin, group):
    parts = [_swap_halves(x[:, s:s + LANES], group) for s in range(0, x.shape[1], LANES)]
    rot = parts[0] if len(parts) == 1 else jnp.concatenate(parts, axis=1)
    return x * cos + rot * sin


def _hgrn_chunk(d, r0, hq, hf, hi, lb_ref, ohg, st_hg, tri_bf, tri_mask):
    rows = pl.ds(r0, CHUNK)
    lbv = lb_ref[d:d + 1, :]
    f = lbv + (1.0 - lbv) * jax.nn.sigmoid(hf[rows, :])
    logf = jnp.log(f)
    kk = 1.0 - f
    b = sum(_dot(tri_bf, p) for p in _split_bf16(logf, 3))
    tot = b[CHUNK - 1:CHUNK, :] if d == 0 else b[0:1, :]
    q_in = (hq[rows, :] * jnp.exp(b)).astype(BF16)
    k_in = (kk * jnp.exp(-b)).astype(BF16)
    k_st = (kk * jnp.exp(tot - b)).astype(BF16)
    dec = jnp.exp(tot)
    v = hi[rows, :].astype(BF16)
    for h in range(HG_HEADS):
        sl = slice(h * HG_DK, (h + 1) * HG_DK)
        att = jnp.where(tri_mask, lax.dot_general(q_in[:, sl], k_in[:, sl], _NT, preferred_element_type=F32), 0.0)
        s_t = st_hg[d, h]
        o = _dot(att.astype(BF16), v[:, sl]) + lax.dot_general(
            q_in[:, sl], s_t.astype(BF16), _NT, preferred_element_type=F32)
        ohg[rows, sl] = o
        st_hg[d, h] = s_t * dec[:, sl] + lax.dot_general(v[:, sl], k_st[:, sl], _TN, preferred_element_type=F32)


def _retention_tile(d, rqk, rv, dmat_ref, qs_ref, ks_ref, gpow_ref, oret, st_ret):
    xr = rqk[...]
    nq = RET_HEADS * RET_DK
    q_r = xr[:, 0:nq]
    k_r = xr[:, nq:2 * nq]
    q_b = q_r.astype(BF16)
    k_b = k_r.astype(BF16)
    q_sc = (q_r * qs_ref[d]).astype(BF16)
    k_sc = (k_r * ks_ref[d]).astype(BF16)
    vv = rv[...].astype(BF16)
    for h in range(RET_HEADS):
        sk = slice(h * RET_DK, (h + 1) * RET_DK)
        sv = slice(h * RET_DV, (h + 1) * RET_DV)
        att = lax.dot_general(q_b[:, sk], k_b[:, sk], _NT, preferred_element_type=F32) * dmat_ref[d, h]
        s_t = st_ret[d, h]
        o = _dot(att.astype(BF16), vv[:, sv]) + lax.dot_general(
            q_sc[:, sk], s_t.astype(BF16), _NT, preferred_element_type=F32)
        oret[:, sv] = o
        st_ret[d, h] = s_t * gpow_ref[d, 0:1, sk] + lax.dot_general(
            vv[:, sv], k_sc[:, sk], _TN, preferred_element_type=F32)


def _rec_kernel(hq_f, hf_f, hi_f, rqk_f, rv_f,
                hq_b, hf_b, hi_b, rqk_b, rv_b,
                lb_ref, dmat_ref, qs_ref, ks_ref, g64_ref,
                ohg_f, oret_f, ohg_b, oret_b, st_hg, st_ret, *, tc):
    @pl.when(pl.program_id(1) == 0)
    def _():
        st_hg[...] = jnp.zeros_like(st_hg)
        st_ret[...] = jnp.zeros_like(st_ret)

    nch = tc // CHUNK
    row = lax.broadcasted_iota(I32, (CHUNK, CHUNK), 0)
    col = lax.broadcasted_iota(I32, (CHUNK, CHUNK), 1)
    mask_f = col <= row
    mask_b = col >= row
    tri_f = mask_f.astype(F32).astype(BF16)
    tri_b = mask_b.astype(F32).astype(BF16)

    ret_tables = (dmat_ref, qs_ref, ks_ref, g64_ref)
    _retention_tile(0, rqk_f, rv_f, *ret_tables, oret_f, st_ret)
    _retention_tile(1, rqk_b, rv_b, *ret_tables, oret_b, st_ret)
    for ci in range(nch):
        _hgrn_chunk(0, ci * CHUNK, hq_f, hf_f, hi_f, lb_ref, ohg_f, st_hg, tri_f, mask_f)
        _hgrn_chunk(1, (nch - 1 - ci) * CHUNK, hq_b, hf_b, hi_b, lb_ref, ohg_b, st_hg, tri_b, mask_b)


def _rec_call(p0, lb, dmat, qs, ks, g64, *, nb, tc, tpb, nct):
    nt = p0.shape[0]

    def fwd(b, j):
        return j

    def bwd(b, j):
        return jnp.where(j < nct, nct - 1 - j, tpb + nct - 1 - j)

    def pspec(col, pos):
        return pl.BlockSpec((tc, HALF), lambda b, j: (b * tpb + pos(b, j), col))

    def whole(a):
        return pl.BlockSpec(a.shape, lambda b, j: (0,) * a.ndim)

    in_specs = ([pspec(REC_COL_Q, fwd), pspec(1, fwd), pspec(3, fwd), pspec(REC_COL_RQK, fwd), pspec(6, fwd)]
                + [pspec(REC_COL_Q, bwd), pspec(2, bwd), pspec(3, bwd), pspec(REC_COL_RQK, bwd), pspec(6, bwd)]
                + [whole(a) for a in (lb, dmat, qs, ks, g64)])
    out_f = pl.BlockSpec((tc, HALF), lambda b, j: (b * tpb + fwd(b, j), 0))
    out_b = pl.BlockSpec((tc, HALF), lambda b, j: (b * tpb + bwd(b, j), 0))
    o_shape = jax.ShapeDtypeStruct((nt, HALF), F32)
    return pl.pallas_call(
        functools.partial(_rec_kernel, tc=tc),
        grid=(nb, tpb),
        in_specs=in_specs,
        out_specs=[out_f, out_f, out_b, out_b],
        out_shape=[o_shape] * 4,
        scratch_shapes=[pltpu.VMEM((2, HG_HEADS, HG_DV, HG_DK), F32),
                        pltpu.VMEM((2, RET_HEADS, RET_DV, RET_DK), F32)],
        compiler_params=_cparams("arbitrary", "arbitrary"),
        name="rec_scan",
    )(*([p0] * 10), lb, dmat, qs, ks, g64)


def _rec_mixer_rows(rs, ohf, ohb, orf, orb, hg_ref, rg_ref, hgain, rgain):
    oh = ohf[rs, :] + ohb[rs, :]
    orr = orf[rs, :] + orb[rs, :]
    hg = hg_ref[rs, :]
    rg = rg_ref[rs, :]
    y_h, y_r = [], []
    for h in range(HG_HEADS):
        sl = slice(h * LANES, (h + 1) * LANES)
        o = oh[:, sl]
        y = o * lax.rsqrt(jnp.mean(o * o, axis=-1, keepdims=True) + NORM_EPS) * hgain[...]
        y_h.append((y * _silu(hg[:, sl])).astype(BF16))
        o = orr[:, sl]
        oc = o - jnp.mean(o, axis=-1, keepdims=True)
        y = oc * lax.rsqrt(jnp.mean(oc * oc, axis=-1, keepdims=True) + NORM_EPS) * rgain[...]
        y_r.append((y * _silu(rg[:, sl])).astype(BF16))
    return jnp.concatenate(y_h + y_r, axis=1)


def _post_kernel(*refs, d, tm, row_fn, n_mixer_refs):
    mixer_refs = refs[:n_mixer_refs]
    (h_ref, mod_ref, wout_ref, gain_ref, wr_ref, br_ref,
     hn_ref, f_ref, lpos_ref, gate_ref, ne_ref, cb_ref, cnt_ref, cnt_sc) = refs[n_mixer_refs:]
    if n_mixer_refs == 2:
        mixer_rows = lambda rs: jnp.concatenate([mixer_refs[0][rs, :], mixer_refs[1][rs, :]], axis=1)
    else:
        mixer_rows = lambda rs: _rec_mixer_rows(rs, *mixer_refs)
    i = pl.program_id(0)

    @pl.when(i == 0)
    def _():
        cnt_sc[...] = jnp.zeros_like(cnt_sc)

    r = row_fn(i)
    w_hi, w_lo = _split_bf16(wr_ref[...], 2)
    nt_dot = lambda a, b: lax.dot_general(a, b, _NT, preferred_element_type=F32)
    n_part = 2 if tm % (2 * LANES) == 0 else 1
    logit_parts = []
    for part in range(n_part):
        rs = slice(part * tm // n_part, (part + 1) * tm // n_part)
        hn = h_ref[rs, :] + mod_ref[pl.ds(r, 1), 2 * d:3 * d] * _dot(mixer_rows(rs), wout_ref[...])
        hn_ref[rs, :] = hn
        f = _norm_mod(hn, gain_ref[...], mod_ref[pl.ds(r, 1), 3 * d:4 * d], mod_ref[pl.ds(r, 1), 4 * d:5 * d])
        f_ref[rs, :] = f
        f_hi, f_lo = _split_bf16(f, 2)
        logit_parts.append((nt_dot(w_hi, f_hi) + nt_dot(w_lo, f_hi) + nt_dot(w_hi, f_lo))[0:N_EXPERTS])
    logits = (logit_parts[0] if n_part == 1 else jnp.concatenate(logit_parts, axis=1)) + br_ref[...]
    row = lax.broadcasted_iota(I32, (N_EXPERTS, tm), 0)
    vals, idxs = [], []
    work = logits
    for _ in range(TOP_K):
        m = jnp.max(work, axis=0, keepdims=True)
        sel = jnp.min(jnp.where(work == m, row, N_EXPERTS), axis=0, keepdims=True)
        vals.append(m)
        idxs.append(sel)
        work = jnp.where(row == sel, -jnp.inf, work)
    exps = [jnp.exp(v - vals[0]) for v in vals]
    inv = 1.0 / sum(exps)
    onehot = sum((row == s).astype(F32) for s in idxs)
    n_e = jnp.broadcast_to(jnp.sum(onehot, axis=1, keepdims=True), (N_EXPERTS, LANES))
    rr = lax.broadcasted_iota(I32, (tm, tm), 0)
    cc = lax.broadcasted_iota(I32, (tm, tm), 1)
    within = _dot(onehot.astype(BF16), (rr < cc).astype(F32).astype(BF16))
    ea = lax.broadcasted_iota(I32, (N_EXPERTS, N_EXPERTS), 0)
    eb = lax.broadcasted_iota(I32, (N_EXPERTS, N_EXPERTS), 1)
    before = _dot((eb < ea).astype(F32).astype(BF16), n_e.astype(BF16))[:, 0:1]
    base = within + before
    sub = lax.broadcasted_iota(I32, (8, tm), 0)
    lpos_slab = jnp.zeros((8, tm), F32)
    gate_slab = jnp.zeros((8, tm), F32)
    for k in range(TOP_K):
        gate_slab = jnp.where(sub == k, exps[k] * inv, gate_slab)
        lp = jnp.sum(jnp.where(row == idxs[k], base, 0.0), axis=0, keepdims=True)
        lpos_slab = jnp.where(sub == k, lp, lpos_slab)
    lpos_ref[...] = lpos_slab.astype(I32)
    gate_ref[...] = gate_slab
    ne_ref[0] = n_e
    cb_ref[0] = cnt_sc[...]
    cnt_sc[...] = cnt_sc[...] + n_e
    cnt_ref[...] = cnt_sc[...]


def _post_call(mixer, h, mod, wout, gain, wr, br, *, tm, n_tiles, h_block, row_fn):
    d = h.shape[1]
    n = n_tiles * tm
    tile = lambda w: pl.BlockSpec((tm, w), lambda i: (i, 0))
    const = lambda a: pl.BlockSpec(a.shape, lambda i: (0,) * a.ndim)
    routing = pl.BlockSpec((8, tm), lambda i: (i, 0))
    counts = pl.BlockSpec((1, N_EXPERTS, LANES), lambda i: (i, 0, 0))
    mixer_specs = [const(a) if col is None else pl.BlockSpec((tm, HALF), lambda i, col=col: (i, col))
                   for a, col in mixer]
    return pl.pallas_call(
        functools.partial(_post_kernel, d=d, tm=tm, row_fn=row_fn, n_mixer_refs=len(mixer)),
        grid=(n_tiles,),
        in_specs=mixer_specs + [pl.BlockSpec((tm, d), lambda i: (h_block(i), 0)),
                                const(mod), const(wout), const(gain), const(wr), const(br)],
        out_specs=[tile(d), tile(d), routing, routing, counts, counts,
                   pl.BlockSpec((N_EXPERTS, LANES), lambda i: (0, 0))],
        out_shape=[jax.ShapeDtypeStruct((n, d), F32), jax.ShapeDtypeStruct((n, d), F32),
                   jax.ShapeDtypeStruct((n_tiles * 8, tm), I32), jax.ShapeDtypeStruct((n_tiles * 8, tm), F32),
                   jax.ShapeDtypeStruct((n_tiles, N_EXPERTS, LANES), F32),
                   jax.ShapeDtypeStruct((n_tiles, N_EXPERTS, LANES), F32),
                   jax.ShapeDtypeStruct((N_EXPERTS, LANES), F32)],
        scratch_shapes=[pltpu.VMEM((N_EXPERTS, LANES), F32)],
        compiler_params=_cparams("arbitrary"),
        name="post_mixer",
    )(*[a for a, _ in mixer], h, mod, wout, gain, wr, br)


SITE_TABLE_WIDTH = 1024
_SITE_HEAD = 16
_SITE_CLASS = 2 * N_EXPERTS


def _chunk_sizes(limit):
    return [1 << b for b in range(limit.bit_length() - 1, -1, -1)]


def _token_rows(start, count, pitch):
    return pl.ds(pl.multiple_of(start * pitch, pitch), count * pitch)


def _site_table(local_start, grouped_start, length, *, tm, pitch):
    sizes = jnp.asarray(_chunk_sizes(tm), I32)
    n_tiles, n_cls = length.shape[0], sizes.shape[0]
    assert n_cls <= _SITE_HEAD and _SITE_HEAD + n_cls * _SITE_CLASS <= SITE_TABLE_WIDTH
    n = length[..., None]
    done = jnp.bitwise_and(n, -2 * sizes)
    valid = jnp.bitwise_and(n, sizes) != 0
    loc = (local_start[..., None] + done) * pitch
    grp = (grouped_start[..., None] + done) * pitch
    rank = jnp.cumsum(valid.astype(I32), axis=1) - 1
    slot_of = jnp.logical_and(valid[..., None], rank[..., None] == jnp.arange(N_EXPERTS, dtype=I32))
    pick = lambda a: jnp.sum(jnp.where(slot_of, a[..., None], 0), axis=1)
    pairs = jnp.stack([pick(loc), pick(grp)], axis=-1).reshape(n_tiles, n_cls * _SITE_CLASS)
    counts = jnp.pad(jnp.sum(valid.astype(I32), axis=1), ((0, 0), (0, _SITE_HEAD - n_cls)))
    table = jnp.concatenate([counts, pairs], axis=1)
    return jnp.pad(table, ((0, 0), (0, SITE_TABLE_WIDTH - table.shape[1])))[:, None, :]


def _segment_copies(site_ref, local, slot, grouped, sem, *, tm, pitch, to_grouped):
    for k, size in enumerate(_chunk_sizes(tm)):
        base = _SITE_HEAD + k * _SITE_CLASS

        def body(j, c, base=base, size=size):
            loc0 = site_ref[0, 0, base + 2 * j]
            grp0 = site_ref[0, 0, base + 2 * j + 1]
            loc = local.at[slot, pl.ds(pl.multiple_of(loc0, pitch), size * pitch)]
            grp = grouped.at[pl.ds(pl.multiple_of(grp0, pitch), size * pitch)]
            (pltpu.make_async_copy(loc, grp, sem) if to_grouped else pltpu.make_async_copy(grp, loc, sem)).start()
            return c

        lax.fori_loop(0, site_ref[0, 0, k], body, 0)


def _pad_zero_copies(ztab_ref, zeros, grouped, sem, *, tm, pitch, wait):
    for e in range(N_EXPERTS):
        g0 = ztab_ref[e]
        n = ztab_ref[N_EXPERTS + e]
        for size in _chunk_sizes(tm // 2):
            done = jnp.bitwise_and(n, -2 * size)

            @pl.when(jnp.bitwise_and(n, size) != 0)
            def _():
                cp = pltpu.make_async_copy(zeros.at[pl.ds(0, size * pitch)],
                                           grouped.at[_token_rows(g0 + done, size, pitch)], sem)
                cp.wait() if wait else cp.start()


def _lanes_to_tiles(ref, x):
    rows, pitch = x.shape[0], x.shape[1] // LANES
    for c in range(pitch):
        ref[pl.ds(c, rows, stride=pitch), :] = x[:, c * LANES:(c + 1) * LANES]


def _tiles_to_lanes(ref, pitch):
    rows = ref.shape[0] // pitch
    return jnp.concatenate([ref[pl.ds(c, rows, stride=pitch), :] for c in range(pitch)], axis=1)


def _dispatch_kernel(ztab_ref, site_ref, f_ref, lpos_ref, xs_ref, zbuf, zeros, sems, zsem, *, tm, pitch):
    i = pl.program_id(0)
    n_tiles = pl.num_programs(0)
    slot = lax.rem(i, 2)
    rows = tm * TOP_K

    def tile_bytes(s):
        return pltpu.make_async_copy(zbuf.at[s], xs_ref.at[pl.ds(0, rows * pitch)], sems.at[s])

    @pl.when(i >= 2)
    def _():
        tile_bytes(slot).wait()

    pos = lax.broadcasted_iota(I32, (rows, tm), 0)
    lp = lpos_ref[...]
    sel = pos == lp[0:1, :]
    for k in range(1, TOP_K):
        sel = jnp.logical_or(sel, pos == lp[k:k + 1, :])
    z = _dot(jnp.where(sel, 1.0, 0.0).astype(BF16), f_ref[...].astype(BF16))
    _lanes_to_tiles(zbuf.at[slot], z)
    _segment_copies(site_ref, zbuf, slot, xs_ref, sems.at[slot], tm=tm, pitch=pitch, to_grouped=True)

    @pl.when(i == n_tiles - 1)
    def _():
        zeros[...] = jnp.zeros_like(zeros)
        _pad_zero_copies(ztab_ref, zeros, xs_ref, zsem, tm=tm, pitch=pitch, wait=False)
        _pad_zero_copies(ztab_ref, zeros, xs_ref, zsem, tm=tm, pitch=pitch, wait=True)
        half = tm // 2

        def tail(j, start):
            cp = pltpu.make_async_copy(zeros, xs_ref.at[_token_rows(j * half, half, pitch)], zsem)
            cp.start() if start else cp.wait()

        first, last = ztab_ref[2 * N_EXPERTS], xs_ref.shape[0] // (half * pitch)
        lax.fori_loop(first, last, lambda j, c: (tail(j, True), c)[1], 0)
        lax.fori_loop(first, last, lambda j, c: (tail(j, False), c)[1], 0)
        tile_bytes(slot).wait()
        tile_bytes(1 - slot).wait()


def _dispatch_call(ztab, sites, f, lpos, n_rows, *, tm):
    n, d = f.shape
    assert n // tm >= 2 and tm & (tm - 1) == 0
    pitch = d // LANES
    grid_spec = pltpu.PrefetchScalarGridSpec(
        num_scalar_prefetch=1,
        grid=(n // tm,),
        in_specs=[pl.BlockSpec((1, 1, SITE_TABLE_WIDTH), lambda i, z: (i, 0, 0), memory_space=pltpu.SMEM),
                  pl.BlockSpec((tm, d), lambda i, z: (i, 0)),
                  pl.BlockSpec((8, tm), lambda i, z: (i, 0))],
        out_specs=pl.BlockSpec(memory_space=pl.ANY),
        scratch_shapes=[pltpu.VMEM((2, tm * TOP_K * pitch, LANES), F32), pltpu.VMEM((tm // 2 * pitch, LANES), F32),
                        pltpu.SemaphoreType.DMA((2,)), pltpu.SemaphoreType.DMA(())])
    return pl.pallas_call(
        functools.partial(_dispatch_kernel, tm=tm, pitch=pitch),
        grid_spec=grid_spec,
        out_shape=jax.ShapeDtypeStruct((n_rows * pitch, LANES), F32),
        compiler_params=_cparams("arbitrary"),
        name="moe_dispatch",
    )(ztab, sites, f, lpos)


def _expert_kernel(b0_ref, nb_ref, xs_ref, wgu_ref, bgu_ref, wd_ref, bd_ref, y_ref,
                   xbuf, ybuf, wgu_bf, wd_bf, xsem, ysem, *, de, tm, pitch):
    e = pl.program_id(0)
    last_e = pl.num_programs(0) - 1
    nb = nb_ref[e]
    b0 = b0_ref[e]
    n_used = b0_ref[last_e] + nb_ref[last_e]
    rows = tm * pitch

    def block(g):
        return pl.ds(pl.multiple_of(g * rows, rows), rows)

    def x_copy(g, s):
        return pltpu.make_async_copy(xs_ref.at[block(g)], xbuf.at[s], xsem.at[s])

    def y_copy(g, s):
        return pltpu.make_async_copy(ybuf.at[s], y_ref.at[block(g)], ysem.at[s])

    @pl.when(jnp.logical_and(e == 0, n_used > 0))
    def _():
        x_copy(0, 0).start()

    @pl.when(nb > 0)
    def _():
        wgu_bf[...] = wgu_ref[0, 0].astype(BF16)
        wd_bf[...] = wd_ref[0, 0].astype(BF16)

    def body(j, c):
        g = b0 + j
        s = lax.rem(g, 2)
        x_copy(g, s).wait()

        @pl.when(g >= 2)
        def _():
            y_copy(g - 2, s).wait()

        @pl.when(g + 1 < n_used)
        def _():
            x_copy(g + 1, 1 - s).start()

        gu = _dot(_tiles_to_lanes(xbuf.at[s], pitch).astype(BF16), wgu_bf[...]) + bgu_ref[0, 0]
        gate = jnp.minimum(gu[:, 0:de], SWIGLU_LIMIT)
        up = jnp.clip(gu[:, de:2 * de], -SWIGLU_LIMIT, SWIGLU_LIMIT)
        hid = (up + 1.0) * gate * jax.nn.sigmoid(SWIGLU_ALPHA * gate)
        _lanes_to_tiles(ybuf.at[s], _dot(hid.astype(BF16), wd_bf[...]) + bd_ref[0, 0])
        y_copy(g, s).start()
        return c

    lax.fori_loop(0, nb, body, 0)

    @pl.when(e == last_e)
    def _():
        @pl.when(n_used >= 2)
        def _():
            y_copy(n_used - 2, lax.rem(n_used, 2)).wait()

        @pl.when(n_used >= 1)
        def _():
            y_copy(n_used - 1, lax.rem(n_used + 1, 2)).wait()

        ybuf[0] = jnp.zeros(ybuf.shape[1:], F32)

        def tail(g, start):
            cp = pltpu.make_async_copy(ybuf.at[0], y_ref.at[block(g)], ysem.at[0])
            cp.start() if start else cp.wait()

        n_blocks = y_ref.shape[0] // rows
        lax.fori_loop(n_used, n_blocks, lambda g, c: (tail(g, True), c)[1], 0)
        lax.fori_loop(n_used, n_blocks, lambda g, c: (tail(g, False), c)[1], 0)


def _expert_call(blk0, nblk, xs, w_gu, b_gu, w_down, b_down, *, tm, layer):
    nl, ne, d, de2 = w_gu.shape
    pitch = d // LANES
    de = de2 // 2
    grid_spec = pltpu.PrefetchScalarGridSpec(
        num_scalar_prefetch=2,
        grid=(ne,),
        in_specs=[pl.BlockSpec(memory_space=pl.ANY),
                  pl.BlockSpec((1, 1, d, de2), lambda e, b0, nb: (layer, e, 0, 0)),
                  pl.BlockSpec((1, 1, 1, de2), lambda e, b0, nb: (layer, e, 0, 0)),
                  pl.BlockSpec((1, 1, de, d), lambda e, b0, nb: (layer, e, 0, 0)),
                  pl.BlockSpec((1, 1, 1, d), lambda e, b0, nb: (layer, e, 0, 0))],
        out_specs=pl.BlockSpec(memory_space=pl.ANY),
        scratch_shapes=[pltpu.VMEM((2, tm * pitch, LANES), F32), pltpu.VMEM((2, tm * pitch, LANES), F32),
                        pltpu.VMEM((d, de2), BF16), pltpu.VMEM((de, d), BF16),
                        pltpu.SemaphoreType.DMA((2,)), pltpu.SemaphoreType.DMA((2,))])
    return pl.pallas_call(
        functools.partial(_expert_kernel, de=de, tm=tm, pitch=pitch),
        grid_spec=grid_spec,
        out_shape=jax.ShapeDtypeStruct(xs.shape, F32),
        compiler_params=_cparams("arbitrary"),
        name="moe_experts",
    )(blk0, nblk, xs, w_gu, b_gu.reshape(nl, ne, 1, de2), w_down, b_down.reshape(nl, ne, 1, d))


def _combine_kernel(site0_ref, site1_ref, lpos_ref, gate_ref, h_ref, mod_ref, y_ref, o_ref, ybuf, sems,
                    *, d, tm, row_fn):
    i = pl.program_id(0)
    n_tiles = pl.num_programs(0)
    slot = lax.rem(i, 2)
    rows = tm * TOP_K
    pitch = d // LANES

    def fetch(site_ref, s):
        _segment_copies(site_ref, ybuf, s, y_ref, sems.at[s], tm=tm, pitch=pitch, to_grouped=False)

    @pl.when(i == 0)
    def _():
        fetch(site0_ref, 0)

    @pl.when(i + 1 < n_tiles)
    def _():
        fetch(site1_ref, 1 - slot)

    pltpu.make_async_copy(y_ref.at[pl.ds(0, rows * pitch)], ybuf.at[slot], sems.at[slot]).wait()
    yb = _tiles_to_lanes(ybuf.at[slot], pitch).astype(BF16)
    pos = lax.broadcasted_iota(I32, (rows, tm), 0)
    lp = lpos_ref[...]
    gates = gate_ref[...]
    g = jnp.zeros((rows, tm), F32)
    for k in range(TOP_K):
        g = jnp.where(pos == lp[k:k + 1, :], gates[k:k + 1, :], g)
    g_hi, g_lo = _split_bf16(g, 2)
    acc = (lax.dot_general(g_hi, yb, _TN, preferred_element_type=F32)
           + lax.dot_general(g_lo, yb, _TN, preferred_element_type=F32))
    r = row_fn(i)
    o_ref[...] = h_ref[...] + mod_ref[pl.ds(r, 1), 5 * d:6 * d] * acc


def _combine_call(sites, lpos, gates, hn, mod, y_rows, *, tm, row_fn):
    n, d = hn.shape
    pitch = d // LANES
    n_tiles = n // tm
    site_block = (1, 1, SITE_TABLE_WIDTH)
    return pl.pallas_call(
        functools.partial(_combine_kernel, d=d, tm=tm, row_fn=row_fn),
        grid=(n_tiles,),
        in_specs=[pl.BlockSpec(site_block, lambda i: (0, 0, 0), memory_space=pltpu.SMEM),
                  pl.BlockSpec(site_block, lambda i: (jnp.minimum(i + 1, n_tiles - 1), 0, 0), memory_space=pltpu.SMEM),
                  pl.BlockSpec((8, tm), lambda i: (i, 0)),
                  pl.BlockSpec((8, tm), lambda i: (i, 0)),
                  pl.BlockSpec((tm, d), lambda i: (i, 0)),
                  pl.BlockSpec(mod.shape, lambda i: (0, 0)),
                  pl.BlockSpec(memory_space=pl.ANY)],
        out_specs=pl.BlockSpec((tm, d), lambda i: (i, 0)),
        out_shape=jax.ShapeDtypeStruct((n, d), F32),
        scratch_shapes=[pltpu.VMEM((2, tm * TOP_K * pitch, LANES), F32), pltpu.SemaphoreType.DMA((2,))],
        compiler_params=_cparams("arbitrary"),
        name="moe_combine",
    )(sites, sites, lpos, gates, hn, mod, y_rows)


def _moe(f, hn, mod, lpos, gates, ne_t, cb_t, counts, w_gu, b_gu, w_down, b_down, *, tm, row_fn, layer):
    n, _ = f.shape
    n_blocks = -(-n * TOP_K // tm) + N_EXPERTS
    cnt = counts[:, 0].astype(I32)
    padded = (cnt + tm - 1) // tm * tm
    pad_end = jnp.cumsum(padded)
    pad_start = pad_end - padded
    ne = ne_t[:, :, 0].astype(I32)
    cb = cb_t[:, :, 0].astype(I32)
    sites = _site_table(jnp.cumsum(ne, axis=1) - ne, pad_start[None] + cb, ne, tm=tm, pitch=f.shape[1] // LANES)
    n_valid = (pad_end[-1] // tm).astype(I32)
    ztab = jnp.concatenate([pad_start + cnt, padded - cnt, 2 * n_valid[None]])
    xs = _dispatch_call(ztab, sites, f, lpos, n_blocks * tm, tm=tm)
    y_rows = _expert_call(pad_start // tm, padded // tm, xs, w_gu, b_gu, w_down, b_down, tm=tm, layer=layer)
    return _combine_call(sites, lpos, gates, hn, mod, y_rows, tm=tm, row_fn=row_fn)


def _att_in_kernel(h_ref, mod_ref, gain_ref, w_ref, qg_gain, kg_gain, dq_gain, dk_gain, bd_ref,
                   cg_ref, sg_ref, cd_ref, sd_ref,
                   qg_ref, kg_ref, vg_ref, qd_ref, kd_ref, vd_ref, *, d, tpb, nct, nb):
    r = _unified_row(pl.program_id(0), tpb, nct, nb)
    a = _norm_mod(h_ref[...], gain_ref[...], mod_ref[pl.ds(r, 1), 0:d], mod_ref[pl.ds(r, 1), d:2 * d])
    p = _dot(a.astype(BF16), w_ref[...])
    cg, sg = cg_ref[...], sg_ref[...]

    def head_norm_rope(x, gain):
        y = x * lax.rsqrt(jnp.mean(x * x, axis=-1, keepdims=True) + NORM_EPS) * gain
        return _rope(y, cg, sg, GQA_HD).astype(BF16)

    def group_norm_rope(x, gain):
        ss = sum(_dot(piece, bd_ref[...]) for piece in _split_bf16(x * x, 2))
        y = x * lax.rsqrt(ss * (1.0 / DIFF_HD) + NORM_EPS) * gain
        return _rope(y, cd_ref[...], sd_ref[...], DIFF_HD).astype(BF16)

    c0 = 0
    for hh in range(GQA_HEADS):
        qg_ref[:, hh * GQA_HD:(hh + 1) * GQA_HD] = head_norm_rope(p[:, c0:c0 + GQA_HD], qg_gain[...])
        c0 += GQA_HD
    for hh in range(GQA_KV_HEADS):
        kg_ref[:, hh * GQA_HD:(hh + 1) * GQA_HD] = head_norm_rope(p[:, c0:c0 + GQA_HD], kg_gain[...])
        c0 += GQA_HD
    wv = GQA_KV_HEADS * GQA_HD
    vg_ref[...] = p[:, c0:c0 + wv].astype(BF16)
    c0 += wv
    qd_ref[...] = group_norm_rope(p[:, c0:c0 + HALF], dq_gain[...])
    c0 += HALF
    kd_ref[...] = group_norm_rope(p[:, c0:c0 + HALF], dk_gain[...])
    c0 += HALF
    vd_ref[...] = p[:, c0:c0 + HALF].astype(BF16)


def _att_in_call(h, mod, gain, w, qg_gain, kg_gain, dq_gain, dk_gain, bd, cg, sg, cd, sd, *, tm, tpb, nct, nb):
    nt, d = h.shape
    const = lambda a: pl.BlockSpec(a.shape, lambda i: (0,) * a.ndim)
    tab = lambda w_: pl.BlockSpec((tm, w_), lambda i: (lax.rem(i, tpb), 0))
    out = lambda w_: pl.BlockSpec((tm, w_), lambda i: (i, 0))
    shp = lambda w_: jax.ShapeDtypeStruct((nt, w_), BF16)
    wkv = GQA_KV_HEADS * GQA_HD
    return pl.pallas_call(
        functools.partial(_att_in_kernel, d=d, tpb=tpb, nct=nct, nb=nb),
        grid=(nt // tm,),
        in_specs=[pl.BlockSpec((tm, d), lambda i: (i, 0)), const(mod), const(gain), const(w),
                  const(qg_gain), const(kg_gain), const(dq_gain), const(dk_gain), const(bd),
                  tab(GQA_HD), tab(GQA_HD), tab(HALF), tab(HALF)],
        out_specs=[out(HALF), out(wkv), out(wkv), out(HALF), out(HALF), out(HALF)],
        out_shape=[shp(HALF), shp(wkv), shp(wkv), shp(HALF), shp(HALF), shp(HALF)],
        compiler_params=_cparams("arbitrary"),
        name="att_in_proj",
    )(h, mod, gain, w, qg_gain, kg_gain, dq_gain, dk_gain, bd, cg, sg, cd, sd)


def _softmax_parts(q, k):
    s = lax.dot_general(q, k, _NT, preferred_element_type=F32)
    e = jnp.exp2(s - jnp.max(s, axis=-1, keepdims=True))
    return e, 1.0 / jnp.sum(e, axis=-1, keepdims=True)


def _gqa_kernel(q_ref, k_ref, v_ref, o_ref, *, group):
    for h in range(GQA_HEADS):
        sl = slice(h * GQA_HD, (h + 1) * GQA_HD)
        kv = slice(h // group * GQA_HD, (h // group + 1) * GQA_HD)
        e, inv = _softmax_parts(q_ref[:, sl], k_ref[:, kv])
        o_ref[:, sl] = (_dot(e.astype(BF16), v_ref[:, kv]) * inv).astype(BF16)


def _gqa_call(qg, kg, vg, *, nb, tq, tot, nqt, lat0):
    group = GQA_HEADS // GQA_KV_HEADS
    upb = tot // tq
    return pl.pallas_call(
        functools.partial(_gqa_kernel, group=group),
        grid=(nb, nqt),
        in_specs=[pl.BlockSpec((tq, GQA_HEADS * GQA_HD), lambda b, q: (b * upb + lat0 + q, 0)),
                  pl.BlockSpec((tot, GQA_KV_HEADS * GQA_HD), lambda b, q: (b, 0)),
                  pl.BlockSpec((tot, GQA_KV_HEADS * GQA_HD), lambda b, q: (b, 0))],
        out_specs=pl.BlockSpec((tq, GQA_HEADS * GQA_HD), lambda b, q: (b * nqt + q, 0)),
        out_shape=jax.ShapeDtypeStruct((nb * nqt * tq, HALF), BF16),
        compiler_params=_cparams("arbitrary", "arbitrary"),
        name="gqa_attention",
    )(qg, kg, vg)


def _diff_kernel(q_ref, k_ref, v_ref, lam_ref, gain_ref, o_ref, *, lam_init):
    lp = lam_ref[...]
    lam = (jnp.exp(jnp.sum(lp[0:1] * lp[1:2], axis=-1, keepdims=True))
           - jnp.exp(jnp.sum(lp[2:3] * lp[3:4], axis=-1, keepdims=True)) + lam_init)
    hd2 = 2 * DIFF_HD
    for j in range(q_ref.shape[1] // hd2):
        sl = slice(j * hd2, (j + 1) * hd2)
        q = q_ref[:, sl]
        k = k_ref[:, sl]
        lane = lax.broadcasted_iota(I32, q.shape, 1)
        zero = jnp.zeros_like(q)
        e1, inv1 = _softmax_parts(jnp.where(lane < DIFF_HD, q, zero), k)
        e2, inv2 = _softmax_parts(jnp.where(lane >= DIFF_HD, q, zero), k)
        v = v_ref[:, sl]
        o = _dot(e1.astype(BF16), v) * inv1 - _dot(e2.astype(BF16), v) * (lam * inv2)
        y = o * lax.rsqrt(jnp.mean(o * o, axis=-1, keepdims=True) + NORM_EPS) * gain_ref[...]
        o_ref[:, sl] = (y * (1.0 - lam_init)).astype(BF16)


def _diff_call(qd, kd, vd, lam_params, gain, *, nb, tq, tot, nqt, lat0, lam_init):
    upb = tot // tq
    hd2 = DIFF_HEADS * (2 * DIFF_HD)
    return pl.pallas_call(
        functools.partial(_diff_kernel, lam_init=lam_init),
        grid=(nb, HALF // hd2, nqt),
        in_specs=[pl.BlockSpec((tq, hd2), lambda b, h, q: (b * upb + lat0 + q, h)),
                  pl.BlockSpec((tot, hd2), lambda b, h, q: (b, h)),
                  pl.BlockSpec((tot, hd2), lambda b, h, q: (b, h)),
                  pl.BlockSpec(lam_params.shape, lambda b, h, q: (0, 0)),
                  pl.BlockSpec(gain.shape, lambda b, h, q: (0, 0))],
        out_specs=pl.BlockSpec((tq, hd2), lambda b, h, q: (b * nqt + q, h)),
        out_shape=jax.ShapeDtypeStruct((nb * nqt * tq, HALF), BF16),
        compiler_params=_cparams("arbitrary", "arbitrary", "arbitrary"),
        name="diff_attention",
    )(qd, kd, vd, lam_params, gain)


def _deinterleave(hd):
    return np.concatenate([np.arange(0, hd, 2), np.arange(1, hd, 2)])


def _head_perm(n_heads, hd):
    return np.concatenate([h * hd + _deinterleave(hd) for h in range(n_heads)])


def _rope_tables(n_lat, n_ctx, hd, reps):
    t = np.arange(n_lat)
    row, col = (t // GRID_W).astype(np.float64), (t % GRID_W).astype(np.float64)
    axis_dim = hd // 2
    inv_freq = ROPE_THETA ** (-np.arange(0, axis_dim, 2, dtype=np.float64) / axis_dim)
    ang = np.concatenate([row[:, None] * inv_freq, col[:, None] * inv_freq], axis=-1)
    cos = np.concatenate([np.ones((n_ctx, hd // 2)), np.cos(ang)], axis=0)
    sin = np.concatenate([np.zeros((n_ctx, hd // 2)), np.sin(ang)], axis=0)
    c = np.tile(np.concatenate([cos, cos], axis=1), (1, reps))
    s = np.tile(np.concatenate([-sin, sin], axis=1), (1, reps))
    return jnp.asarray(c, F32), jnp.asarray(s, F32)


def _retention_tables(n):
    gam = 1.0 - 2.0 ** (-5.0 - np.arange(RET_HEADS, dtype=np.float64))
    gams = (gam, gam[::-1])
    c = np.arange(n, dtype=np.float64)
    diff = c[:, None] - c[None, :]
    dmat = np.zeros((2, RET_HEADS, n, n))
    qs = np.zeros((2, n, RET_HEADS * RET_DK))
    ks = np.zeros((2, n, RET_HEADS * RET_DK))
    gpow = np.zeros((2, 8, RET_HEADS * RET_DK))
    for d in range(2):
        for h in range(RET_HEADS):
            g = gams[d][h]
            sl = slice(h * RET_DK, (h + 1) * RET_DK)
            if d == 0:
                dmat[d, h] = np.where(diff >= 0, g ** np.maximum(diff, 0), 0.0)
                qs[d, :, sl] = (g ** (c + 1))[:, None]
                ks[d, :, sl] = (g ** (n - 1 - c))[:, None]
            else:
                dmat[d, h] = np.where(diff <= 0, g ** np.maximum(-diff, 0), 0.0)
                qs[d, :, sl] = (g ** (n - c))[:, None]
                ks[d, :, sl] = (g ** c)[:, None]
            gpow[d, :, sl] = g ** n
    return tuple(jnp.asarray(a, F32) for a in (dmat, qs, ks, gpow))


def _plan(nb, n_lat, n_ctx):
    tm = 256 if (n_ctx % 256 == 0 and n_lat % 256 == 0) else 128
    assert n_ctx % tm == 0 and n_lat % tm == 0 and tm % CHUNK == 0 and n_lat % GRID_W == 0
    tq = tm
    return tm, tq


def kernel(x, c, ctx, c_ctx, norm_mix, norm_ffn, w_ada, b_ada, rec_w_in, rec_lb_logits, rec_w_out, rec_hg_gain,
           rec_ret_gain, att_w_in, att_w_out, att_q_gain, att_k_gain, diff_q_gain, diff_k_gain, diff_lambda,
           diff_gain, w_router, b_router, w_gu, b_gu, w_down, b_down):
    nb, n_lat, d = x.shape
    n_ctx = ctx.shape[1]
    assert w_ada.shape[0] == 2, "two layers: recurrent mixer then attention mixer"
    tm, tq = _plan(nb, n_lat, n_ctx)
    tot = n_ctx + n_lat
    tpb, nct = tot // tm, n_ctx // tm
    nt = nb * tot
    nql = n_lat // tm

    unified_row = functools.partial(_unified_row, tpb=tpb, nct=nct, nb=nb)
    latent_row = lambda i: lax.div(i, nql)
    latent_block = lambda i: lax.div(i, nql) * tpb + nct + lax.rem(i, nql)
    ident = lambda i: i

    n_cond = -(-(nb + 1) // 8) * 8
    cond = jnp.concatenate([c, c_ctx[None], jnp.zeros((n_cond - nb - 1, d), F32)], axis=0)
    mod = _ada_call(cond, w_ada, b_ada)

    wr = jnp.pad(jnp.swapaxes(w_router, 1, 2), ((0, 0), (0, E_PAD - N_EXPERTS), (0, 0)))
    br = b_router[:, :, None]

    h = jnp.concatenate([ctx, x], axis=1).reshape(nt, d)

    lb = jnp.cumsum(jax.nn.softmax(rec_lb_logits.astype(F32), axis=0), axis=0)[0]
    w_in = rec_w_in[0]
    c_rq = HG_HEADS * HG_DK * 3 + HG_HEADS * HG_DV * 2
    nrk = RET_HEADS * RET_DK
    perm = _head_perm(RET_HEADS, RET_DK)
    w_in = jnp.concatenate([w_in[:, :c_rq], w_in[:, c_rq + perm], w_in[:, c_rq + nrk + perm] * (RET_DK ** -0.5),
                            w_in[:, c_rq + 2 * nrk:]], axis=1).astype(BF16)
    cs_r, sn_r = _rope_tables(n_lat, n_ctx, RET_DK, 2 * RET_HEADS)
    dmat, qs, ks, g64 = _retention_tables(tm)
    p0 = _inproj_call(h, mod[0], norm_mix[0][None], w_in, cs_r, sn_r, tm=tm, tpb=tpb, nct=nct, nb=nb)
    ohf, orf, ohb, orb = _rec_call(p0, lb, dmat, qs, ks, g64, nb=nb, tc=tm, tpb=tpb, nct=nct)
    mixer = [(ohf, 0), (ohb, 0), (orf, 0), (orb, 0), (p0, 4), (p0, 7),
             (rec_hg_gain[0][None], None), (rec_ret_gain[0][None], None)]
    hn, f, *routing = _post_call(
        mixer, h, mod[0], rec_w_out[0].astype(BF16), norm_ffn[0][None], wr[0], br[0],
        tm=tm, n_tiles=nt // tm, h_block=ident, row_fn=unified_row)
    h = _moe(f, hn, mod[0], *routing, w_gu, b_gu, w_down, b_down, tm=tm, row_fn=unified_row, layer=0)

    pg, pd = _head_perm(GQA_HEADS, GQA_HD), _head_perm(2 * DIFF_HEADS, DIFF_HD)
    w_in = att_w_in[0]
    o_gk = GQA_HEADS * GQA_HD
    o_gv = o_gk + GQA_KV_HEADS * GQA_HD
    o_dq = o_gv + GQA_KV_HEADS * GQA_HD
    o_dk = o_dq + HALF
    o_dv = o_dk + HALF
    w_in = jnp.concatenate([w_in[:, pg], w_in[:, o_gk + pg[:GQA_KV_HEADS * GQA_HD]], w_in[:, o_gv:o_dq],
                            w_in[:, o_dq + pd], w_in[:, o_dk + pd], w_in[:, o_dv:]], axis=1).astype(BF16)
    dg, dd = _deinterleave(GQA_HD), _deinterleave(DIFF_HD)
    cg, sg = _rope_tables(n_lat, n_ctx, GQA_HD, 1)
    cd, sd = _rope_tables(n_lat, n_ctx, DIFF_HD, 2 * DIFF_HEADS)
    bd = jnp.asarray(np.kron(np.eye(2 * DIFF_HEADS), np.ones((DIFF_HD, DIFF_HD))), BF16)
    log2e = math.log2(math.e)
    qg, kg, vg, qd, kd, vd = _att_in_call(
        h, mod[1], norm_mix[1][None], w_in,
        (att_q_gain[0][dg] * (GQA_HD ** -0.5 * log2e))[None], att_k_gain[0][dg][None],
        jnp.tile(diff_q_gain[0][dd] * (DIFF_HD ** -0.5 * log2e), 2 * DIFF_HEADS)[None],
        jnp.tile(diff_k_gain[0][dd], 2 * DIFF_HEADS)[None],
        bd, cg, sg, cd, sd, tm=tm, tpb=tpb, nct=nct, nb=nb)
    nqt = n_lat // tq
    yg = _gqa_call(qg, kg, vg, nb=nb, tq=tq, tot=tot, nqt=nqt, lat0=n_ctx // tq)
    lam_init = 0.8 - 0.6 * math.exp(-0.3 * 1)
    yd = _diff_call(qd, kd, vd, diff_lambda[0], diff_gain[0][None], nb=nb, tq=tq, tot=tot, nqt=nqt,
                    lat0=n_ctx // tq, lam_init=lam_init)
    hn, f, *routing = _post_call(
        [(yg, 0), (yd, 0)], h, mod[1], att_w_out[0].astype(BF16), norm_ffn[1][None], wr[1], br[1],
        tm=tm, n_tiles=nb * nql, h_block=latent_block, row_fn=latent_row)
    out = _moe(f, hn, mod[1], *routing, w_gu, b_gu, w_down, b_down, tm=tm, row_fn=latent_row, layer=1)
    return out.reshape(nb, n_lat, d)
```

```python
import functools
import math

import numpy as np
import jax
import jax.numpy as jnp
from jax import lax
from jax.experimental import pallas as pl
from jax.experimental.pallas import tpu as pltpu

F32 = jnp.float32
BF16 = jnp.bfloat16
I32 = jnp.int32

GRID_W = 64
HG_HEADS, HG_DK, HG_DV = 4, 128, 128
RET_HEADS, RET_DK, RET_DV = 4, 64, 128
GQA_HEADS, GQA_KV_HEADS, GQA_HD = 4, 2, 128
DIFF_HEADS, DIFF_HD = 4, 64
N_EXPERTS, TOP_K = 32, 4
SWIGLU_LIMIT, SWIGLU_ALPHA = 7.0, 1.702
CHUNK = 64
ROPE_THETA = 10000.0
NORM_EPS = 1e-6

LANES = 128
E_PAD = LANES
V7X_VMEM_LIMIT = 56 * 1024 * 1024
HALF = 512

_NT = (((1,), (1,)), ((), ()))
_TN = (((0,), (0,)), ((), ()))


def _cparams(*sem):
    return pltpu.CompilerParams(dimension_semantics=sem, vmem_limit_bytes=V7X_VMEM_LIMIT)


def _silu(x):
    return x * jax.nn.sigmoid(x)


def _dot(a, b):
    return jnp.dot(a, b, preferred_element_type=F32)


def _split_bf16(x, terms):
    out = []
    for _ in range(terms):
        p = x.astype(BF16)
        out.append(p)
        x = x - p.astype(F32)
    return out


def _norm_mod(x, gain, shift, scale):
    y = x * lax.rsqrt(jnp.mean(x * x, axis=-1, keepdims=True) + NORM_EPS) * gain
    return y * (1.0 + scale) + shift


def _unified_row(i, tpb, nct, nb):
    return jnp.where(lax.rem(i, tpb) < nct, nb, lax.div(i, tpb))


def _ada_kernel(cond_ref, w_ref, b_ref, o_ref):
    s = _silu(cond_ref[...]).astype(BF16)
    o_ref[0] = _dot(s, w_ref[0].astype(BF16)) + b_ref[0]


def _ada_call(cond, w_ada, b_ada):
    n_layers, d, d6 = w_ada.shape
    r = cond.shape[0]
    tn = d6 // 4
    return pl.pallas_call(
        _ada_kernel,
        grid=(n_layers, d6 // tn),
        in_specs=[pl.BlockSpec((r, d), lambda l, n: (0, 0)),
                  pl.BlockSpec((1, d, tn), lambda l, n: (l, 0, n)),
                  pl.BlockSpec((1, 1, tn), lambda l, n: (l, 0, n))],
        out_specs=pl.BlockSpec((1, r, tn), lambda l, n: (l, 0, n)),
        out_shape=jax.ShapeDtypeStruct((n_layers, r, d6), F32),
        compiler_params=_cparams("arbitrary", "arbitrary"),
        name="ada_modulation",
    )(cond, w_ada, b_ada.reshape(n_layers, 1, d6))


REC_COL_Q = 0
REC_COL_RQK = 5


def _inproj_kernel(x_ref, c_ref, mod_ref, gain_ref, w_ref, cs_ref, sn_ref, o_ref, *, d, tpb, nct, nb):
    i = pl.program_id(0)
    r = _unified_row(i, tpb, nct, nb)
    h = _stream_rows((x_ref, c_ref), lambda t: lax.rem(t, tpb) < nct, i, slice(None))
    a = _norm_mod(h, gain_ref[...], mod_ref[pl.ds(r, 1), 0:d], mod_ref[pl.ds(r, 1), d:2 * d])
    p = _dot(a.astype(BF16), w_ref[...])
    for blk in range(p.shape[1] // HALF):
        sl = slice(blk * HALF, (blk + 1) * HALF)
        x = p[:, sl]
        if blk == REC_COL_Q:
            x = _silu(x) * (HG_DK ** -0.5)
        elif blk == REC_COL_RQK:
            x = _rope(x, cs_ref[...], sn_ref[...], RET_DK)
        o_ref[:, sl] = x


def _inproj_call(stream, mod, gain, w, cs, sn, *, tm, tpb, nct, nb):
    d = stream[0][0].shape[1]
    nt = nb * tpb * tm
    n_out = w.shape[1]
    table = pl.BlockSpec((tm, HALF), lambda i: (lax.rem(i, tpb), 0))
    return pl.pallas_call(
        functools.partial(_inproj_kernel, d=d, tpb=tpb, nct=nct, nb=nb),
        grid=(nt // tm,),
        in_specs=[pl.BlockSpec((tm, d), lambda i, blk=blk: (blk(i), 0)) for _, blk in stream]
                 + [pl.BlockSpec(mod.shape, lambda i: (0, 0)),
                  pl.BlockSpec((1, d), lambda i: (0, 0)),
                  pl.BlockSpec((d, n_out), lambda i: (0, 0)), table, table],
        out_specs=pl.BlockSpec((tm, n_out), lambda i: (i, 0)),
        out_shape=jax.ShapeDtypeStruct((nt, n_out), F32),
        compiler_params=_cparams("arbitrary"),
        name="rec_in_proj",
    )(*[a for a, _ in stream], mod, gain, w, cs, sn)


def _swap_halves(x, group):
    if group == LANES:
        return pltpu.roll(x, LANES // 2, 1)
    lane = lax.broadcasted_iota(I32, x.shape, 1)
    half = group // 2
    return jnp.where(lax.rem(lane, group) < half, pltpu.roll(x, LANES - half, 1), pltpu.roll(x, half, 1))


def _rope(x, cos, sin, group):
    parts = [_swap_halves(x[:, s:s + LANES], group) for s in range(0, x.shape[1], LANES)]
    rot = parts[0] if len(parts) == 1 else jnp.concatenate(parts, axis=1)
    return x * cos + rot * sin


def _hgrn_chunk(d, r0, hq, hf, hi, lb_ref, ohg, st_hg, tri_bf, tri_mask):
    rows = pl.ds(r0, CHUNK)
    lbv = lb_ref[d:d + 1, :]
    f = lbv + (1.0 - lbv) * jax.nn.sigmoid(hf[rows, :])
    logf = jnp.log(f)
    kk = 1.0 - f
    b = sum(_dot(tri_bf, p) for p in _split_bf16(logf, 3))
    tot = b[CHUNK - 1:CHUNK, :] if d == 0 else b[0:1, :]
    q_in = (hq[rows, :] * jnp.exp(b)).astype(BF16)
    k_in = (kk * jnp.exp(-b)).astype(BF16)
    k_st = (kk * jnp.exp(tot - b)).astype(BF16)
    dec = jnp.exp(tot)
    v = hi[rows, :].astype(BF16)
    for h in range(HG_HEADS):
        sl = slice(h * HG_DK, (h + 1) * HG_DK)
        att = jnp.where(tri_mask, lax.dot_general(q_in[:, sl], k_in[:, sl], _NT, preferred_element_type=F32), 0.0)
        s_t = st_hg[d, h]
        o = _dot(att.astype(BF16), v[:, sl]) + lax.dot_general(
            q_in[:, sl], s_t.astype(BF16), _NT, preferred_element_type=F32)
        ohg[rows, sl] = o
        st_hg[d, h] = s_t * dec[:, sl] + lax.dot_general(v[:, sl], k_st[:, sl], _TN, preferred_element_type=F32)


def _retention_tile(d, rqk, rv, dmat_ref, qs_ref, ks_ref, gpow_ref, oret, st_ret):
    xr = rqk[...]
    nq = RET_HEADS * RET_DK
    q_r = xr[:, 0:nq]
    k_r = xr[:, nq:2 * nq]
    q_b = q_r.astype(BF16)
    k_b = k_r.astype(BF16)
    q_sc = (q_r * qs_ref[d]).astype(BF16)
    k_sc = (k_r * ks_ref[d]).astype(BF16)
    vv = rv[...].astype(BF16)
    for h in range(RET_HEADS):
        sk = slice(h * RET_DK, (h + 1) * RET_DK)
        sv = slice(h * RET_DV, (h + 1) * RET_DV)
        att = lax.dot_general(q_b[:, sk], k_b[:, sk], _NT, preferred_element_type=F32) * dmat_ref[d, h]
        s_t = st_ret[d, h]
        o = _dot(att.astype(BF16), vv[:, sv]) + lax.dot_general(
            q_sc[:, sk], s_t.astype(BF16), _NT, preferred_element_type=F32)
        oret[:, sv] = o
        st_ret[d, h] = s_t * gpow_ref[d, 0:1, sk] + lax.dot_general(
            vv[:, sv], k_sc[:, sk], _TN, preferred_element_type=F32)


def _rec_kernel(hq_f, hf_f, hi_f, rqk_f, rv_f,
                hq_b, hf_b, hi_b, rqk_b, rv_b,
                lb_ref, dmat_ref, qs_ref, ks_ref, g64_ref,
                ohg_f, oret_f, ohg_b, oret_b, st_hg, st_ret, *, tc):
    @pl.when(pl.program_id(1) == 0)
    def _():
        st_hg[...] = jnp.zeros_like(st_hg)
        st_ret[...] = jnp.zeros_like(st_ret)

    nch = tc // CHUNK
    row = lax.broadcasted_iota(I32, (CHUNK, CHUNK), 0)
    col = lax.broadcasted_iota(I32, (CHUNK, CHUNK), 1)
    mask_f = col <= row
    mask_b = col >= row
    tri_f = mask_f.astype(F32).astype(BF16)
    tri_b = mask_b.astype(F32).astype(BF16)

    ret_tables = (dmat_ref, qs_ref, ks_ref, g64_ref)
    _retention_tile(0, rqk_f, rv_f, *ret_tables, oret_f, st_ret)
    _retention_tile(1, rqk_b, rv_b, *ret_tables, oret_b, st_ret)
    for ci in range(nch):
        _hgrn_chunk(0, ci * CHUNK, hq_f, hf_f, hi_f, lb_ref, ohg_f, st_hg, tri_f, mask_f)
        _hgrn_chunk(1, (nch - 1 - ci) * CHUNK, hq_b, hf_b, hi_b, lb_ref, ohg_b, st_hg, tri_b, mask_b)


def _rec_call(p0, lb, dmat, qs, ks, g64, *, nb, tc, tpb, nct):
    nt = p0.shape[0]

    def fwd(b, j):
        return j

    def bwd(b, j):
        return jnp.where(j < nct, nct - 1 - j, tpb + nct - 1 - j)

    def pspec(col, pos):
        return pl.BlockSpec((tc, HALF), lambda b, j: (b * tpb + pos(b, j), col))

    def whole(a):
        return pl.BlockSpec(a.shape, lambda b, j: (0,) * a.ndim)

    in_specs = ([pspec(REC_COL_Q, fwd), pspec(1, fwd), pspec(3, fwd), pspec(REC_COL_RQK, fwd), pspec(6, fwd)]
                + [pspec(REC_COL_Q, bwd), pspec(2, bwd), pspec(3, bwd), pspec(REC_COL_RQK, bwd), pspec(6, bwd)]
                + [whole(a) for a in (lb, dmat, qs, ks, g64)])
    out_f = pl.BlockSpec((tc, HALF), lambda b, j: (b * tpb + fwd(b, j), 0))
    out_b = pl.BlockSpec((tc, HALF), lambda b, j: (b * tpb + bwd(b, j), 0))
    o_shape = jax.ShapeDtypeStruct((nt, HALF), F32)
    return pl.pallas_call(
        functools.partial(_rec_kernel, tc=tc),
        grid=(nb, tpb),
        in_specs=in_specs,
        out_specs=[out_f, out_f, out_b, out_b],
        out_shape=[o_shape] * 4,
        scratch_shapes=[pltpu.VMEM((2, HG_HEADS, HG_DV, HG_DK), F32),
                        pltpu.VMEM((2, RET_HEADS, RET_DV, RET_DK), F32)],
        compiler_params=_cparams("arbitrary", "arbitrary"),
        name="rec_scan",
    )(*([p0] * 10), lb, dmat, qs, ks, g64)


def _rec_mixer_rows(rs, ohf, ohb, orf, orb, hg_ref, rg_ref, hgain, rgain):
    oh = ohf[rs, :] + ohb[rs, :]
    orr = orf[rs, :] + orb[rs, :]
    hg = hg_ref[rs, :]
    rg = rg_ref[rs, :]
    y_h, y_r = [], []
    for h in range(HG_HEADS):
        sl = slice(h * LANES, (h + 1) * LANES)
        o = oh[:, sl]
        y = o * lax.rsqrt(jnp.mean(o * o, axis=-1, keepdims=True) + NORM_EPS) * hgain[...]
        y_h.append((y * _silu(hg[:, sl])).astype(BF16))
        o = orr[:, sl]
        oc = o - jnp.mean(o, axis=-1, keepdims=True)
        y = oc * lax.rsqrt(jnp.mean(oc * oc, axis=-1, keepdims=True) + NORM_EPS) * rgain[...]
        y_r.append((y * _silu(rg[:, sl])).astype(BF16))
    return jnp.concatenate(y_h + y_r, axis=1)


def _stream_rows(h_refs, is_context, i, rs):
    if len(h_refs) == 1:
        return h_refs[0][rs, :]
    return jnp.where(is_context(i), h_refs[1][rs, :], h_refs[0][rs, :])


def _post_kernel(*refs, d, tm, row_fn, n_mixer_refs, n_h_refs, is_context):
    mixer_refs = refs[:n_mixer_refs]
    h_refs = refs[n_mixer_refs:n_mixer_refs + n_h_refs]
    (mod_ref, wout_ref, gain_ref, wr_ref, br_ref,
     hn_ref, f_ref, lpos_ref, gate_ref, ne_ref, cb_ref, cnt_ref, cnt_sc) = refs[n_mixer_refs + n_h_refs:]
    if n_mixer_refs == 2:
        mixer_rows = lambda rs: jnp.concatenate([mixer_refs[0][rs, :], mixer_refs[1][rs, :]], axis=1)
    else:
        mixer_rows = lambda rs: _rec_mixer_rows(rs, *mixer_refs)
    i = pl.program_id(0)

    @pl.when(i == 0)
    def _():
        cnt_sc[...] = jnp.zeros_like(cnt_sc)

    r = row_fn(i)
    w_hi, w_lo = _split_bf16(wr_ref[...], 2)
    nt_dot = lambda a, b: lax.dot_general(a, b, _NT, preferred_element_type=F32)
    n_part = 2 if tm % (2 * LANES) == 0 else 1
    logit_parts = []
    for part in range(n_part):
        rs = slice(part * tm // n_part, (part + 1) * tm // n_part)
        hn = (_stream_rows(h_refs, is_context, i, rs)
              + mod_ref[pl.ds(r, 1), 2 * d:3 * d] * _dot(mixer_rows(rs), wout_ref[...]))
        hn_ref[rs, :] = hn
        f = _norm_mod(hn, gain_ref[...], mod_ref[pl.ds(r, 1), 3 * d:4 * d], mod_ref[pl.ds(r, 1), 4 * d:5 * d])
        f_hi, f_lo = _split_bf16(f, 2)
        f_ref[rs, :] = f_hi
        logit_parts.append((nt_dot(w_hi, f_hi) + nt_dot(w_lo, f_hi) + nt_dot(w_hi, f_lo))[0:N_EXPERTS])
    logits = (logit_parts[0] if n_part == 1 else jnp.concatenate(logit_parts, axis=1)) + br_ref[...]
    row = lax.broadcasted_iota(I32, (N_EXPERTS, tm), 0)
    vals, idxs = [], []
    work = logits
    for _ in range(TOP_K):
        m = jnp.max(work, axis=0, keepdims=True)
        sel = jnp.min(jnp.where(work == m, row, N_EXPERTS), axis=0, keepdims=True)
        vals.append(m)
        idxs.append(sel)
        work = jnp.where(row == sel, -jnp.inf, work)
    exps = [jnp.exp(v - vals[0]) for v in vals]
    inv = 1.0 / sum(exps)
    onehot = sum((row == s).astype(F32) for s in idxs)
    n_e = jnp.broadcast_to(jnp.sum(onehot, axis=1, keepdims=True), (N_EXPERTS, LANES))
    rr = lax.broadcasted_iota(I32, (tm, tm), 0)
    cc = lax.broadcasted_iota(I32, (tm, tm), 1)
    within = _dot(onehot.astype(BF16), (rr < cc).astype(F32).astype(BF16))
    ea = lax.broadcasted_iota(I32, (N_EXPERTS, N_EXPERTS), 0)
    eb = lax.broadcasted_iota(I32, (N_EXPERTS, N_EXPERTS), 1)
    before = _dot((eb < ea).astype(F32).astype(BF16), n_e.astype(BF16))[:, 0:1]
    base = within + before
    sub = lax.broadcasted_iota(I32, (8, tm), 0)
    lpos_slab = jnp.zeros((8, tm), F32)
    gate_slab = jnp.zeros((8, tm), F32)
    for k in range(TOP_K):
        gate_slab = jnp.where(sub == k, exps[k] * inv, gate_slab)
        lp = jnp.sum(jnp.where(row == idxs[k], base, 0.0), axis=0, keepdims=True)
        lpos_slab = jnp.where(sub == k, lp, lpos_slab)
    lpos_ref[...] = lpos_slab.astype(I32)
    gate_ref[...] = gate_slab
    ne_ref[0] = n_e
    cb_ref[0] = cnt_sc[...]
    cnt_sc[...] = cnt_sc[...] + n_e
    cnt_ref[...] = cnt_sc[...]


def _post_call(mixer, stream, mod, wout, gain, wr, br, *, tm, n_tiles, row_fn, is_context=None):
    d = stream[0][0].shape[1]
    n = n_tiles * tm
    tile = lambda w: pl.BlockSpec((tm, w), lambda i: (i, 0))
    const = lambda a: pl.BlockSpec(a.shape, lambda i: (0,) * a.ndim)
    routing = pl.BlockSpec((8, tm), lambda i: (i, 0))
    counts = pl.BlockSpec((1, N_EXPERTS, LANES), lambda i: (i, 0, 0))
    mixer_specs = [const(a) if col is None else pl.BlockSpec((tm, HALF), lambda i, col=col: (i, col))
                   for a, col in mixer]
    stream_specs = [pl.BlockSpec((tm, d), lambda i, blk=blk: (blk(i), 0)) for _, blk in stream]
    return pl.pallas_call(
        functools.partial(_post_kernel, d=d, tm=tm, row_fn=row_fn, n_mixer_refs=len(mixer),
                          n_h_refs=len(stream), is_context=is_context),
        grid=(n_tiles,),
        in_specs=mixer_specs + stream_specs + [const(mod), const(wout), const(gain), const(wr), const(br)],
        out_specs=[tile(d), tile(d), routing, routing, counts, counts,
                   pl.BlockSpec((N_EXPERTS, LANES), lambda i: (0, 0))],
        out_shape=[jax.ShapeDtypeStruct((n, d), F32), jax.ShapeDtypeStruct((n, d), BF16),
                   jax.ShapeDtypeStruct((n_tiles * 8, tm), I32), jax.ShapeDtypeStruct((n_tiles * 8, tm), F32),
                   jax.ShapeDtypeStruct((n_tiles, N_EXPERTS, LANES), F32),
                   jax.ShapeDtypeStruct((n_tiles, N_EXPERTS, LANES), F32),
                   jax.ShapeDtypeStruct((N_EXPERTS, LANES), F32)],
        scratch_shapes=[pltpu.VMEM((N_EXPERTS, LANES), F32)],
        compiler_params=_cparams("arbitrary"),
        name="post_mixer",
    )(*[a for a, _ in mixer], *[a for a, _ in stream], mod, wout, gain, wr, br)


SITE_TABLE_WIDTH = 1024
_SITE_HEAD = 16
_SITE_CLASS = 2 * N_EXPERTS


def _chunk_sizes(limit):
    return [1 << b for b in range(limit.bit_length() - 1, -1, -1)]


def _token_rows(start, count, pitch):
    return pl.ds(pl.multiple_of(start * pitch, pitch), count * pitch)


def _site_table(local_start, grouped_start, length, *, tm, pitch):
    sizes = jnp.asarray(_chunk_sizes(tm), I32)
    n_tiles, n_cls = length.shape[0], sizes.shape[0]
    assert n_cls <= _SITE_HEAD and _SITE_HEAD + n_cls * _SITE_CLASS <= SITE_TABLE_WIDTH
    n = length[..., None]
    done = jnp.bitwise_and(n, -2 * sizes)
    valid = jnp.bitwise_and(n, sizes) != 0
    loc = (local_start[..., None] + done) * pitch
    grp = (grouped_start[..., None] + done) * pitch
    rank = jnp.cumsum(valid.astype(I32), axis=1) - 1
    slot_of = jnp.logical_and(valid[..., None], rank[..., None] == jnp.arange(N_EXPERTS, dtype=I32))
    pick = lambda a: jnp.sum(jnp.where(slot_of, a[..., None], 0), axis=1)
    pairs = jnp.stack([pick(loc), pick(grp)], axis=-1).reshape(n_tiles, n_cls * _SITE_CLASS)
    counts = jnp.pad(jnp.sum(valid.astype(I32), axis=1), ((0, 0), (0, _SITE_HEAD - n_cls)))
    table = jnp.concatenate([counts, pairs], axis=1)
    return jnp.pad(table, ((0, 0), (0, SITE_TABLE_WIDTH - table.shape[1])))[:, None, :]


def _segment_copies(site_ref, local, slot, grouped, sem, *, tm, pitch, to_grouped):
    for k, size in enumerate(_chunk_sizes(tm)):
        base = _SITE_HEAD + k * _SITE_CLASS

        def body(j, c, base=base, size=size):
            loc0 = site_ref[0, 0, base + 2 * j]
            grp0 = site_ref[0, 0, base + 2 * j + 1]
            loc = local.at[slot, pl.ds(pl.multiple_of(loc0, pitch), size * pitch)]
            grp = grouped.at[pl.ds(pl.multiple_of(grp0, pitch), size * pitch)]
            (pltpu.make_async_copy(loc, grp, sem) if to_grouped else pltpu.make_async_copy(grp, loc, sem)).start()
            return c

        lax.fori_loop(0, site_ref[0, 0, k], body, 0)


def _pad_zero_copies(ztab_ref, zeros, grouped, sem, *, tm, pitch, wait):
    for e in range(N_EXPERTS):
        g0 = ztab_ref[e]
        n = ztab_ref[N_EXPERTS + e]
        for size in _chunk_sizes(tm // 2):
            done = jnp.bitwise_and(n, -2 * size)

            @pl.when(jnp.bitwise_and(n, size) != 0)
            def _():
                cp = pltpu.make_async_copy(zeros.at[pl.ds(0, size * pitch)],
                                           grouped.at[_token_rows(g0 + done, size, pitch)], sem)
                cp.wait() if wait else cp.start()


def _lanes_to_tiles(ref, x):
    rows, pitch = x.shape[0], x.shape[1] // LANES
    for c in range(pitch):
        ref[pl.ds(c, rows, stride=pitch), :] = x[:, c * LANES:(c + 1) * LANES]


def _tiles_to_lanes(ref, pitch):
    rows = ref.shape[0] // pitch
    return jnp.concatenate([ref[pl.ds(c, rows, stride=pitch), :] for c in range(pitch)], axis=1)


def _dispatch_kernel(ztab_ref, site_ref, f_ref, lpos_ref, xs_ref, zbuf, zeros, sems, zsem, *, tm, pitch):
    i = pl.program_id(0)
    n_tiles = pl.num_programs(0)
    slot = lax.rem(i, 2)
    rows = tm * TOP_K

    def tile_bytes(s):
        return pltpu.make_async_copy(zbuf.at[s], xs_ref.at[pl.ds(0, rows * pitch)], sems.at[s])

    @pl.when(i >= 2)
    def _():
        tile_bytes(slot).wait()

    pos = lax.broadcasted_iota(I32, (rows, tm), 0)
    lp = lpos_ref[...]
    sel = pos == lp[0:1, :]
    for k in range(1, TOP_K):
        sel = jnp.logical_or(sel, pos == lp[k:k + 1, :])
    z = _dot(jnp.where(sel, 1.0, 0.0).astype(BF16), f_ref[...])
    _lanes_to_tiles(zbuf.at[slot], z)
    _segment_copies(site_ref, zbuf, slot, xs_ref, sems.at[slot], tm=tm, pitch=pitch, to_grouped=True)

    @pl.when(i == n_tiles - 1)
    def _():
        zeros[...] = jnp.zeros_like(zeros)
        _pad_zero_copies(ztab_ref, zeros, xs_ref, zsem, tm=tm, pitch=pitch, wait=False)
        _pad_zero_copies(ztab_ref, zeros, xs_ref, zsem, tm=tm, pitch=pitch, wait=True)
        half = tm // 2

        def tail(j, start):
            cp = pltpu.make_async_copy(zeros, xs_ref.at[_token_rows(j * half, half, pitch)], zsem)
            cp.start() if start else cp.wait()

        first, last = ztab_ref[2 * N_EXPERTS], xs_ref.shape[0] // (half * pitch)
        lax.fori_loop(first, last, lambda j, c: (tail(j, True), c)[1], 0)
        lax.fori_loop(first, last, lambda j, c: (tail(j, False), c)[1], 0)
        tile_bytes(slot).wait()
        tile_bytes(1 - slot).wait()


def _dispatch_call(ztab, sites, f, lpos, n_rows, *, tm):
    n, d = f.shape
    assert n // tm >= 2 and tm & (tm - 1) == 0
    pitch = d // LANES
    grid_spec = pltpu.PrefetchScalarGridSpec(
        num_scalar_prefetch=1,
        grid=(n // tm,),
        in_specs=[pl.BlockSpec((1, 1, SITE_TABLE_WIDTH), lambda i, z: (i, 0, 0), memory_space=pltpu.SMEM),
                  pl.BlockSpec((tm, d), lambda i, z: (i, 0)),
                  pl.BlockSpec((8, tm), lambda i, z: (i, 0))],
        out_specs=pl.BlockSpec(memory_space=pl.ANY),
        scratch_shapes=[pltpu.VMEM((2, tm * TOP_K * pitch, LANES), F32), pltpu.VMEM((tm // 2 * pitch, LANES), F32),
                        pltpu.SemaphoreType.DMA((2,)), pltpu.SemaphoreType.DMA(())])
    return pl.pallas_call(
        functools.partial(_dispatch_kernel, tm=tm, pitch=pitch),
        grid_spec=grid_spec,
        out_shape=jax.ShapeDtypeStruct((n_rows * pitch, LANES), F32),
        compiler_params=_cparams("arbitrary"),
        name="moe_dispatch",
    )(ztab, sites, f, lpos)


def _expert_kernel(b0_ref, nb_ref, xs_ref, wgu_ref, bgu_ref, wd_ref, bd_ref, y_ref,
                   xbuf, ybuf, wgu_bf, wd_bf, xsem, ysem, *, de, tm, pitch):
    e = pl.program_id(0)
    last_e = pl.num_programs(0) - 1
    nb = nb_ref[e]
    b0 = b0_ref[e]
    n_used = b0_ref[last_e] + nb_ref[last_e]
    rows = tm * pitch

    def block(g):
        return pl.ds(pl.multiple_of(g * rows, rows), rows)

    def x_copy(g, s):
        return pltpu.make_async_copy(xs_ref.at[block(g)], xbuf.at[s], xsem.at[s])

    def y_copy(g, s):
        return pltpu.make_async_copy(ybuf.at[s], y_ref.at[block(g)], ysem.at[s])

    @pl.when(jnp.logical_and(e == 0, n_used > 0))
    def _():
        x_copy(0, 0).start()

    @pl.when(nb > 0)
    def _():
        wgu_bf[...] = wgu_ref[0, 0].astype(BF16)
        wd_bf[...] = wd_ref[0, 0].astype(BF16)

    def body(j, c):
        g = b0 + j
        s = lax.rem(g, 2)
        x_copy(g, s).wait()

        @pl.when(g >= 2)
        def _():
            y_copy(g - 2, s).wait()

        @pl.when(g + 1 < n_used)
        def _():
            x_copy(g + 1, 1 - s).start()

        gu = _dot(_tiles_to_lanes(xbuf.at[s], pitch).astype(BF16), wgu_bf[...]) + bgu_ref[0, 0]
        gate = jnp.minimum(gu[:, 0:de], SWIGLU_LIMIT)
        up = jnp.clip(gu[:, de:2 * de], -SWIGLU_LIMIT, SWIGLU_LIMIT)
        hid = (up + 1.0) * gate * jax.nn.sigmoid(SWIGLU_ALPHA * gate)
        _lanes_to_tiles(ybuf.at[s], _dot(hid.astype(BF16), wd_bf[...]) + bd_ref[0, 0])
        y_copy(g, s).start()
        return c

    lax.fori_loop(0, nb, body, 0)

    @pl.when(e == last_e)
    def _():
        @pl.when(n_used >= 2)
        def _():
            y_copy(n_used - 2, lax.rem(n_used, 2)).wait()

        @pl.when(n_used >= 1)
        def _():
            y_copy(n_used - 1, lax.rem(n_used + 1, 2)).wait()

        ybuf[0] = jnp.zeros(ybuf.shape[1:], F32)

        def tail(g, start):
            cp = pltpu.make_async_copy(ybuf.at[0], y_ref.at[block(g)], ysem.at[0])
            cp.start() if start else cp.wait()

        n_blocks = y_ref.shape[0] // rows
        lax.fori_loop(n_used, n_blocks, lambda g, c: (tail(g, True), c)[1], 0)
        lax.fori_loop(n_used, n_blocks, lambda g, c: (tail(g, False), c)[1], 0)


def _expert_call(blk0, nblk, xs, w_gu, b_gu, w_down, b_down, *, tm, layer):
    nl, ne, d, de2 = w_gu.shape
    pitch = d // LANES
    de = de2 // 2
    grid_spec = pltpu.PrefetchScalarGridSpec(
        num_scalar_prefetch=2,
        grid=(ne,),
        in_specs=[pl.BlockSpec(memory_space=pl.ANY),
                  pl.BlockSpec((1, 1, d, de2), lambda e, b0, nb: (layer, e, 0, 0)),
                  pl.BlockSpec((1, 1, 1, de2), lambda e, b0, nb: (layer, e, 0, 0)),
                  pl.BlockSpec((1, 1, de, d), lambda e, b0, nb: (layer, e, 0, 0)),
                  pl.BlockSpec((1, 1, 1, d), lambda e, b0, nb: (layer, e, 0, 0))],
        out_specs=pl.BlockSpec(memory_space=pl.ANY),
        scratch_shapes=[pltpu.VMEM((2, tm * pitch, LANES), F32), pltpu.VMEM((2, tm * pitch, LANES), F32),
                        pltpu.VMEM((d, de2), BF16), pltpu.VMEM((de, d), BF16),
                        pltpu.SemaphoreType.DMA((2,)), pltpu.SemaphoreType.DMA((2,))])
    return pl.pallas_call(
        functools.partial(_expert_kernel, de=de, tm=tm, pitch=pitch),
        grid_spec=grid_spec,
        out_shape=jax.ShapeDtypeStruct(xs.shape, F32),
        compiler_params=_cparams("arbitrary"),
        name="moe_experts",
    )(blk0, nblk, xs, w_gu, b_gu.reshape(nl, ne, 1, de2), w_down, b_down.reshape(nl, ne, 1, d))


def _combine_kernel(site0_ref, site1_ref, lpos_ref, gate_ref, h_ref, mod_ref, y_ref, o_ref, ybuf, sems,
                    *, d, tm, row_fn):
    i = pl.program_id(0)
    n_tiles = pl.num_programs(0)
    slot = lax.rem(i, 2)
    rows = tm * TOP_K
    pitch = d // LANES

    def fetch(site_ref, s):
        _segment_copies(site_ref, ybuf, s, y_ref, sems.at[s], tm=tm, pitch=pitch, to_grouped=False)

    @pl.when(i == 0)
    def _():
        fetch(site0_ref, 0)

    @pl.when(i + 1 < n_tiles)
    def _():
        fetch(site1_ref, 1 - slot)

    pltpu.make_async_copy(y_ref.at[pl.ds(0, rows * pitch)], ybuf.at[slot], sems.at[slot]).wait()
    yb = _tiles_to_lanes(ybuf.at[slot], pitch).astype(BF16)
    pos = lax.broadcasted_iota(I32, (rows, tm), 0)
    lp = lpos_ref[...]
    gates = gate_ref[...]
    g = jnp.zeros((rows, tm), F32)
    for k in range(TOP_K):
        g = jnp.where(pos == lp[k:k + 1, :], gates[k:k + 1, :], g)
    g_hi, g_lo = _split_bf16(g, 2)
    acc = (lax.dot_general(g_hi, yb, _TN, preferred_element_type=F32)
           + lax.dot_general(g_lo, yb, _TN, preferred_element_type=F32))
    r = row_fn(i)
    o_ref[...] = h_ref[...] + mod_ref[pl.ds(r, 1), 5 * d:6 * d] * acc


def _combine_call(sites, lpos, gates, hn, mod, y_rows, *, tm, row_fn):
    n, d = hn.shape
    pitch = d // LANES
    n_tiles = n // tm
    site_block = (1, 1, SITE_TABLE_WIDTH)
    return pl.pallas_call(
        functools.partial(_combine_kernel, d=d, tm=tm, row_fn=row_fn),
        grid=(n_tiles,),
        in_specs=[pl.BlockSpec(site_block, lambda i: (0, 0, 0), memory_space=pltpu.SMEM),
                  pl.BlockSpec(site_block, lambda i: (jnp.minimum(i + 1, n_tiles - 1), 0, 0), memory_space=pltpu.SMEM),
                  pl.BlockSpec((8, tm), lambda i: (i, 0)),
                  pl.BlockSpec((8, tm), lambda i: (i, 0)),
                  pl.BlockSpec((tm, d), lambda i: (i, 0)),
                  pl.BlockSpec(mod.shape, lambda i: (0, 0)),
                  pl.BlockSpec(memory_space=pl.ANY)],
        out_specs=pl.BlockSpec((tm, d), lambda i: (i, 0)),
        out_shape=jax.ShapeDtypeStruct((n, d), F32),
        scratch_shapes=[pltpu.VMEM((2, tm * TOP_K * pitch, LANES), F32), pltpu.SemaphoreType.DMA((2,))],
        compiler_params=_cparams("arbitrary"),
        name="moe_combine",
    )(sites, sites, lpos, gates, hn, mod, y_rows)


def _moe(f, hn, mod, lpos, gates, ne_t, cb_t, counts, w_gu, b_gu, w_down, b_down, *, tm, row_fn, layer):
    n, _ = f.shape
    n_blocks = -(-n * TOP_K // tm) + N_EXPERTS
    cnt = counts[:, 0].astype(I32)
    padded = (cnt + tm - 1) // tm * tm
    pad_end = jnp.cumsum(padded)
    pad_start = pad_end - padded
    ne = ne_t[:, :, 0].astype(I32)
    cb = cb_t[:, :, 0].astype(I32)
    sites = _site_table(jnp.cumsum(ne, axis=1) - ne, pad_start[None] + cb, ne, tm=tm, pitch=f.shape[1] // LANES)
    n_valid = (pad_end[-1] // tm).astype(I32)
    ztab = jnp.concatenate([pad_start + cnt, padded - cnt, 2 * n_valid[None]])
    xs = _dispatch_call(ztab, sites, f, lpos, n_blocks * tm, tm=tm)
    y_rows = _expert_call(pad_start // tm, padded // tm, xs, w_gu, b_gu, w_down, b_down, tm=tm, layer=layer)
    return _combine_call(sites, lpos, gates, hn, mod, y_rows, tm=tm, row_fn=row_fn)


def _att_in_kernel(h_ref, mod_ref, gain_ref, w_ref, qg_gain, kg_gain, dq_gain, dk_gain, bd_ref,
                   cg_ref, sg_ref, cd_ref, sd_ref,
                   qg_ref, kg_ref, vg_ref, qd_ref, kd_ref, vd_ref, *, d, tpb, nct, nb):
    r = _unified_row(pl.program_id(0), tpb, nct, nb)
    a = _norm_mod(h_ref[...], gain_ref[...], mod_ref[pl.ds(r, 1), 0:d], mod_ref[pl.ds(r, 1), d:2 * d])
    p = _dot(a.astype(BF16), w_ref[...])
    cg, sg = cg_ref[...], sg_ref[...]

    def head_norm_rope(x, gain):
        y = x * lax.rsqrt(jnp.mean(x * x, axis=-1, keepdims=True) + NORM_EPS) * gain
        return _rope(y, cg, sg, GQA_HD).astype(BF16)

    def group_norm_rope(x, gain):
        ss = sum(_dot(piece, bd_ref[...]) for piece in _split_bf16(x * x, 2))
        y = x * lax.rsqrt(ss * (1.0 / DIFF_HD) + NORM_EPS) * gain
        return _rope(y, cd_ref[...], sd_ref[...], DIFF_HD).astype(BF16)

    c0 = 0
    for hh in range(GQA_HEADS):
        qg_ref[:, hh * GQA_HD:(hh + 1) * GQA_HD] = head_norm_rope(p[:, c0:c0 + GQA_HD], qg_gain[...])
        c0 += GQA_HD
    for hh in range(GQA_KV_HEADS):
        kg_ref[:, hh * GQA_HD:(hh + 1) * GQA_HD] = head_norm_rope(p[:, c0:c0 + GQA_HD], kg_gain[...])
        c0 += GQA_HD
    wv = GQA_KV_HEADS * GQA_HD
    vg_ref[...] = p[:, c0:c0 + wv].astype(BF16)
    c0 += wv
    qd_ref[...] = group_norm_rope(p[:, c0:c0 + HALF], dq_gain[...])
    c0 += HALF
    kd_ref[...] = group_norm_rope(p[:, c0:c0 + HALF], dk_gain[...])
    c0 += HALF
    vd_ref[...] = p[:, c0:c0 + HALF].astype(BF16)


def _att_in_call(h, mod, gain, w, qg_gain, kg_gain, dq_gain, dk_gain, bd, cg, sg, cd, sd, *, tm, tpb, nct, nb):
    nt, d = h.shape
    const = lambda a: pl.BlockSpec(a.shape, lambda i: (0,) * a.ndim)
    tab = lambda w_: pl.BlockSpec((tm, w_), lambda i: (lax.rem(i, tpb), 0))
    out = lambda w_: pl.BlockSpec((tm, w_), lambda i: (i, 0))
    shp = lambda w_: jax.ShapeDtypeStruct((nt, w_), BF16)
    wkv = GQA_KV_HEADS * GQA_HD
    return pl.pallas_call(
        functools.partial(_att_in_kernel, d=d, tpb=tpb, nct=nct, nb=nb),
        grid=(nt // tm,),
        in_specs=[pl.BlockSpec((tm, d), lambda i: (i, 0)), const(mod), const(gain), const(w),
                  const(qg_gain), const(kg_gain), const(dq_gain), const(dk_gain), const(bd),
                  tab(GQA_HD), tab(GQA_HD), tab(HALF), tab(HALF)],
        out_specs=[out(HALF), out(wkv), out(wkv), out(HALF), out(HALF), out(HALF)],
        out_shape=[shp(HALF), shp(wkv), shp(wkv), shp(HALF), shp(HALF), shp(HALF)],
        compiler_params=_cparams("arbitrary"),
        name="att_in_proj",
    )(h, mod, gain, w, qg_gain, kg_gain, dq_gain, dk_gain, bd, cg, sg, cd, sd)


def _softmax_parts(q, k):
    s = lax.dot_general(q, k, _NT, preferred_element_type=F32)
    e = jnp.exp2(s - jnp.max(s, axis=-1, keepdims=True))
    return e, 1.0 / jnp.sum(e, axis=-1, keepdims=True)


def _gqa_kernel(q_ref, k_ref, v_ref, o_ref, *, group):
    for h in range(GQA_HEADS):
        sl = slice(h * GQA_HD, (h + 1) * GQA_HD)
        kv = slice(h // group * GQA_HD, (h // group + 1) * GQA_HD)
        e, inv = _softmax_parts(q_ref[:, sl], k_ref[:, kv])
        o_ref[:, sl] = (_dot(e.astype(BF16), v_ref[:, kv]) * inv).astype(BF16)


def _gqa_call(qg, kg, vg, *, nb, tq, tot, nqt, lat0):
    group = GQA_HEADS // GQA_KV_HEADS
    upb = tot // tq
    return pl.pallas_call(
        functools.partial(_gqa_kernel, group=group),
        grid=(nb, nqt),
        in_specs=[pl.BlockSpec((tq, GQA_HEADS * GQA_HD), lambda b, q: (b * upb + lat0 + q, 0)),
                  pl.BlockSpec((tot, GQA_KV_HEADS * GQA_HD), lambda b, q: (b, 0)),
                  pl.BlockSpec((tot, GQA_KV_HEADS * GQA_HD), lambda b, q: (b, 0))],
        out_specs=pl.BlockSpec((tq, GQA_HEADS * GQA_HD), lambda b, q: (b * nqt + q, 0)),
        out_shape=jax.ShapeDtypeStruct((nb * nqt * tq, HALF), BF16),
        compiler_params=_cparams("arbitrary", "arbitrary"),
        name="gqa_attention",
    )(qg, kg, vg)


def _diff_kernel(q_ref, k_ref, v_ref, lam_ref, gain_ref, o_ref, *, lam_init):
    lp = lam_ref[...]
    lam = (jnp.exp(jnp.sum(lp[0:1] * lp[1:2], axis=-1, keepdims=True))
           - jnp.exp(jnp.sum(lp[2:3] * lp[3:4], axis=-1, keepdims=True)) + lam_init)
    hd2 = 2 * DIFF_HD
    for j in range(q_ref.shape[1] // hd2):
        sl = slice(j * hd2, (j + 1) * hd2)
        q = q_ref[:, sl]
        k = k_ref[:, sl]
        lane = lax.broadcasted_iota(I32, q.shape, 1)
        zero = jnp.zeros_like(q)
        e1, inv1 = _softmax_parts(jnp.where(lane < DIFF_HD, q, zero), k)
        e2, inv2 = _softmax_parts(jnp.where(lane >= DIFF_HD, q, zero), k)
        v = v_ref[:, sl]
        o = _dot(e1.astype(BF16), v) * inv1 - _dot(e2.astype(BF16), v) * (lam * inv2)
        y = o * lax.rsqrt(jnp.mean(o * o, axis=-1, keepdims=True) + NORM_EPS) * gain_ref[...]
        o_ref[:, sl] = (y * (1.0 - lam_init)).astype(BF16)


def _diff_call(qd, kd, vd, lam_params, gain, *, nb, tq, tot, nqt, lat0, lam_init):
    upb = tot // tq
    hd2 = DIFF_HEADS * (2 * DIFF_HD)
    return pl.pallas_call(
        functools.partial(_diff_kernel, lam_init=lam_init),
        grid=(nb, HALF // hd2, nqt),
        in_specs=[pl.BlockSpec((tq, hd2), lambda b, h, q: (b * upb + lat0 + q, h)),
                  pl.BlockSpec((tot, hd2), lambda b, h, q: (b, h)),
                  pl.BlockSpec((tot, hd2), lambda b, h, q: (b, h)),
                  pl.BlockSpec(lam_params.shape, lambda b, h, q: (0, 0)),
                  pl.BlockSpec(gain.shape, lambda b, h, q: (0, 0))],
        out_specs=pl.BlockSpec((tq, hd2), lambda b, h, q: (b * nqt + q, h)),
        out_shape=jax.ShapeDtypeStruct((nb * nqt * tq, HALF), BF16),
        compiler_params=_cparams("arbitrary", "arbitrary", "arbitrary"),
        name="diff_attention",
    )(qd, kd, vd, lam_params, gain)


def _deinterleave(hd):
    return np.concatenate([np.arange(0, hd, 2), np.arange(1, hd, 2)])


def _head_perm(n_heads, hd):
    return np.concatenate([h * hd + _deinterleave(hd) for h in range(n_heads)])


def _rope_tables(n_lat, n_ctx, hd, reps):
    t = np.arange(n_lat)
    row, col = (t // GRID_W).astype(np.float64), (t % GRID_W).astype(np.float64)
    axis_dim = hd // 2
    inv_freq = ROPE_THETA ** (-np.arange(0, axis_dim, 2, dtype=np.float64) / axis_dim)
    ang = np.concatenate([row[:, None] * inv_freq, col[:, None] * inv_freq], axis=-1)
    cos = np.concatenate([np.ones((n_ctx, hd // 2)), np.cos(ang)], axis=0)
    sin = np.concatenate([np.zeros((n_ctx, hd // 2)), np.sin(ang)], axis=0)
    c = np.tile(np.concatenate([cos, cos], axis=1), (1, reps))
    s = np.tile(np.concatenate([-sin, sin], axis=1), (1, reps))
    return jnp.asarray(c, F32), jnp.asarray(s, F32)


def _retention_tables(n):
    gam = 1.0 - 2.0 ** (-5.0 - np.arange(RET_HEADS, dtype=np.float64))
    gams = (gam, gam[::-1])
    c = np.arange(n, dtype=np.float64)
    diff = c[:, None] - c[None, :]
    dmat = np.zeros((2, RET_HEADS, n, n))
    qs = np.zeros((2, n, RET_HEADS * RET_DK))
    ks = np.zeros((2, n, RET_HEADS * RET_DK))
    gpow = np.zeros((2, 8, RET_HEADS * RET_DK))
    for d in range(2):
        for h in range(RET_HEADS):
            g = gams[d][h]
            sl = slice(h * RET_DK, (h + 1) * RET_DK)
            if d == 0:
                dmat[d, h] = np.where(diff >= 0, g ** np.maximum(diff, 0), 0.0)
                qs[d, :, sl] = (g ** (c + 1))[:, None]
                ks[d, :, sl] = (g ** (n - 1 - c))[:, None]
            else:
                dmat[d, h] = np.where(diff <= 0, g ** np.maximum(-diff, 0), 0.0)
                qs[d, :, sl] = (g ** (n - c))[:, None]
                ks[d, :, sl] = (g ** c)[:, None]
            gpow[d, :, sl] = g ** n
    return tuple(jnp.asarray(a, F32) for a in (dmat, qs, ks, gpow))


def _plan(nb, n_lat, n_ctx):
    tm = 256 if (n_ctx % 256 == 0 and n_lat % 256 == 0) else 128
    assert n_ctx % tm == 0 and n_lat % tm == 0 and tm % CHUNK == 0 and n_lat % GRID_W == 0
    tq = tm
    return tm, tq


def kernel(x, c, ctx, c_ctx, norm_mix, norm_ffn, w_ada, b_ada, rec_w_in, rec_lb_logits, rec_w_out, rec_hg_gain,
           rec_ret_gain, att_w_in, att_w_out, att_q_gain, att_k_gain, diff_q_gain, diff_k_gain, diff_lambda,
           diff_gain, w_router, b_router, w_gu, b_gu, w_down, b_down):
    nb, n_lat, d = x.shape
    n_ctx = ctx.shape[1]
    assert w_ada.shape[0] == 2, "two layers: recurrent mixer then attention mixer"
    tm, tq = _plan(nb, n_lat, n_ctx)
    tot = n_ctx + n_lat
    tpb, nct = tot // tm, n_ctx // tm
    nt = nb * tot
    nql = n_lat // tm

    unified_row = functools.partial(_unified_row, tpb=tpb, nct=nct, nb=nb)
    latent_row = lambda i: lax.div(i, nql)
    latent_block = lambda i: lax.div(i, nql) * tpb + nct + lax.rem(i, nql)

    n_cond = -(-(nb + 1) // 8) * 8
    cond = jnp.concatenate([c, c_ctx[None], jnp.zeros((n_cond - nb - 1, d), F32)], axis=0)
    mod = _ada_call(cond, w_ada, b_ada)

    wr = jnp.pad(jnp.swapaxes(w_router, 1, 2), ((0, 0), (0, E_PAD - N_EXPERTS), (0, 0)))
    br = b_router[:, :, None]

    is_context = lambda i: lax.rem(i, tpb) < nct
    stream0 = [(x.reshape(nb * n_lat, d), lambda i: lax.div(i, tpb) * nql + jnp.maximum(lax.rem(i, tpb) - nct, 0)),
               (ctx.reshape(nb * n_ctx, d), lambda i: lax.div(i, tpb) * nct + jnp.minimum(lax.rem(i, tpb), nct - 1))]

    lb = jnp.cumsum(jax.nn.softmax(rec_lb_logits.astype(F32), axis=0), axis=0)[0]
    w_in = rec_w_in[0]
    c_rq = HG_HEADS * HG_DK * 3 + HG_HEADS * HG_DV * 2
    nrk = RET_HEADS * RET_DK
    perm = _head_perm(RET_HEADS, RET_DK)
    w_in = jnp.concatenate([w_in[:, :c_rq], w_in[:, c_rq + perm], w_in[:, c_rq + nrk + perm] * (RET_DK ** -0.5),
                            w_in[:, c_rq + 2 * nrk:]], axis=1).astype(BF16)
    cs_r, sn_r = _rope_tables(n_lat, n_ctx, RET_DK, 2 * RET_HEADS)
    dmat, qs, ks, g64 = _retention_tables(tm)
    p0 = _inproj_call(stream0, mod[0], norm_mix[0][None], w_in, cs_r, sn_r, tm=tm, tpb=tpb, nct=nct, nb=nb)
    ohf, orf, ohb, orb = _rec_call(p0, lb, dmat, qs, ks, g64, nb=nb, tc=tm, tpb=tpb, nct=nct)
    mixer = [(ohf, 0), (ohb, 0), (orf, 0), (orb, 0), (p0, 4), (p0, 7),
             (rec_hg_gain[0][None], None), (rec_ret_gain[0][None], None)]
    hn, f, *routing = _post_call(
        mixer, stream0, mod[0], rec_w_out[0].astype(BF16), norm_ffn[0][None], wr[0], br[0],
        tm=tm, n_tiles=nt // tm, row_fn=unified_row, is_context=is_context)
    h = _moe(f, hn, mod[0], *routing, w_gu, b_gu, w_down, b_down, tm=tm, row_fn=unified_row, layer=0)

    pg, pd = _head_perm(GQA_HEADS, GQA_HD), _head_perm(2 * DIFF_HEADS, DIFF_HD)
    w_in = att_w_in[0]
    o_gk = GQA_HEADS * GQA_HD
    o_gv = o_gk + GQA_KV_HEADS * GQA_HD
    o_dq = o_gv + GQA_KV_HEADS * GQA_HD
    o_dk = o_dq + HALF
    o_dv = o_dk + HALF
    w_in = jnp.concatenate([w_in[:, pg], w_in[:, o_gk + pg[:GQA_KV_HEADS * GQA_HD]], w_in[:, o_gv:o_dq],
                            w_in[:, o_dq + pd], w_in[:, o_dk + pd], w_in[:, o_dv:]], axis=1).astype(BF16)
    dg, dd = _deinterleave(GQA_HD), _deinterleave(DIFF_HD)
    cg, sg = _rope_tables(n_lat, n_ctx, GQA_HD, 1)
    cd, sd = _rope_tables(n_lat, n_ctx, DIFF_HD, 2 * DIFF_HEADS)
    bd = jnp.asarray(np.kron(np.eye(2 * DIFF_HEADS), np.ones((DIFF_HD, DIFF_HD))), BF16)
    log2e = math.log2(math.e)
    qg, kg, vg, qd, kd, vd = _att_in_call(
        h, mod[1], norm_mix[1][None], w_in,
        (att_q_gain[0][dg] * (GQA_HD ** -0.5 * log2e))[None], att_k_gain[0][dg][None],
        jnp.tile(diff_q_gain[0][dd] * (DIFF_HD ** -0.5 * log2e), 2 * DIFF_HEADS)[None],
        jnp.tile(diff_k_gain[0][dd], 2 * DIFF_HEADS)[None],
        bd, cg, sg, cd, sd, tm=tm, tpb=tpb, nct=nct, nb=nb)
    nqt = n_lat // tq
    yg = _gqa_call(qg, kg, vg, nb=nb, tq=tq, tot=tot, nqt=nqt, lat0=n_ctx // tq)
    lam_init = 0.8 - 0.6 * math.exp(-0.3 * 1)
    yd = _diff_call(qd, kd, vd, diff_lambda[0], diff_gain[0][None], nb=nb, tq=tq, tot=tot, nqt=nqt,
                    lat0=n_ctx // tq, lam_init=lam_init)
    hn, f, *routing = _post_call(
        [(yg, 0), (yd, 0)], [(h, latent_block)], mod[1], att_w_out[0].astype(BF16), norm_ffn[1][None], wr[1], br[1],
        tm=tm, n_tiles=nb * nql, row_fn=latent_row)
    out = _moe(f, hn, mod[1], *routing, w_gu, b_gu, w_down, b_down, tm=tm, row_fn=latent_row, layer=1)
    return out.reshape(nb, n_lat, d)
```

```python
import functools
import math

import numpy as np
import jax
import jax.numpy as jnp
from jax import lax
from jax.experimental import pallas as pl
from jax.experimental.pallas import tpu as pltpu

F32 = jnp.float32
BF16 = jnp.bfloat16
I32 = jnp.int32

GRID_W = 64
HG_HEADS, HG_DK, HG_DV = 4, 128, 128
RET_HEADS, RET_DK, RET_DV = 4, 64, 128
GQA_HEADS, GQA_KV_HEADS, GQA_HD = 4, 2, 128
DIFF_HEADS, DIFF_HD = 4, 64
N_EXPERTS, TOP_K = 32, 4
SWIGLU_LIMIT, SWIGLU_ALPHA = 7.0, 1.702
CHUNK = 64
ROPE_THETA = 10000.0
NORM_EPS = 1e-6

LANES = 128
E_PAD = LANES
V7X_VMEM_LIMIT = 56 * 1024 * 1024
HALF = 512

_NT = (((1,), (1,)), ((), ()))
_TN = (((0,), (0,)), ((), ()))


def _cparams(*sem):
    return pltpu.CompilerParams(dimension_semantics=sem, vmem_limit_bytes=V7X_VMEM_LIMIT)


def _silu(x):
    return x * jax.nn.sigmoid(x)


def _dot(a, b):
    return jnp.dot(a, b, preferred_element_type=F32)


def _split_bf16(x, terms):
    out = []
    for _ in range(terms):
        p = x.astype(BF16)
        out.append(p)
        x = x - p.astype(F32)
    return out


def _norm_mod(x, gain, shift, scale):
    y = x * lax.rsqrt(jnp.mean(x * x, axis=-1, keepdims=True) + NORM_EPS) * gain
    return y * (1.0 + scale) + shift


def _unified_row(i, tpb, nct, nb):
    return jnp.where(lax.rem(i, tpb) < nct, nb, lax.div(i, tpb))


def _ada_kernel(cond_ref, w_ref, b_ref, o_ref):
    s = _silu(cond_ref[...]).astype(BF16)
    o_ref[0] = _dot(s, w_ref[0].astype(BF16)) + b_ref[0]


def _ada_call(cond, w_ada, b_ada):
    n_layers, d, d6 = w_ada.shape
    r = cond.shape[0]
    tn = d6 // 4
    return pl.pallas_call(
        _ada_kernel,
        grid=(n_layers, d6 // tn),
        in_specs=[pl.BlockSpec((r, d), lambda l, n: (0, 0)),
                  pl.BlockSpec((1, d, tn), lambda l, n: (l, 0, n)),
                  pl.BlockSpec((1, 1, tn), lambda l, n: (l, 0, n))],
        out_specs=pl.BlockSpec((1, r, tn), lambda l, n: (l, 0, n)),
        out_shape=jax.ShapeDtypeStruct((n_layers, r, d6), F32),
        compiler_params=_cparams("arbitrary", "arbitrary"),
        name="ada_modulation",
    )(cond, w_ada, b_ada.reshape(n_layers, 1, d6))


REC_COL_Q = 0
REC_COL_RQK = 5


def _inproj_kernel(x_ref, c_ref, mod_ref, gain_ref, w_ref, cs_ref, sn_ref, o_ref, *, d, tpb, nct, nb):
    i = pl.program_id(0)
    r = _unified_row(i, tpb, nct, nb)
    h = _stream_rows((x_ref, c_ref), lambda t: lax.rem(t, tpb) < nct, i, slice(None))
    a = _norm_mod(h, gain_ref[...], mod_ref[pl.ds(r, 1), 0:d], mod_ref[pl.ds(r, 1), d:2 * d])
    p = _dot(a.astype(BF16), w_ref[...])
    for blk in range(p.shape[1] // HALF):
        sl = slice(blk * HALF, (blk + 1) * HALF)
        x = p[:, sl]
        if blk == REC_COL_Q:
            x = _silu(x) * (HG_DK ** -0.5)
        elif blk == REC_COL_RQK:
            x = _rope(x, cs_ref[...], sn_ref[...], RET_DK)
        o_ref[:, sl] = x


def _inproj_call(stream, mod, gain, w, cs, sn, *, tm, tpb, nct, nb):
    d = stream[0][0].shape[1]
    nt = nb * tpb * tm
    n_out = w.shape[1]
    table = pl.BlockSpec((tm, HALF), lambda i: (lax.rem(i, tpb), 0))
    return pl.pallas_call(
        functools.partial(_inproj_kernel, d=d, tpb=tpb, nct=nct, nb=nb),
        grid=(nt // tm,),
        in_specs=[pl.BlockSpec((tm, d), lambda i, blk=blk: (blk(i), 0)) for _, blk in stream]
                 + [pl.BlockSpec(mod.shape, lambda i: (0, 0)),
                  pl.BlockSpec((1, d), lambda i: (0, 0)),
                  pl.BlockSpec((d, n_out), lambda i: (0, 0)), table, table],
        out_specs=pl.BlockSpec((tm, n_out), lambda i: (i, 0)),
        out_shape=jax.ShapeDtypeStruct((nt, n_out), F32),
        compiler_params=_cparams("arbitrary"),
        name="rec_in_proj",
    )(*[a for a, _ in stream], mod, gain, w, cs, sn)


def _swap_halves(x, group):
    if group == LANES:
        return pltpu.roll(x, LANES // 2, 1)
    lane = lax.broadcasted_iota(I32, x.shape, 1)
    half = group // 2
    return jnp.where(lax.rem(lane, group) < half, pltpu.roll(x, LANES - half, 1), pltpu.roll(x, half, 1))


def _rope(x, cos, sin, group):
    parts = [_swap_halves(x[:, s:s + LANES], group) for s in range(0, x.shape[1], LANES)]
    rot = parts[0] if len(parts) == 1 else jnp.concatenate(parts, axis=1)
    return x * cos + rot * sin


def _hgrn_prepare(d, hq, hf, hi, lb_ref, tri_blocks):
    tc = hf.shape[0]
    lbv = lb_ref[d:d + 1, :]
    f = lbv + (1.0 - lbv) * jax.nn.sigmoid(hf[...])
    logf = jnp.log(f)
    kk = 1.0 - f
    b = sum(_dot(tri_blocks, p) for p in _split_bf16(logf, 3))
    last = CHUNK - 1 if d == 0 else 0
    tots = [b[c * CHUNK + last:c * CHUNK + last + 1, :] for c in range(tc // CHUNK)]
    tot_rows = jnp.concatenate([jnp.broadcast_to(t, (CHUNK, t.shape[1])) for t in tots], axis=0)
    q_in = (hq[...] * jnp.exp(b)).astype(BF16)
    k_in = (kk * jnp.exp(-b)).astype(BF16)
    k_st = (kk * jnp.exp(tot_rows - b)).astype(BF16)
    return q_in, k_in, k_st, [jnp.exp(t) for t in tots], hi[...].astype(BF16)


def _hgrn_chunk(d, c, prepared, ohg, st_hg, tri_mask):
    q_all, k_all, kst_all, decs, v_all = prepared
    rows = slice(c * CHUNK, (c + 1) * CHUNK)
    q_in, k_in, k_st, v, dec = q_all[rows], k_all[rows], kst_all[rows], v_all[rows], decs[c]
    for h in range(HG_HEADS):
        sl = slice(h * HG_DK, (h + 1) * HG_DK)
        att = jnp.where(tri_mask, lax.dot_general(q_in[:, sl], k_in[:, sl], _NT, preferred_element_type=F32), 0.0)
        s_t = st_hg[d, h]
        o = _dot(att.astype(BF16), v[:, sl]) + lax.dot_general(
            q_in[:, sl], s_t.astype(BF16), _NT, preferred_element_type=F32)
        ohg[rows, sl] = o
        st_hg[d, h] = s_t * dec[:, sl] + lax.dot_general(v[:, sl], k_st[:, sl], _TN, preferred_element_type=F32)


def _retention_tile(d, rqk, rv, dmat_ref, qs_ref, ks_ref, gpow_ref, oret, st_ret):
    xr = rqk[...]
    nq = RET_HEADS * RET_DK
    q_r = xr[:, 0:nq]
    k_r = xr[:, nq:2 * nq]
    q_b = q_r.astype(BF16)
    k_b = k_r.astype(BF16)
    q_sc = (q_r * qs_ref[d]).astype(BF16)
    k_sc = (k_r * ks_ref[d]).astype(BF16)
    vv = rv[...].astype(BF16)
    for h in range(RET_HEADS):
        sk = slice(h * RET_DK, (h + 1) * RET_DK)
        sv = slice(h * RET_DV, (h + 1) * RET_DV)
        att = lax.dot_general(q_b[:, sk], k_b[:, sk], _NT, preferred_element_type=F32) * dmat_ref[d, h]
        s_t = st_ret[d, h]
        o = _dot(att.astype(BF16), vv[:, sv]) + lax.dot_general(
            q_sc[:, sk], s_t.astype(BF16), _NT, preferred_element_type=F32)
        oret[:, sv] = o
        st_ret[d, h] = s_t * gpow_ref[d, 0:1, sk] + lax.dot_general(
            vv[:, sv], k_sc[:, sk], _TN, preferred_element_type=F32)


def _rec_kernel(hq_f, hf_f, hi_f, rqk_f, rv_f,
                hq_b, hf_b, hi_b, rqk_b, rv_b,
                lb_ref, dmat_ref, qs_ref, ks_ref, g64_ref,
                ohg_f, oret_f, ohg_b, oret_b, st_hg, st_ret, *, tc):
    @pl.when(pl.program_id(1) == 0)
    def _():
        st_hg[...] = jnp.zeros_like(st_hg)
        st_ret[...] = jnp.zeros_like(st_ret)

    nch = tc // CHUNK
    row = lax.broadcasted_iota(I32, (CHUNK, CHUNK), 0)
    col = lax.broadcasted_iota(I32, (CHUNK, CHUNK), 1)
    mask_f = col <= row
    mask_b = col >= row
    trow = lax.broadcasted_iota(I32, (tc, tc), 0)
    tcol = lax.broadcasted_iota(I32, (tc, tc), 1)
    same_chunk = lax.div(trow, CHUNK) == lax.div(tcol, CHUNK)
    tri_f = jnp.logical_and(same_chunk, tcol <= trow).astype(F32).astype(BF16)
    tri_b = jnp.logical_and(same_chunk, tcol >= trow).astype(F32).astype(BF16)

    ret_tables = (dmat_ref, qs_ref, ks_ref, g64_ref)
    _retention_tile(0, rqk_f, rv_f, *ret_tables, oret_f, st_ret)
    _retention_tile(1, rqk_b, rv_b, *ret_tables, oret_b, st_ret)
    prep_f = _hgrn_prepare(0, hq_f, hf_f, hi_f, lb_ref, tri_f)
    prep_b = _hgrn_prepare(1, hq_b, hf_b, hi_b, lb_ref, tri_b)
    for ci in range(nch):
        _hgrn_chunk(0, ci, prep_f, ohg_f, st_hg, mask_f)
        _hgrn_chunk(1, nch - 1 - ci, prep_b, ohg_b, st_hg, mask_b)


def _rec_call(p0, lb, dmat, qs, ks, g64, *, nb, tc, tpb, nct):
    nt = p0.shape[0]

    def fwd(b, j):
        return j

    def bwd(b, j):
        return jnp.where(j < nct, nct - 1 - j, tpb + nct - 1 - j)

    def pspec(col, pos):
        return pl.BlockSpec((tc, HALF), lambda b, j: (b * tpb + pos(b, j), col))

    def whole(a):
        return pl.BlockSpec(a.shape, lambda b, j: (0,) * a.ndim)

    in_specs = ([pspec(REC_COL_Q, fwd), pspec(1, fwd), pspec(3, fwd), pspec(REC_COL_RQK, fwd), pspec(6, fwd)]
                + [pspec(REC_COL_Q, bwd), pspec(2, bwd), pspec(3, bwd), pspec(REC_COL_RQK, bwd), pspec(6, bwd)]
                + [whole(a) for a in (lb, dmat, qs, ks, g64)])
    out_f = pl.BlockSpec((tc, HALF), lambda b, j: (b * tpb + fwd(b, j), 0))
    out_b = pl.BlockSpec((tc, HALF), lambda b, j: (b * tpb + bwd(b, j), 0))
    o_shape = jax.ShapeDtypeStruct((nt, HALF), F32)
    return pl.pallas_call(
        functools.partial(_rec_kernel, tc=tc),
        grid=(nb, tpb),
        in_specs=in_specs,
        out_specs=[out_f, out_f, out_b, out_b],
        out_shape=[o_shape] * 4,
        scratch_shapes=[pltpu.VMEM((2, HG_HEADS, HG_DV, HG_DK), F32),
                        pltpu.VMEM((2, RET_HEADS, RET_DV, RET_DK), F32)],
        compiler_params=_cparams("arbitrary", "arbitrary"),
        name="rec_scan",
    )(*([p0] * 10), lb, dmat, qs, ks, g64)


def _rec_mixer_rows(rs, ohf, ohb, orf, orb, hg_ref, rg_ref, hgain, rgain):
    oh = ohf[rs, :] + ohb[rs, :]
    orr = orf[rs, :] + orb[rs, :]
    hg = hg_ref[rs, :]
    rg = rg_ref[rs, :]
    y_h, y_r = [], []
    for h in range(HG_HEADS):
        sl = slice(h * LANES, (h + 1) * LANES)
        o = oh[:, sl]
        y = o * lax.rsqrt(jnp.mean(o * o, axis=-1, keepdims=True) + NORM_EPS) * hgain[...]
        y_h.append((y * _silu(hg[:, sl])).astype(BF16))
        o = orr[:, sl]
        oc = o - jnp.mean(o, axis=-1, keepdims=True)
        y = oc * lax.rsqrt(jnp.mean(oc * oc, axis=-1, keepdims=True) + NORM_EPS) * rgain[...]
        y_r.append((y * _silu(rg[:, sl])).astype(BF16))
    return jnp.concatenate(y_h + y_r, axis=1)


def _stream_rows(h_refs, is_context, i, rs):
    if len(h_refs) == 1:
        return h_refs[0][rs, :]
    return jnp.where(is_context(i), h_refs[1][rs, :], h_refs[0][rs, :])


def _post_kernel(*refs, d, tm, row_fn, n_mixer_refs, n_h_refs, is_context):
    mixer_refs = refs[:n_mixer_refs]
    h_refs = refs[n_mixer_refs:n_mixer_refs + n_h_refs]
    (mod_ref, wout_ref, gain_ref, wr_ref, br_ref,
     hn_ref, f_ref, lpos_ref, gate_ref, ne_ref, cb_ref, cnt_ref, cnt_sc) = refs[n_mixer_refs + n_h_refs:]
    if n_mixer_refs == 2:
        mixer_rows = lambda rs: jnp.concatenate([mixer_refs[0][rs, :], mixer_refs[1][rs, :]], axis=1)
    else:
        mixer_rows = lambda rs: _rec_mixer_rows(rs, *mixer_refs)
    i = pl.program_id(0)

    @pl.when(i == 0)
    def _():
        cnt_sc[...] = jnp.zeros_like(cnt_sc)

    r = row_fn(i)
    w_hi, w_lo = _split_bf16(wr_ref[...], 2)
    nt_dot = lambda a, b: lax.dot_general(a, b, _NT, preferred_element_type=F32)
    n_part = 2 if tm % (2 * LANES) == 0 else 1
    logit_parts = []
    for part in range(n_part):
        rs = slice(part * tm // n_part, (part + 1) * tm // n_part)
        hn = (_stream_rows(h_refs, is_context, i, rs)
              + mod_ref[pl.ds(r, 1), 2 * d:3 * d] * _dot(mixer_rows(rs), wout_ref[...]))
        hn_ref[rs, :] = hn
        f = _norm_mod(hn, gain_ref[...], mod_ref[pl.ds(r, 1), 3 * d:4 * d], mod_ref[pl.ds(r, 1), 4 * d:5 * d])
        f_hi, f_lo = _split_bf16(f, 2)
        f_ref[rs, :] = f_hi
        logit_parts.append((nt_dot(w_hi, f_hi) + nt_dot(w_lo, f_hi) + nt_dot(w_hi, f_lo))[0:N_EXPERTS])
    logits = (logit_parts[0] if n_part == 1 else jnp.concatenate(logit_parts, axis=1)) + br_ref[...]
    row = lax.broadcasted_iota(I32, (N_EXPERTS, tm), 0)
    vals, idxs = [], []
    work = logits
    for _ in range(TOP_K):
        m = jnp.max(work, axis=0, keepdims=True)
        sel = jnp.min(jnp.where(work == m, row, N_EXPERTS), axis=0, keepdims=True)
        vals.append(m)
        idxs.append(sel)
        work = jnp.where(row == sel, -jnp.inf, work)
    exps = [jnp.exp(v - vals[0]) for v in vals]
    inv = 1.0 / sum(exps)
    onehot = sum((row == s).astype(F32) for s in idxs)
    n_e = jnp.broadcast_to(jnp.sum(onehot, axis=1, keepdims=True), (N_EXPERTS, LANES))
    rr = lax.broadcasted_iota(I32, (tm, tm), 0)
    cc = lax.broadcasted_iota(I32, (tm, tm), 1)
    within = _dot(onehot.astype(BF16), (rr < cc).astype(F32).astype(BF16))
    ea = lax.broadcasted_iota(I32, (N_EXPERTS, N_EXPERTS), 0)
    eb = lax.broadcasted_iota(I32, (N_EXPERTS, N_EXPERTS), 1)
    before = _dot((eb < ea).astype(F32).astype(BF16), n_e.astype(BF16))[:, 0:1]
    base = within + before
    sub = lax.broadcasted_iota(I32, (8, tm), 0)
    lpos_slab = jnp.zeros((8, tm), F32)
    gate_slab = jnp.zeros((8, tm), F32)
    for k in range(TOP_K):
        gate_slab = jnp.where(sub == k, exps[k] * inv, gate_slab)
        lp = jnp.sum(jnp.where(row == idxs[k], base, 0.0), axis=0, keepdims=True)
        lpos_slab = jnp.where(sub == k, lp, lpos_slab)
    lpos_ref[...] = lpos_slab.astype(I32)
    gate_ref[...] = gate_slab
    ne_ref[0] = n_e
    cb_ref[0] = cnt_sc[...]
    cnt_sc[...] = cnt_sc[...] + n_e
    cnt_ref[...] = cnt_sc[...]


def _post_call(mixer, stream, mod, wout, gain, wr, br, *, tm, n_tiles, row_fn, is_context=None):
    d = stream[0][0].shape[1]
    n = n_tiles * tm
    tile = lambda w: pl.BlockSpec((tm, w), lambda i: (i, 0))
    const = lambda a: pl.BlockSpec(a.shape, lambda i: (0,) * a.ndim)
    routing = pl.BlockSpec((8, tm), lambda i: (i, 0))
    counts = pl.BlockSpec((1, N_EXPERTS, LANES), lambda i: (i, 0, 0))
    mixer_specs = [const(a) if col is None else pl.BlockSpec((tm, HALF), lambda i, col=col: (i, col))
                   for a, col in mixer]
    stream_specs = [pl.BlockSpec((tm, d), lambda i, blk=blk: (blk(i), 0)) for _, blk in stream]
    return pl.pallas_call(
        functools.partial(_post_kernel, d=d, tm=tm, row_fn=row_fn, n_mixer_refs=len(mixer),
                          n_h_refs=len(stream), is_context=is_context),
        grid=(n_tiles,),
        in_specs=mixer_specs + stream_specs + [const(mod), const(wout), const(gain), const(wr), const(br)],
        out_specs=[tile(d), tile(d), routing, routing, counts, counts,
                   pl.BlockSpec((N_EXPERTS, LANES), lambda i: (0, 0))],
        out_shape=[jax.ShapeDtypeStruct((n, d), F32), jax.ShapeDtypeStruct((n, d), BF16),
                   jax.ShapeDtypeStruct((n_tiles * 8, tm), I32), jax.ShapeDtypeStruct((n_tiles * 8, tm), F32),
                   jax.ShapeDtypeStruct((n_tiles, N_EXPERTS, LANES), F32),
                   jax.ShapeDtypeStruct((n_tiles, N_EXPERTS, LANES), F32),
                   jax.ShapeDtypeStruct((N_EXPERTS, LANES), F32)],
        scratch_shapes=[pltpu.VMEM((N_EXPERTS, LANES), F32)],
        compiler_params=_cparams("arbitrary"),
        name="post_mixer",
    )(*[a for a, _ in mixer], *[a for a, _ in stream], mod, wout, gain, wr, br)


SITE_TABLE_WIDTH = 1024
_SITE_HEAD = 16
_SITE_CLASS = 2 * N_EXPERTS


def _chunk_sizes(limit):
    return [1 << b for b in range(limit.bit_length() - 1, -1, -1)]


def _token_rows(start, count, pitch):
    return pl.ds(pl.multiple_of(start * pitch, pitch), count * pitch)


def _site_table(local_start, grouped_start, length, *, tm, pitch):
    sizes = jnp.asarray(_chunk_sizes(tm), I32)
    n_tiles, n_cls = length.shape[0], sizes.shape[0]
    assert n_cls <= _SITE_HEAD and _SITE_HEAD + n_cls * _SITE_CLASS <= SITE_TABLE_WIDTH
    n = length[..., None]
    done = jnp.bitwise_and(n, -2 * sizes)
    valid = jnp.bitwise_and(n, sizes) != 0
    loc = (local_start[..., None] + done) * pitch
    grp = (grouped_start[..., None] + done) * pitch
    rank = jnp.cumsum(valid.astype(I32), axis=1) - 1
    slot_of = jnp.logical_and(valid[..., None], rank[..., None] == jnp.arange(N_EXPERTS, dtype=I32))
    pick = lambda a: jnp.sum(jnp.where(slot_of, a[..., None], 0), axis=1)
    pairs = jnp.stack([pick(loc), pick(grp)], axis=-1).reshape(n_tiles, n_cls * _SITE_CLASS)
    counts = jnp.pad(jnp.sum(valid.astype(I32), axis=1), ((0, 0), (0, _SITE_HEAD - n_cls)))
    table = jnp.concatenate([counts, pairs], axis=1)
    return jnp.pad(table, ((0, 0), (0, SITE_TABLE_WIDTH - table.shape[1])))[:, None, :]


def _segment_copies(site_ref, local, slot, grouped, sem, *, tm, pitch, to_grouped):
    for k, size in enumerate(_chunk_sizes(tm)):
        base = _SITE_HEAD + k * _SITE_CLASS

        def body(j, c, base=base, size=size):
            loc0 = site_ref[0, 0, base + 2 * j]
            grp0 = site_ref[0, 0, base + 2 * j + 1]
            loc = local.at[slot, pl.ds(pl.multiple_of(loc0, pitch), size * pitch)]
            grp = grouped.at[pl.ds(pl.multiple_of(grp0, pitch), size * pitch)]
            (pltpu.make_async_copy(loc, grp, sem) if to_grouped else pltpu.make_async_copy(grp, loc, sem)).start()
            return c

        lax.fori_loop(0, site_ref[0, 0, k], body, 0)


def _pad_zero_copies(ztab_ref, zeros, grouped, sem, *, tm, pitch, wait):
    for e in range(N_EXPERTS):
        g0 = ztab_ref[e]
        n = ztab_ref[N_EXPERTS + e]
        for size in _chunk_sizes(tm // 2):
            done = jnp.bitwise_and(n, -2 * size)

            @pl.when(jnp.bitwise_and(n, size) != 0)
            def _():
                cp = pltpu.make_async_copy(zeros.at[pl.ds(0, size * pitch)],
                                           grouped.at[_token_rows(g0 + done, size, pitch)], sem)
                cp.wait() if wait else cp.start()


def _lanes_to_tiles(ref, x):
    rows, pitch = x.shape[0], x.shape[1] // LANES
    for c in range(pitch):
        ref[pl.ds(c, rows, stride=pitch), :] = x[:, c * LANES:(c + 1) * LANES]


def _tiles_to_lanes(ref, pitch):
    rows = ref.shape[0] // pitch
    return jnp.concatenate([ref[pl.ds(c, rows, stride=pitch), :] for c in range(pitch)], axis=1)


def _dispatch_kernel(ztab_ref, site_ref, f_ref, lpos_ref, xs_ref, zbuf, zeros, sems, zsem, *, tm, pitch):
    i = pl.program_id(0)
    n_tiles = pl.num_programs(0)
    slot = lax.rem(i, 2)
    rows = tm * TOP_K

    def tile_bytes(s):
        return pltpu.make_async_copy(zbuf.at[s], xs_ref.at[pl.ds(0, rows * pitch)], sems.at[s])

    @pl.when(i >= 2)
    def _():
        tile_bytes(slot).wait()

    pos = lax.broadcasted_iota(I32, (rows, tm), 0)
    lp = lpos_ref[...]
    sel = pos == lp[0:1, :]
    for k in range(1, TOP_K):
        sel = jnp.logical_or(sel, pos == lp[k:k + 1, :])
    z = _dot(jnp.where(sel, 1.0, 0.0).astype(BF16), f_ref[...])
    _lanes_to_tiles(zbuf.at[slot], z)
    _segment_copies(site_ref, zbuf, slot, xs_ref, sems.at[slot], tm=tm, pitch=pitch, to_grouped=True)

    @pl.when(i == n_tiles - 1)
    def _():
        zeros[...] = jnp.zeros_like(zeros)
        _pad_zero_copies(ztab_ref, zeros, xs_ref, zsem, tm=tm, pitch=pitch, wait=False)
        _pad_zero_copies(ztab_ref, zeros, xs_ref, zsem, tm=tm, pitch=pitch, wait=True)
        half = tm // 2

        def tail(j, start):
            cp = pltpu.make_async_copy(zeros, xs_ref.at[_token_rows(j * half, half, pitch)], zsem)
            cp.start() if start else cp.wait()

        first, last = ztab_ref[2 * N_EXPERTS], xs_ref.shape[0] // (half * pitch)
        lax.fori_loop(first, last, lambda j, c: (tail(j, True), c)[1], 0)
        lax.fori_loop(first, last, lambda j, c: (tail(j, False), c)[1], 0)
        tile_bytes(slot).wait()
        tile_bytes(1 - slot).wait()


def _dispatch_call(ztab, sites, f, lpos, n_rows, *, tm):
    n, d = f.shape
    assert n // tm >= 2 and tm & (tm - 1) == 0
    pitch = d // LANES
    grid_spec = pltpu.PrefetchScalarGridSpec(
        num_scalar_prefetch=1,
        grid=(n // tm,),
        in_specs=[pl.BlockSpec((1, 1, SITE_TABLE_WIDTH), lambda i, z: (i, 0, 0), memory_space=pltpu.SMEM),
                  pl.BlockSpec((tm, d), lambda i, z: (i, 0)),
                  pl.BlockSpec((8, tm), lambda i, z: (i, 0))],
        out_specs=pl.BlockSpec(memory_space=pl.ANY),
        scratch_shapes=[pltpu.VMEM((2, tm * TOP_K * pitch, LANES), F32), pltpu.VMEM((tm // 2 * pitch, LANES), F32),
                        pltpu.SemaphoreType.DMA((2,)), pltpu.SemaphoreType.DMA(())])
    return pl.pallas_call(
        functools.partial(_dispatch_kernel, tm=tm, pitch=pitch),
        grid_spec=grid_spec,
        out_shape=jax.ShapeDtypeStruct((n_rows * pitch, LANES), F32),
        compiler_params=_cparams("arbitrary"),
        name="moe_dispatch",
    )(ztab, sites, f, lpos)


def _expert_kernel(b0_ref, nb_ref, xs_ref, wgu_ref, bgu_ref, wd_ref, bd_ref, y_ref,
                   xbuf, ybuf, wgu_bf, wd_bf, xsem, ysem, *, de, tm, pitch):
    e = pl.program_id(0)
    last_e = pl.num_programs(0) - 1
    nb = nb_ref[e]
    b0 = b0_ref[e]
    n_used = b0_ref[last_e] + nb_ref[last_e]
    rows = tm * pitch

    def block(g):
        return pl.ds(pl.multiple_of(g * rows, rows), rows)

    def x_copy(g, s):
        return pltpu.make_async_copy(xs_ref.at[block(g)], xbuf.at[s], xsem.at[s])

    def y_copy(g, s):
        return pltpu.make_async_copy(ybuf.at[s], y_ref.at[block(g)], ysem.at[s])

    @pl.when(jnp.logical_and(e == 0, n_used > 0))
    def _():
        x_copy(0, 0).start()

    @pl.when(nb > 0)
    def _():
        wgu_bf[...] = wgu_ref[0, 0].astype(BF16)
        wd_bf[...] = wd_ref[0, 0].astype(BF16)

    def body(j, c):
        g = b0 + j
        s = lax.rem(g, 2)
        x_copy(g, s).wait()

        @pl.when(g >= 2)
        def _():
            y_copy(g - 2, s).wait()

        @pl.when(g + 1 < n_used)
        def _():
            x_copy(g + 1, 1 - s).start()

        gu = _dot(_tiles_to_lanes(xbuf.at[s], pitch).astype(BF16), wgu_bf[...]) + bgu_ref[0, 0]
        gate = jnp.minimum(gu[:, 0:de], SWIGLU_LIMIT)
        up = jnp.clip(gu[:, de:2 * de], -SWIGLU_LIMIT, SWIGLU_LIMIT)
        hid = (up + 1.0) * gate * jax.nn.sigmoid(SWIGLU_ALPHA * gate)
        _lanes_to_tiles(ybuf.at[s], _dot(hid.astype(BF16), wd_bf[...]) + bd_ref[0, 0])
        y_copy(g, s).start()
        return c

    lax.fori_loop(0, nb, body, 0)

    @pl.when(e == last_e)
    def _():
        @pl.when(n_used >= 2)
        def _():
            y_copy(n_used - 2, lax.rem(n_used, 2)).wait()

        @pl.when(n_used >= 1)
        def _():
            y_copy(n_used - 1, lax.rem(n_used + 1, 2)).wait()

        ybuf[0] = jnp.zeros(ybuf.shape[1:], F32)

        def tail(g, start):
            cp = pltpu.make_async_copy(ybuf.at[0], y_ref.at[block(g)], ysem.at[0])
            cp.start() if start else cp.wait()

        n_blocks = y_ref.shape[0] // rows
        lax.fori_loop(n_used, n_blocks, lambda g, c: (tail(g, True), c)[1], 0)
        lax.fori_loop(n_used, n_blocks, lambda g, c: (tail(g, False), c)[1], 0)


def _expert_call(blk0, nblk, xs, w_gu, b_gu, w_down, b_down, *, tm, layer):
    nl, ne, d, de2 = w_gu.shape
    pitch = d // LANES
    de = de2 // 2
    grid_spec = pltpu.PrefetchScalarGridSpec(
        num_scalar_prefetch=2,
        grid=(ne,),
        in_specs=[pl.BlockSpec(memory_space=pl.ANY),
                  pl.BlockSpec((1, 1, d, de2), lambda e, b0, nb: (layer, e, 0, 0)),
                  pl.BlockSpec((1, 1, 1, de2), lambda e, b0, nb: (layer, e, 0, 0)),
                  pl.BlockSpec((1, 1, de, d), lambda e, b0, nb: (layer, e, 0, 0)),
                  pl.BlockSpec((1, 1, 1, d), lambda e, b0, nb: (layer, e, 0, 0))],
        out_specs=pl.BlockSpec(memory_space=pl.ANY),
        scratch_shapes=[pltpu.VMEM((2, tm * pitch, LANES), F32), pltpu.VMEM((2, tm * pitch, LANES), F32),
                        pltpu.VMEM((d, de2), BF16), pltpu.VMEM((de, d), BF16),
                        pltpu.SemaphoreType.DMA((2,)), pltpu.SemaphoreType.DMA((2,))])
    return pl.pallas_call(
        functools.partial(_expert_kernel, de=de, tm=tm, pitch=pitch),
        grid_spec=grid_spec,
        out_shape=jax.ShapeDtypeStruct(xs.shape, F32),
        compiler_params=_cparams("arbitrary"),
        name="moe_experts",
    )(blk0, nblk, xs, w_gu, b_gu.reshape(nl, ne, 1, de2), w_down, b_down.reshape(nl, ne, 1, d))


def _combine_kernel(site0_ref, site1_ref, lpos_ref, gate_ref, h_ref, mod_ref, y_ref, o_ref, ybuf, sems,
                    *, d, tm, row_fn):
    i = pl.program_id(0)
    n_tiles = pl.num_programs(0)
    slot = lax.rem(i, 2)
    rows = tm * TOP_K
    pitch = d // LANES

    def fetch(site_ref, s):
        _segment_copies(site_ref, ybuf, s, y_ref, sems.at[s], tm=tm, pitch=pitch, to_grouped=False)

    @pl.when(i == 0)
    def _():
        fetch(site0_ref, 0)

    @pl.when(i + 1 < n_tiles)
    def _():
        fetch(site1_ref, 1 - slot)

    pltpu.make_async_copy(y_ref.at[pl.ds(0, rows * pitch)], ybuf.at[slot], sems.at[slot]).wait()
    yb = _tiles_to_lanes(ybuf.at[slot], pitch).astype(BF16)
    pos = lax.broadcasted_iota(I32, (rows, tm), 0)
    lp = lpos_ref[...]
    gates = gate_ref[...]
    g = jnp.zeros((rows, tm), F32)
    for k in range(TOP_K):
        g = jnp.where(pos == lp[k:k + 1, :], gates[k:k + 1, :], g)
    g_hi, g_lo = _split_bf16(g, 2)
    acc = (lax.dot_general(g_hi, yb, _TN, preferred_element_type=F32)
           + lax.dot_general(g_lo, yb, _TN, preferred_element_type=F32))
    r = row_fn(i)
    o_ref[...] = h_ref[...] + mod_ref[pl.ds(r, 1), 5 * d:6 * d] * acc


def _combine_call(sites, lpos, gates, hn, mod, y_rows, *, tm, row_fn):
    n, d = hn.shape
    pitch = d // LANES
    n_tiles = n // tm
    site_block = (1, 1, SITE_TABLE_WIDTH)
    return pl.pallas_call(
        functools.partial(_combine_kernel, d=d, tm=tm, row_fn=row_fn),
        grid=(n_tiles,),
        in_specs=[pl.BlockSpec(site_block, lambda i: (0, 0, 0), memory_space=pltpu.SMEM),
                  pl.BlockSpec(site_block, lambda i: (jnp.minimum(i + 1, n_tiles - 1), 0, 0), memory_space=pltpu.SMEM),
                  pl.BlockSpec((8, tm), lambda i: (i, 0)),
                  pl.BlockSpec((8, tm), lambda i: (i, 0)),
                  pl.BlockSpec((tm, d), lambda i: (i, 0)),
                  pl.BlockSpec(mod.shape, lambda i: (0, 0)),
                  pl.BlockSpec(memory_space=pl.ANY)],
        out_specs=pl.BlockSpec((tm, d), lambda i: (i, 0)),
        out_shape=jax.ShapeDtypeStruct((n, d), F32),
        scratch_shapes=[pltpu.VMEM((2, tm * TOP_K * pitch, LANES), F32), pltpu.SemaphoreType.DMA((2,))],
        compiler_params=_cparams("arbitrary"),
        name="moe_combine",
    )(sites, sites, lpos, gates, hn, mod, y_rows)


def _moe(f, hn, mod, lpos, gates, ne_t, cb_t, counts, w_gu, b_gu, w_down, b_down, *, tm, row_fn, layer):
    n, _ = f.shape
    n_blocks = -(-n * TOP_K // tm) + N_EXPERTS
    cnt = counts[:, 0].astype(I32)
    padded = (cnt + tm - 1) // tm * tm
    pad_end = jnp.cumsum(padded)
    pad_start = pad_end - padded
    ne = ne_t[:, :, 0].astype(I32)
    cb = cb_t[:, :, 0].astype(I32)
    sites = _site_table(jnp.cumsum(ne, axis=1) - ne, pad_start[None] + cb, ne, tm=tm, pitch=f.shape[1] // LANES)
    n_valid = (pad_end[-1] // tm).astype(I32)
    ztab = jnp.concatenate([pad_start + cnt, padded - cnt, 2 * n_valid[None]])
    xs = _dispatch_call(ztab, sites, f, lpos, n_blocks * tm, tm=tm)
    y_rows = _expert_call(pad_start // tm, padded // tm, xs, w_gu, b_gu, w_down, b_down, tm=tm, layer=layer)
    return _combine_call(sites, lpos, gates, hn, mod, y_rows, tm=tm, row_fn=row_fn)


def _att_in_kernel(h_ref, mod_ref, gain_ref, w_ref, qg_gain, kg_gain, dq_gain, dk_gain, bd_ref,
                   cg_ref, sg_ref, cd_ref, sd_ref,
                   qg_ref, kg_ref, vg_ref, qd_ref, kd_ref, vd_ref, *, d, tpb, nct, nb):
    r = _unified_row(pl.program_id(0), tpb, nct, nb)
    a = _norm_mod(h_ref[...], gain_ref[...], mod_ref[pl.ds(r, 1), 0:d], mod_ref[pl.ds(r, 1), d:2 * d])
    p = _dot(a.astype(BF16), w_ref[...])
    cg, sg = cg_ref[...], sg_ref[...]

    def head_norm_rope(x, gain):
        y = x * lax.rsqrt(jnp.mean(x * x, axis=-1, keepdims=True) + NORM_EPS) * gain
        return _rope(y, cg, sg, GQA_HD).astype(BF16)

    def group_norm_rope(x, gain):
        ss = sum(_dot(piece, bd_ref[...]) for piece in _split_bf16(x * x, 2))
        y = x * lax.rsqrt(ss * (1.0 / DIFF_HD) + NORM_EPS) * gain
        return _rope(y, cd_ref[...], sd_ref[...], DIFF_HD).astype(BF16)

    c0 = 0
    for hh in range(GQA_HEADS):
        qg_ref[:, hh * GQA_HD:(hh + 1) * GQA_HD] = head_norm_rope(p[:, c0:c0 + GQA_HD], qg_gain[...])
        c0 += GQA_HD
    for hh in range(GQA_KV_HEADS):
        kg_ref[:, hh * GQA_HD:(hh + 1) * GQA_HD] = head_norm_rope(p[:, c0:c0 + GQA_HD], kg_gain[...])
        c0 += GQA_HD
    wv = GQA_KV_HEADS * GQA_HD
    vg_ref[...] = p[:, c0:c0 + wv].astype(BF16)
    c0 += wv
    qd_ref[...] = group_norm_rope(p[:, c0:c0 + HALF], dq_gain[...])
    c0 += HALF
    kd_ref[...] = group_norm_rope(p[:, c0:c0 + HALF], dk_gain[...])
    c0 += HALF
    vd_ref[...] = p[:, c0:c0 + HALF].astype(BF16)


def _att_in_call(h, mod, gain, w, qg_gain, kg_gain, dq_gain, dk_gain, bd, cg, sg, cd, sd, *, tm, tpb, nct, nb):
    nt, d = h.shape
    const = lambda a: pl.BlockSpec(a.shape, lambda i: (0,) * a.ndim)
    tab = lambda w_: pl.BlockSpec((tm, w_), lambda i: (lax.rem(i, tpb), 0))
    out = lambda w_: pl.BlockSpec((tm, w_), lambda i: (i, 0))
    shp = lambda w_: jax.ShapeDtypeStruct((nt, w_), BF16)
    wkv = GQA_KV_HEADS * GQA_HD
    return pl.pallas_call(
        functools.partial(_att_in_kernel, d=d, tpb=tpb, nct=nct, nb=nb),
        grid=(nt // tm,),
        in_specs=[pl.BlockSpec((tm, d), lambda i: (i, 0)), const(mod), const(gain), const(w),
                  const(qg_gain), const(kg_gain), const(dq_gain), const(dk_gain), const(bd),
                  tab(GQA_HD), tab(GQA_HD), tab(HALF), tab(HALF)],
        out_specs=[out(HALF), out(wkv), out(wkv), out(HALF), out(HALF), out(HALF)],
        out_shape=[shp(HALF), shp(wkv), shp(wkv), shp(HALF), shp(HALF), shp(HALF)],
        compiler_params=_cparams("arbitrary"),
        name="att_in_proj",
    )(h, mod, gain, w, qg_gain, kg_gain, dq_gain, dk_gain, bd, cg, sg, cd, sd)


def _softmax_parts(q, k):
    s = lax.dot_general(q, k, _NT, preferred_element_type=F32)
    e = jnp.exp2(s - jnp.max(s, axis=-1, keepdims=True))
    return e, 1.0 / jnp.sum(e, axis=-1, keepdims=True)


def _gqa_kernel(q_ref, k_ref, v_ref, o_ref, *, group):
    for h in range(GQA_HEADS):
        sl = slice(h * GQA_HD, (h + 1) * GQA_HD)
        kv = slice(h // group * GQA_HD, (h // group + 1) * GQA_HD)
        e, inv = _softmax_parts(q_ref[:, sl], k_ref[:, kv])
        o_ref[:, sl] = (_dot(e.astype(BF16), v_ref[:, kv]) * inv).astype(BF16)


def _gqa_call(qg, kg, vg, *, nb, tq, tot, nqt, lat0):
    group = GQA_HEADS // GQA_KV_HEADS
    upb = tot // tq
    return pl.pallas_call(
        functools.partial(_gqa_kernel, group=group),
        grid=(nb, nqt),
        in_specs=[pl.BlockSpec((tq, GQA_HEADS * GQA_HD), lambda b, q: (b * upb + lat0 + q, 0)),
                  pl.BlockSpec((tot, GQA_KV_HEADS * GQA_HD), lambda b, q: (b, 0)),
                  pl.BlockSpec((tot, GQA_KV_HEADS * GQA_HD), lambda b, q: (b, 0))],
        out_specs=pl.BlockSpec((tq, GQA_HEADS * GQA_HD), lambda b, q: (b * nqt + q, 0)),
        out_shape=jax.ShapeDtypeStruct((nb * nqt * tq, HALF), BF16),
        compiler_params=_cparams("arbitrary", "arbitrary"),
        name="gqa_attention",
    )(qg, kg, vg)


def _diff_kernel(q_ref, k_ref, v_ref, lam_ref, gain_ref, o_ref, *, lam_init):
    lp = lam_ref[...]
    lam = (jnp.exp(jnp.sum(lp[0:1] * lp[1:2], axis=-1, keepdims=True))
           - jnp.exp(jnp.sum(lp[2:3] * lp[3:4], axis=-1, keepdims=True)) + lam_init)
    hd2 = 2 * DIFF_HD
    for j in range(q_ref.shape[1] // hd2):
        sl = slice(j * hd2, (j + 1) * hd2)
        q = q_ref[:, sl]
        k = k_ref[:, sl]
        lane = lax.broadcasted_iota(I32, q.shape, 1)
        zero = jnp.zeros_like(q)
        e1, inv1 = _softmax_parts(jnp.where(lane < DIFF_HD, q, zero), k)
        e2, inv2 = _softmax_parts(jnp.where(lane >= DIFF_HD, q, zero), k)
        v = v_ref[:, sl]
        o = _dot(e1.astype(BF16), v) * inv1 - _dot(e2.astype(BF16), v) * (lam * inv2)
        y = o * lax.rsqrt(jnp.mean(o * o, axis=-1, keepdims=True) + NORM_EPS) * gain_ref[...]
        o_ref[:, sl] = (y * (1.0 - lam_init)).astype(BF16)


def _diff_call(qd, kd, vd, lam_params, gain, *, nb, tq, tot, nqt, lat0, lam_init):
    upb = tot // tq
    hd2 = DIFF_HEADS * (2 * DIFF_HD)
    return pl.pallas_call(
        functools.partial(_diff_kernel, lam_init=lam_init),
        grid=(nb, HALF // hd2, nqt),
        in_specs=[pl.BlockSpec((tq, hd2), lambda b, h, q: (b * upb + lat0 + q, h)),
                  pl.BlockSpec((tot, hd2), lambda b, h, q: (b, h)),
                  pl.BlockSpec((tot, hd2), lambda b, h, q: (b, h)),
                  pl.BlockSpec(lam_params.shape, lambda b, h, q: (0, 0)),
                  pl.BlockSpec(gain.shape, lambda b, h, q: (0, 0))],
        out_specs=pl.BlockSpec((tq, hd2), lambda b, h, q: (b * nqt + q, h)),
        out_shape=jax.ShapeDtypeStruct((nb * nqt * tq, HALF), BF16),
        compiler_params=_cparams("arbitrary", "arbitrary", "arbitrary"),
        name="diff_attention",
    )(qd, kd, vd, lam_params, gain)


def _deinterleave(hd):
    return np.concatenate([np.arange(0, hd, 2), np.arange(1, hd, 2)])


def _head_perm(n_heads, hd):
    return np.concatenate([h * hd + _deinterleave(hd) for h in range(n_heads)])


def _rope_tables(n_lat, n_ctx, hd, reps):
    t = np.arange(n_lat)
    row, col = (t // GRID_W).astype(np.float64), (t % GRID_W).astype(np.float64)
    axis_dim = hd // 2
    inv_freq = ROPE_THETA ** (-np.arange(0, axis_dim, 2, dtype=np.float64) / axis_dim)
    ang = np.concatenate([row[:, None] * inv_freq, col[:, None] * inv_freq], axis=-1)
    cos = np.concatenate([np.ones((n_ctx, hd // 2)), np.cos(ang)], axis=0)
    sin = np.concatenate([np.zeros((n_ctx, hd // 2)), np.sin(ang)], axis=0)
    c = np.tile(np.concatenate([cos, cos], axis=1), (1, reps))
    s = np.tile(np.concatenate([-sin, sin], axis=1), (1, reps))
    return jnp.asarray(c, F32), jnp.asarray(s, F32)


def _retention_tables(n):
    gam = 1.0 - 2.0 ** (-5.0 - np.arange(RET_HEADS, dtype=np.float64))
    gams = (gam, gam[::-1])
    c = np.arange(n, dtype=np.float64)
    diff = c[:, None] - c[None, :]
    dmat = np.zeros((2, RET_HEADS, n, n))
    qs = np.zeros((2, n, RET_HEADS * RET_DK))
    ks = np.zeros((2, n, RET_HEADS * RET_DK))
    gpow = np.zeros((2, 8, RET_HEADS * RET_DK))
    for d in range(2):
        for h in range(RET_HEADS):
            g = gams[d][h]
            sl = slice(h * RET_DK, (h + 1) * RET_DK)
            if d == 0:
                dmat[d, h] = np.where(diff >= 0, g ** np.maximum(diff, 0), 0.0)
                qs[d, :, sl] = (g ** (c + 1))[:, None]
                ks[d, :, sl] = (g ** (n - 1 - c))[:, None]
            else:
                dmat[d, h] = np.where(diff <= 0, g ** np.maximum(-diff, 0), 0.0)
                qs[d, :, sl] = (g ** (n - c))[:, None]
                ks[d, :, sl] = (g ** c)[:, None]
            gpow[d, :, sl] = g ** n
    return tuple(jnp.asarray(a, F32) for a in (dmat, qs, ks, gpow))


def _plan(nb, n_lat, n_ctx):
    tm = 256 if (n_ctx % 256 == 0 and n_lat % 256 == 0) else 128
    assert n_ctx % tm == 0 and n_lat % tm == 0 and tm % CHUNK == 0 and n_lat % GRID_W == 0
    tq = tm
    return tm, tq


def kernel(x, c, ctx, c_ctx, norm_mix, norm_ffn, w_ada, b_ada, rec_w_in, rec_lb_logits, rec_w_out, rec_hg_gain,
           rec_ret_gain, att_w_in, att_w_out, att_q_gain, att_k_gain, diff_q_gain, diff_k_gain, diff_lambda,
           diff_gain, w_router, b_router, w_gu, b_gu, w_down, b_down):
    nb, n_lat, d = x.shape
    n_ctx = ctx.shape[1]
    assert w_ada.shape[0] == 2, "two layers: recurrent mixer then attention mixer"
    tm, tq = _plan(nb, n_lat, n_ctx)
    tot = n_ctx + n_lat
    tpb, nct = tot // tm, n_ctx // tm
    nt = nb * tot
    nql = n_lat // tm

    unified_row = functools.partial(_unified_row, tpb=tpb, nct=nct, nb=nb)
    latent_row = lambda i: lax.div(i, nql)
    latent_block = lambda i: lax.div(i, nql) * tpb + nct + lax.rem(i, nql)

    n_cond = -(-(nb + 1) // 8) * 8
    cond = jnp.concatenate([c, c_ctx[None], jnp.zeros((n_cond - nb - 1, d), F32)], axis=0)
    mod = _ada_call(cond, w_ada, b_ada)

    wr = jnp.pad(jnp.swapaxes(w_router, 1, 2), ((0, 0), (0, E_PAD - N_EXPERTS), (0, 0)))
    br = b_router[:, :, None]

    is_context = lambda i: lax.rem(i, tpb) < nct
    stream0 = [(x.reshape(nb * n_lat, d), lambda i: lax.div(i, tpb) * nql + jnp.maximum(lax.rem(i, tpb) - nct, 0)),
               (ctx.reshape(nb * n_ctx, d), lambda i: lax.div(i, tpb) * nct + jnp.minimum(lax.rem(i, tpb), nct - 1))]

    lb = jnp.cumsum(jax.nn.softmax(rec_lb_logits.astype(F32), axis=0), axis=0)[0]
    w_in = rec_w_in[0]
    c_rq = HG_HEADS * HG_DK * 3 + HG_HEADS * HG_DV * 2
    nrk = RET_HEADS * RET_DK
    perm = _head_perm(RET_HEADS, RET_DK)
    w_in = jnp.concatenate([w_in[:, :c_rq], w_in[:, c_rq + perm], w_in[:, c_rq + nrk + perm] * (RET_DK ** -0.5),
                            w_in[:, c_rq + 2 * nrk:]], axis=1).astype(BF16)
    cs_r, sn_r = _rope_tables(n_lat, n_ctx, RET_DK, 2 * RET_HEADS)
    dmat, qs, ks, g64 = _retention_tables(tm)
    p0 = _inproj_call(stream0, mod[0], norm_mix[0][None], w_in, cs_r, sn_r, tm=tm, tpb=tpb, nct=nct, nb=nb)
    ohf, orf, ohb, orb = _rec_call(p0, lb, dmat, qs, ks, g64, nb=nb, tc=tm, tpb=tpb, nct=nct)
    mixer = [(ohf, 0), (ohb, 0), (orf, 0), (orb, 0), (p0, 4), (p0, 7),
             (rec_hg_gain[0][None], None), (rec_ret_gain[0][None], None)]
    hn, f, *routing = _post_call(
        mixer, stream0, mod[0], rec_w_out[0].astype(BF16), norm_ffn[0][None], wr[0], br[0],
        tm=tm, n_tiles=nt // tm, row_fn=unified_row, is_context=is_context)
    h = _moe(f, hn, mod[0], *routing, w_gu, b_gu, w_down, b_down, tm=tm, row_fn=unified_row, layer=0)

    pg, pd = _head_perm(GQA_HEADS, GQA_HD), _head_perm(2 * DIFF_HEADS, DIFF_HD)
    w_in = att_w_in[0]
    o_gk = GQA_HEADS * GQA_HD
    o_gv = o_gk + GQA_KV_HEADS * GQA_HD
    o_dq = o_gv + GQA_KV_HEADS * GQA_HD
    o_dk = o_dq + HALF
    o_dv = o_dk + HALF
    w_in = jnp.concatenate([w_in[:, pg], w_in[:, o_gk + pg[:GQA_KV_HEADS * GQA_HD]], w_in[:, o_gv:o_dq],
                            w_in[:, o_dq + pd], w_in[:, o_dk + pd], w_in[:, o_dv:]], axis=1).astype(BF16)
    dg, dd = _deinterleave(GQA_HD), _deinterleave(DIFF_HD)
    cg, sg = _rope_tables(n_lat, n_ctx, GQA_HD, 1)
    cd, sd = _rope_tables(n_lat, n_ctx, DIFF_HD, 2 * DIFF_HEADS)
    bd = jnp.asarray(np.kron(np.eye(2 * DIFF_HEADS), np.ones((DIFF_HD, DIFF_HD))), BF16)
    log2e = math.log2(math.e)
    qg, kg, vg, qd, kd, vd = _att_in_call(
        h, mod[1], norm_mix[1][None], w_in,
        (att_q_gain[0][dg] * (GQA_HD ** -0.5 * log2e))[None], att_k_gain[0][dg][None],
        jnp.tile(diff_q_gain[0][dd] * (DIFF_HD ** -0.5 * log2e), 2 * DIFF_HEADS)[None],
        jnp.tile(diff_k_gain[0][dd], 2 * DIFF_HEADS)[None],
        bd, cg, sg, cd, sd, tm=tm, tpb=tpb, nct=nct, nb=nb)
    nqt = n_lat // tq
    yg = _gqa_call(qg, kg, vg, nb=nb, tq=tq, tot=tot, nqt=nqt, lat0=n_ctx // tq)
    lam_init = 0.8 - 0.6 * math.exp(-0.3 * 1)
    yd = _diff_call(qd, kd, vd, diff_lambda[0], diff_gain[0][None], nb=nb, tq=tq, tot=tot, nqt=nqt,
                    lat0=n_ctx // tq, lam_init=lam_init)
    hn, f, *routing = _post_call(
        [(yg, 0), (yd, 0)], [(h, latent_block)], mod[1], att_w_out[0].astype(BF16), norm_ffn[1][None], wr[1], br[1],
        tm=tm, n_tiles=nb * nql, row_fn=latent_row)
    out = _moe(f, hn, mod[1], *routing, w_gu, b_gu, w_down, b_down, tm=tm, row_fn=latent_row, layer=1)
    return out.reshape(nb, n_lat, d)
```

```python
import functools
import math

import numpy as np
import jax
import jax.numpy as jnp
from jax import lax
from jax.experimental import pallas as pl
from jax.experimental.pallas import tpu as pltpu

F32 = jnp.float32
BF16 = jnp.bfloat16
I32 = jnp.int32

GRID_W = 64
HG_HEADS, HG_DK, HG_DV = 4, 128, 128
RET_HEADS, RET_DK, RET_DV = 4, 64, 128
GQA_HEADS, GQA_KV_HEADS, GQA_HD = 4, 2, 128
DIFF_HEADS, DIFF_HD = 4, 64
N_EXPERTS, TOP_K = 32, 4
SWIGLU_LIMIT, SWIGLU_ALPHA = 7.0, 1.702
CHUNK = 64
ROPE_THETA = 10000.0
NORM_EPS = 1e-6

LANES = 128
E_PAD = LANES
V7X_VMEM_LIMIT = 56 * 1024 * 1024
HALF = 512

_NT = (((1,), (1,)), ((), ()))
_TN = (((0,), (0,)), ((), ()))


def _cparams(*sem):
    return pltpu.CompilerParams(dimension_semantics=sem, vmem_limit_bytes=V7X_VMEM_LIMIT)


def _silu(x):
    return x * jax.nn.sigmoid(x)


def _dot(a, b):
    return jnp.dot(a, b, preferred_element_type=F32)


def _split_bf16(x, terms):
    out = []
    for _ in range(terms):
        p = x.astype(BF16)
        out.append(p)
        x = x - p.astype(F32)
    return out


def _norm_mod(x, gain, shift, scale):
    y = x * lax.rsqrt(jnp.mean(x * x, axis=-1, keepdims=True) + NORM_EPS) * gain
    return y * (1.0 + scale) + shift


def _unified_row(i, tpb, nct, nb):
    return jnp.where(lax.rem(i, tpb) < nct, nb, lax.div(i, tpb))


def _ada_kernel(cond_ref, w_ref, b_ref, o_ref):
    s = _silu(cond_ref[...]).astype(BF16)
    o_ref[0] = _dot(s, w_ref[0].astype(BF16)) + b_ref[0]


def _ada_call(cond, w_ada, b_ada):
    n_layers, d, d6 = w_ada.shape
    r = cond.shape[0]
    tn = d6 // 4
    return pl.pallas_call(
        _ada_kernel,
        grid=(n_layers, d6 // tn),
        in_specs=[pl.BlockSpec((r, d), lambda l, n: (0, 0)),
                  pl.BlockSpec((1, d, tn), lambda l, n: (l, 0, n)),
                  pl.BlockSpec((1, 1, tn), lambda l, n: (l, 0, n))],
        out_specs=pl.BlockSpec((1, r, tn), lambda l, n: (l, 0, n)),
        out_shape=jax.ShapeDtypeStruct((n_layers, r, d6), F32),
        compiler_params=_cparams("arbitrary", "arbitrary"),
        name="ada_modulation",
    )(cond, w_ada, b_ada.reshape(n_layers, 1, d6))


REC_COL_Q = 0
REC_COL_RQK = 5


def _inproj_kernel(x_ref, c_ref, mod_ref, gain_ref, w_ref, cs_ref, sn_ref, o_ref, *, d, tpb, nct, nb):
    i = pl.program_id(0)
    r = _unified_row(i, tpb, nct, nb)
    h = _stream_rows((x_ref, c_ref), lambda t: lax.rem(t, tpb) < nct, i, slice(None))
    a = _norm_mod(h, gain_ref[...], mod_ref[pl.ds(r, 1), 0:d], mod_ref[pl.ds(r, 1), d:2 * d])
    p = _dot(a.astype(BF16), w_ref[...])
    for blk in range(p.shape[1] // HALF):
        sl = slice(blk * HALF, (blk + 1) * HALF)
        x = p[:, sl]
        if blk == REC_COL_Q:
            x = _silu(x) * (HG_DK ** -0.5)
        elif blk == REC_COL_RQK:
            x = _rope(x, cs_ref[...], sn_ref[...], RET_DK)
        o_ref[:, sl] = x


def _inproj_call(stream, mod, gain, w, cs, sn, *, tm, tpb, nct, nb):
    d = stream[0][0].shape[1]
    nt = nb * tpb * tm
    n_out = w.shape[1]
    table = pl.BlockSpec((tm, HALF), lambda i: (lax.rem(i, tpb), 0))
    return pl.pallas_call(
        functools.partial(_inproj_kernel, d=d, tpb=tpb, nct=nct, nb=nb),
        grid=(nt // tm,),
        in_specs=[pl.BlockSpec((tm, d), lambda i, blk=blk: (blk(i), 0)) for _, blk in stream]
                 + [pl.BlockSpec(mod.shape, lambda i: (0, 0)),
                  pl.BlockSpec((1, d), lambda i: (0, 0)),
                  pl.BlockSpec((d, n_out), lambda i: (0, 0)), table, table],
        out_specs=pl.BlockSpec((tm, n_out), lambda i: (i, 0)),
        out_shape=jax.ShapeDtypeStruct((nt, n_out), F32),
        compiler_params=_cparams("arbitrary"),
        name="rec_in_proj",
    )(*[a for a, _ in stream], mod, gain, w, cs, sn)


def _swap_halves(x, group):
    if group == LANES:
        return pltpu.roll(x, LANES // 2, 1)
    lane = lax.broadcasted_iota(I32, x.shape, 1)
    half = group // 2
    return jnp.where(lax.rem(lane, group) < half, pltpu.roll(x, LANES - half, 1), pltpu.roll(x, half, 1))


def _rope(x, cos, sin, group):
    parts = [_swap_halves(x[:, s:s + LANES], group) for s in range(0, x.shape[1], LANES)]
    rot = parts[0] if len(parts) == 1 else jnp.concatenate(parts, axis=1)
    return x * cos + rot * sin


def _hgrn_prepare(d, hq, hf, hi, lb_ref, tri_blocks):
    tc = hf.shape[0]
    lbv = lb_ref[d:d + 1, :]
    f = lbv + (1.0 - lbv) * jax.nn.sigmoid(hf[...])
    logf = jnp.log(f)
    kk = 1.0 - f
    b = sum(_dot(tri_blocks, p) for p in _split_bf16(logf, 3))
    last = CHUNK - 1 if d == 0 else 0
    tots = [b[c * CHUNK + last:c * CHUNK + last + 1, :] for c in range(tc // CHUNK)]
    tot_rows = jnp.concatenate([jnp.broadcast_to(t, (CHUNK, t.shape[1])) for t in tots], axis=0)
    q_in = (hq[...] * jnp.exp(b)).astype(BF16)
    k_in = (kk * jnp.exp(-b)).astype(BF16)
    k_st = (kk * jnp.exp(tot_rows - b)).astype(BF16)
    return q_in, k_in, k_st, [jnp.exp(t) for t in tots], hi[...].astype(BF16)


def _hgrn_chunk(d, c, prepared, ohg, st_hg, tri_mask):
    q_all, k_all, kst_all, decs, v_all = prepared
    rows = slice(c * CHUNK, (c + 1) * CHUNK)
    q_in, k_in, k_st, v, dec = q_all[rows], k_all[rows], kst_all[rows], v_all[rows], decs[c]
    for h in range(HG_HEADS):
        sl = slice(h * HG_DK, (h + 1) * HG_DK)
        att = jnp.where(tri_mask, lax.dot_general(q_in[:, sl], k_in[:, sl], _NT, preferred_element_type=F32), 0.0)
        s_t = st_hg[d, h]
        o = _dot(att.astype(BF16), v[:, sl]) + lax.dot_general(
            q_in[:, sl], s_t.astype(BF16), _NT, preferred_element_type=F32)
        ohg[rows, sl] = o
        st_hg[d, h] = s_t * dec[:, sl] + lax.dot_general(v[:, sl], k_st[:, sl], _TN, preferred_element_type=F32)


def _retention_tile(d, rqk, rv, dmat_ref, qs_ref, ks_ref, gpow_ref, oret, st_ret):
    xr = rqk[...]
    nq = RET_HEADS * RET_DK
    q_r = xr[:, 0:nq]
    k_r = xr[:, nq:2 * nq]
    q_b = q_r.astype(BF16)
    k_b = k_r.astype(BF16)
    q_sc = (q_r * qs_ref[d]).astype(BF16)
    k_sc = (k_r * ks_ref[d]).astype(BF16)
    vv = rv[...].astype(BF16)
    for h in range(RET_HEADS):
        sk = slice(h * RET_DK, (h + 1) * RET_DK)
        sv = slice(h * RET_DV, (h + 1) * RET_DV)
        att = lax.dot_general(q_b[:, sk], k_b[:, sk], _NT, preferred_element_type=F32) * dmat_ref[d, h]
        s_t = st_ret[d, h]
        o = _dot(att.astype(BF16), vv[:, sv]) + lax.dot_general(
            q_sc[:, sk], s_t.astype(BF16), _NT, preferred_element_type=F32)
        oret[:, sv] = o
        st_ret[d, h] = s_t * gpow_ref[d, 0:1, sk] + lax.dot_general(
            vv[:, sv], k_sc[:, sk], _TN, preferred_element_type=F32)


def _rec_kernel(hq_f, hf_f, hi_f, rqk_f, rv_f,
                hq_b, hf_b, hi_b, rqk_b, rv_b,
                lb_ref, dmat_ref, qs_ref, ks_ref, g64_ref,
                ohg_f, oret_f, ohg_b, oret_b, st_hg, st_ret, *, tc):
    @pl.when(pl.program_id(1) == 0)
    def _():
        st_hg[...] = jnp.zeros_like(st_hg)
        st_ret[...] = jnp.zeros_like(st_ret)

    nch = tc // CHUNK
    row = lax.broadcasted_iota(I32, (CHUNK, CHUNK), 0)
    col = lax.broadcasted_iota(I32, (CHUNK, CHUNK), 1)
    mask_f = col <= row
    mask_b = col >= row
    trow = lax.broadcasted_iota(I32, (tc, tc), 0)
    tcol = lax.broadcasted_iota(I32, (tc, tc), 1)
    same_chunk = lax.div(trow, CHUNK) == lax.div(tcol, CHUNK)
    tri_f = jnp.logical_and(same_chunk, tcol <= trow).astype(F32).astype(BF16)
    tri_b = jnp.logical_and(same_chunk, tcol >= trow).astype(F32).astype(BF16)

    ret_tables = (dmat_ref, qs_ref, ks_ref, g64_ref)
    _retention_tile(0, rqk_f, rv_f, *ret_tables, oret_f, st_ret)
    _retention_tile(1, rqk_b, rv_b, *ret_tables, oret_b, st_ret)
    prep_f = _hgrn_prepare(0, hq_f, hf_f, hi_f, lb_ref, tri_f)
    prep_b = _hgrn_prepare(1, hq_b, hf_b, hi_b, lb_ref, tri_b)
    for ci in range(nch):
        _hgrn_chunk(0, ci, prep_f, ohg_f, st_hg, mask_f)
        _hgrn_chunk(1, nch - 1 - ci, prep_b, ohg_b, st_hg, mask_b)


def _rec_call(p0, lb, dmat, qs, ks, g64, *, nb, tc, tpb, nct):
    nt = p0.shape[0]

    def fwd(b, j):
        return j

    def bwd(b, j):
        return jnp.where(j < nct, nct - 1 - j, tpb + nct - 1 - j)

    def pspec(col, pos):
        return pl.BlockSpec((tc, HALF), lambda b, j: (b * tpb + pos(b, j), col))

    def whole(a):
        return pl.BlockSpec(a.shape, lambda b, j: (0,) * a.ndim)

    in_specs = ([pspec(REC_COL_Q, fwd), pspec(1, fwd), pspec(3, fwd), pspec(REC_COL_RQK, fwd), pspec(6, fwd)]
                + [pspec(REC_COL_Q, bwd), pspec(2, bwd), pspec(3, bwd), pspec(REC_COL_RQK, bwd), pspec(6, bwd)]
                + [whole(a) for a in (lb, dmat, qs, ks, g64)])
    out_f = pl.BlockSpec((tc, HALF), lambda b, j: (b * tpb + fwd(b, j), 0))
    out_b = pl.BlockSpec((tc, HALF), lambda b, j: (b * tpb + bwd(b, j), 0))
    o_shape = jax.ShapeDtypeStruct((nt, HALF), F32)
    return pl.pallas_call(
        functools.partial(_rec_kernel, tc=tc),
        grid=(nb, tpb),
        in_specs=in_specs,
        out_specs=[out_f, out_f, out_b, out_b],
        out_shape=[o_shape] * 4,
        scratch_shapes=[pltpu.VMEM((2, HG_HEADS, HG_DV, HG_DK), F32),
                        pltpu.VMEM((2, RET_HEADS, RET_DV, RET_DK), F32)],
        compiler_params=_cparams("arbitrary", "arbitrary"),
        name="rec_scan",
    )(*([p0] * 10), lb, dmat, qs, ks, g64)


def _rec_mixer_rows(rs, ohf, ohb, orf, orb, hg_ref, rg_ref, hgain, rgain):
    oh = ohf[rs, :] + ohb[rs, :]
    orr = orf[rs, :] + orb[rs, :]
    hg = hg_ref[rs, :]
    rg = rg_ref[rs, :]
    y_h, y_r = [], []
    for h in range(HG_HEADS):
        sl = slice(h * LANES, (h + 1) * LANES)
        o = oh[:, sl]
        y = o * lax.rsqrt(jnp.mean(o * o, axis=-1, keepdims=True) + NORM_EPS) * hgain[...]
        y_h.append((y * _silu(hg[:, sl])).astype(BF16))
        o = orr[:, sl]
        oc = o - jnp.mean(o, axis=-1, keepdims=True)
        y = oc * lax.rsqrt(jnp.mean(oc * oc, axis=-1, keepdims=True) + NORM_EPS) * rgain[...]
        y_r.append((y * _silu(rg[:, sl])).astype(BF16))
    return jnp.concatenate(y_h + y_r, axis=1)


def _stream_rows(h_refs, is_context, i, rs):
    if len(h_refs) == 1:
        return h_refs[0][rs, :]
    return jnp.where(is_context(i), h_refs[1][rs, :], h_refs[0][rs, :])


def _post_kernel(*refs, d, tm, row_fn, n_mixer_refs, n_h_refs, is_context):
    mixer_refs = refs[:n_mixer_refs]
    h_refs = refs[n_mixer_refs:n_mixer_refs + n_h_refs]
    (mod_ref, wout_ref, gain_ref, wr_ref, br_ref,
     hn_ref, f_ref, lpos_ref, gate_ref, ne_ref, cb_ref, cnt_ref, cnt_sc) = refs[n_mixer_refs + n_h_refs:]
    if n_mixer_refs == 2:
        mixer_rows = lambda rs: jnp.concatenate([mixer_refs[0][rs, :], mixer_refs[1][rs, :]], axis=1)
    else:
        mixer_rows = lambda rs: _rec_mixer_rows(rs, *mixer_refs)
    i = pl.program_id(0)

    @pl.when(i == 0)
    def _():
        cnt_sc[...] = jnp.zeros_like(cnt_sc)

    r = row_fn(i)
    w_hi, w_lo = _split_bf16(wr_ref[...], 2)
    nt_dot = lambda a, b: lax.dot_general(a, b, _NT, preferred_element_type=F32)
    n_part = 2 if tm % (2 * LANES) == 0 else 1
    logit_parts = []
    for part in range(n_part):
        rs = slice(part * tm // n_part, (part + 1) * tm // n_part)
        hn = (_stream_rows(h_refs, is_context, i, rs)
              + mod_ref[pl.ds(r, 1), 2 * d:3 * d] * _dot(mixer_rows(rs), wout_ref[...]))
        hn_ref[rs, :] = hn
        f = _norm_mod(hn, gain_ref[...], mod_ref[pl.ds(r, 1), 3 * d:4 * d], mod_ref[pl.ds(r, 1), 4 * d:5 * d])
        f_hi, f_lo = _split_bf16(f, 2)
        f_ref[rs, :] = f_hi
        logit_parts.append((nt_dot(w_hi, f_hi) + nt_dot(w_lo, f_hi) + nt_dot(w_hi, f_lo))[0:N_EXPERTS])
    logits = (logit_parts[0] if n_part == 1 else jnp.concatenate(logit_parts, axis=1)) + br_ref[...]
    row = lax.broadcasted_iota(I32, (N_EXPERTS, tm), 0)
    vals, idxs = [], []
    work = logits
    for _ in range(TOP_K):
        m = jnp.max(work, axis=0, keepdims=True)
        sel = jnp.min(jnp.where(work == m, row, N_EXPERTS), axis=0, keepdims=True)
        vals.append(m)
        idxs.append(sel)
        work = jnp.where(row == sel, -jnp.inf, work)
    exps = [jnp.exp(v - vals[0]) for v in vals]
    inv = 1.0 / sum(exps)
    onehot = sum((row == s).astype(F32) for s in idxs)
    n_e = jnp.broadcast_to(jnp.sum(onehot, axis=1, keepdims=True), (N_EXPERTS, LANES))
    rr = lax.broadcasted_iota(I32, (tm, tm), 0)
    cc = lax.broadcasted_iota(I32, (tm, tm), 1)
    within = _dot(onehot.astype(BF16), (rr < cc).astype(F32).astype(BF16))
    ea = lax.broadcasted_iota(I32, (N_EXPERTS, N_EXPERTS), 0)
    eb = lax.broadcasted_iota(I32, (N_EXPERTS, N_EXPERTS), 1)
    before = _dot((eb < ea).astype(F32).astype(BF16), n_e.astype(BF16))[:, 0:1]
    base = within + before
    sub = lax.broadcasted_iota(I32, (8, tm), 0)
    lpos_slab = jnp.zeros((8, tm), F32)
    gate_slab = jnp.zeros((8, tm), F32)
    for k in range(TOP_K):
        gate_slab = jnp.where(sub == k, exps[k] * inv, gate_slab)
        lp = jnp.sum(jnp.where(row == idxs[k], base, 0.0), axis=0, keepdims=True)
        lpos_slab = jnp.where(sub == k, lp, lpos_slab)
    lpos_ref[...] = lpos_slab.astype(I32)
    gate_ref[...] = gate_slab
    ne_ref[0] = n_e
    cb_ref[0] = cnt_sc[...]
    cnt_sc[...] = cnt_sc[...] + n_e
    cnt_ref[...] = cnt_sc[...]


def _post_call(mixer, stream, mod, wout, gain, wr, br, *, tm, n_tiles, row_fn, is_context=None):
    d = stream[0][0].shape[1]
    n = n_tiles * tm
    tile = lambda w: pl.BlockSpec((tm, w), lambda i: (i, 0))
    const = lambda a: pl.BlockSpec(a.shape, lambda i: (0,) * a.ndim)
    routing = pl.BlockSpec((8, tm), lambda i: (i, 0))
    counts = pl.BlockSpec((1, N_EXPERTS, LANES), lambda i: (i, 0, 0))
    mixer_specs = [const(a) if col is None else pl.BlockSpec((tm, HALF), lambda i, col=col: (i, col))
                   for a, col in mixer]
    stream_specs = [pl.BlockSpec((tm, d), lambda i, blk=blk: (blk(i), 0)) for _, blk in stream]
    return pl.pallas_call(
        functools.partial(_post_kernel, d=d, tm=tm, row_fn=row_fn, n_mixer_refs=len(mixer),
                          n_h_refs=len(stream), is_context=is_context),
        grid=(n_tiles,),
        in_specs=mixer_specs + stream_specs + [const(mod), const(wout), const(gain), const(wr), const(br)],
        out_specs=[tile(d), tile(d), routing, routing, counts, counts,
                   pl.BlockSpec((N_EXPERTS, LANES), lambda i: (0, 0))],
        out_shape=[jax.ShapeDtypeStruct((n, d), F32), jax.ShapeDtypeStruct((n, d), BF16),
                   jax.ShapeDtypeStruct((n_tiles * 8, tm), I32), jax.ShapeDtypeStruct((n_tiles * 8, tm), F32),
                   jax.ShapeDtypeStruct((n_tiles, N_EXPERTS, LANES), F32),
                   jax.ShapeDtypeStruct((n_tiles, N_EXPERTS, LANES), F32),
                   jax.ShapeDtypeStruct((N_EXPERTS, LANES), F32)],
        scratch_shapes=[pltpu.VMEM((N_EXPERTS, LANES), F32)],
        compiler_params=_cparams("arbitrary"),
        name="post_mixer",
    )(*[a for a, _ in mixer], *[a for a, _ in stream], mod, wout, gain, wr, br)


SITE_TABLE_WIDTH = 1024
_SITE_HEAD = 16
_SITE_CLASS = 2 * N_EXPERTS


def _chunk_sizes(limit):
    return [1 << b for b in range(limit.bit_length() - 1, -1, -1)]


def _token_rows(start, count, pitch):
    return pl.ds(pl.multiple_of(start * pitch, pitch), count * pitch)


def _site_table(local_start, grouped_start, length, *, tm, pitch):
    sizes = jnp.asarray(_chunk_sizes(tm), I32)
    n_tiles, n_cls = length.shape[0], sizes.shape[0]
    assert n_cls <= _SITE_HEAD and _SITE_HEAD + n_cls * _SITE_CLASS <= SITE_TABLE_WIDTH
    n = length[..., None]
    done = jnp.bitwise_and(n, -2 * sizes)
    valid = jnp.bitwise_and(n, sizes) != 0
    loc = (local_start[..., None] + done) * pitch
    grp = (grouped_start[..., None] + done) * pitch
    rank = jnp.cumsum(valid.astype(I32), axis=1) - 1
    slot_of = jnp.logical_and(valid[..., None], rank[..., None] == jnp.arange(N_EXPERTS, dtype=I32))
    pick = lambda a: jnp.sum(jnp.where(slot_of, a[..., None], 0), axis=1)
    pairs = jnp.stack([pick(loc), pick(grp)], axis=-1).reshape(n_tiles, n_cls * _SITE_CLASS)
    counts = jnp.pad(jnp.sum(valid.astype(I32), axis=1), ((0, 0), (0, _SITE_HEAD - n_cls)))
    table = jnp.concatenate([counts, pairs], axis=1)
    return jnp.pad(table, ((0, 0), (0, SITE_TABLE_WIDTH - table.shape[1])))[:, None, :]


def _segment_copies(site_ref, local, slot, grouped, sem, *, tm, pitch, to_grouped):
    for k, size in enumerate(_chunk_sizes(tm)):
        base = _SITE_HEAD + k * _SITE_CLASS

        def body(j, c, base=base, size=size):
            loc0 = site_ref[0, 0, base + 2 * j]
            grp0 = site_ref[0, 0, base + 2 * j + 1]
            loc = local.at[slot, pl.ds(pl.multiple_of(loc0, pitch), size * pitch)]
            grp = grouped.at[pl.ds(pl.multiple_of(grp0, pitch), size * pitch)]
            (pltpu.make_async_copy(loc, grp, sem) if to_grouped else pltpu.make_async_copy(grp, loc, sem)).start()
            return c

        lax.fori_loop(0, site_ref[0, 0, k], body, 0)


def _pad_zero_copies(ztab_ref, zeros, grouped, sem, *, tm, pitch, wait):
    for e in range(N_EXPERTS):
        g0 = ztab_ref[e]
        n = ztab_ref[N_EXPERTS + e]
        for size in _chunk_sizes(tm // 2):
            done = jnp.bitwise_and(n, -2 * size)

            @pl.when(jnp.bitwise_and(n, size) != 0)
            def _():
                cp = pltpu.make_async_copy(zeros.at[pl.ds(0, size * pitch)],
                                           grouped.at[_token_rows(g0 + done, size, pitch)], sem)
                cp.wait() if wait else cp.start()


def _lanes_to_tiles(ref, x):
    rows, pitch = x.shape[0], x.shape[1] // LANES
    for c in range(pitch):
        ref[pl.ds(c, rows, stride=pitch), :] = x[:, c * LANES:(c + 1) * LANES]


def _tiles_to_lanes(ref, pitch):
    rows = ref.shape[0] // pitch
    return jnp.concatenate([ref[pl.ds(c, rows, stride=pitch), :] for c in range(pitch)], axis=1)


def _dispatch_kernel(ztab_ref, site_ref, f_ref, lpos_ref, xs_ref, zbuf, zeros, sems, zsem, *, tm, pitch):
    i = pl.program_id(0)
    n_tiles = pl.num_programs(0)
    slot = lax.rem(i, 2)
    rows = tm * TOP_K

    def tile_bytes(s):
        return pltpu.make_async_copy(zbuf.at[s], xs_ref.at[pl.ds(0, rows * pitch)], sems.at[s])

    @pl.when(i >= 2)
    def _():
        tile_bytes(slot).wait()

    pos = lax.broadcasted_iota(I32, (rows, tm), 0)
    lp = lpos_ref[...]
    sel = pos == lp[0:1, :]
    for k in range(1, TOP_K):
        sel = jnp.logical_or(sel, pos == lp[k:k + 1, :])
    z = _dot(jnp.where(sel, 1.0, 0.0).astype(BF16), f_ref[...])
    _lanes_to_tiles(zbuf.at[slot], z)
    _segment_copies(site_ref, zbuf, slot, xs_ref, sems.at[slot], tm=tm, pitch=pitch, to_grouped=True)

    @pl.when(i == n_tiles - 1)
    def _():
        zeros[...] = jnp.zeros_like(zeros)
        _pad_zero_copies(ztab_ref, zeros, xs_ref, zsem, tm=tm, pitch=pitch, wait=False)
        _pad_zero_copies(ztab_ref, zeros, xs_ref, zsem, tm=tm, pitch=pitch, wait=True)
        half = tm // 2

        def tail(j, start):
            cp = pltpu.make_async_copy(zeros, xs_ref.at[_token_rows(j * half, half, pitch)], zsem)
            cp.start() if start else cp.wait()

        first, last = ztab_ref[2 * N_EXPERTS], xs_ref.shape[0] // (half * pitch)
        lax.fori_loop(first, last, lambda j, c: (tail(j, True), c)[1], 0)
        lax.fori_loop(first, last, lambda j, c: (tail(j, False), c)[1], 0)
        tile_bytes(slot).wait()
        tile_bytes(1 - slot).wait()


def _dispatch_call(ztab, sites, f, lpos, n_rows, *, tm):
    n, d = f.shape
    assert n // tm >= 2 and tm & (tm - 1) == 0
    pitch = d // LANES
    grid_spec = pltpu.PrefetchScalarGridSpec(
        num_scalar_prefetch=1,
        grid=(n // tm,),
        in_specs=[pl.BlockSpec((1, 1, SITE_TABLE_WIDTH), lambda i, z: (i, 0, 0), memory_space=pltpu.SMEM),
                  pl.BlockSpec((tm, d), lambda i, z: (i, 0)),
                  pl.BlockSpec((8, tm), lambda i, z: (i, 0))],
        out_specs=pl.BlockSpec(memory_space=pl.ANY),
        scratch_shapes=[pltpu.VMEM((2, tm * TOP_K * pitch, LANES), F32), pltpu.VMEM((tm // 2 * pitch, LANES), F32),
                        pltpu.SemaphoreType.DMA((2,)), pltpu.SemaphoreType.DMA(())])
    return pl.pallas_call(
        functools.partial(_dispatch_kernel, tm=tm, pitch=pitch),
        grid_spec=grid_spec,
        out_shape=jax.ShapeDtypeStruct((n_rows * pitch, LANES), F32),
        compiler_params=_cparams("arbitrary"),
        name="moe_dispatch",
    )(ztab, sites, f, lpos)


def _expert_kernel(b0_ref, nb_ref, xs_ref, wgu_ref, bgu_ref, wd_ref, bd_ref, y_ref,
                   xbuf, ybuf, wgu_bf, wd_bf, xsem, ysem, *, de, tm, pitch):
    e = pl.program_id(0)
    last_e = pl.num_programs(0) - 1
    nb = nb_ref[e]
    b0 = b0_ref[e]
    n_used = b0_ref[last_e] + nb_ref[last_e]
    rows = tm * pitch

    def block(g):
        return pl.ds(pl.multiple_of(g * rows, rows), rows)

    def x_copy(g, s):
        return pltpu.make_async_copy(xs_ref.at[block(g)], xbuf.at[s], xsem.at[s])

    def y_copy(g, s):
        return pltpu.make_async_copy(ybuf.at[s], y_ref.at[block(g)], ysem.at[s])

    @pl.when(jnp.logical_and(e == 0, n_used > 0))
    def _():
        x_copy(0, 0).start()

    @pl.when(nb > 0)
    def _():
        wgu_bf[...] = wgu_ref[0, 0].astype(BF16)
        wd_bf[...] = wd_ref[0, 0].astype(BF16)

    def body(j, c):
        g = b0 + j
        s = lax.rem(g, 2)
        x_copy(g, s).wait()

        @pl.when(g >= 2)
        def _():
            y_copy(g - 2, s).wait()

        @pl.when(g + 1 < n_used)
        def _():
            x_copy(g + 1, 1 - s).start()

        gu = _dot(_tiles_to_lanes(xbuf.at[s], pitch).astype(BF16), wgu_bf[...]) + bgu_ref[0, 0]
        gate = jnp.minimum(gu[:, 0:de], SWIGLU_LIMIT)
        up = jnp.clip(gu[:, de:2 * de], -SWIGLU_LIMIT, SWIGLU_LIMIT)
        hid = (up + 1.0) * gate * jax.nn.sigmoid(SWIGLU_ALPHA * gate)
        _lanes_to_tiles(ybuf.at[s], _dot(hid.astype(BF16), wd_bf[...]) + bd_ref[0, 0])
        y_copy(g, s).start()
        return c

    lax.fori_loop(0, nb, body, 0)

    @pl.when(e == last_e)
    def _():
        @pl.when(n_used >= 2)
        def _():
            y_copy(n_used - 2, lax.rem(n_used, 2)).wait()

        @pl.when(n_used >= 1)
        def _():
            y_copy(n_used - 1, lax.rem(n_used + 1, 2)).wait()

        ybuf[0] = jnp.zeros(ybuf.shape[1:], F32)

        def tail(g, start):
            cp = pltpu.make_async_copy(ybuf.at[0], y_ref.at[block(g)], ysem.at[0])
            cp.start() if start else cp.wait()

        n_blocks = y_ref.shape[0] // rows
        lax.fori_loop(n_used, n_blocks, lambda g, c: (tail(g, True), c)[1], 0)
        lax.fori_loop(n_used, n_blocks, lambda g, c: (tail(g, False), c)[1], 0)


def _expert_call(blk0, nblk, xs, w_gu, b_gu, w_down, b_down, *, tm, layer):
    nl, ne, d, de2 = w_gu.shape
    pitch = d // LANES
    de = de2 // 2
    grid_spec = pltpu.PrefetchScalarGridSpec(
        num_scalar_prefetch=2,
        grid=(ne,),
        in_specs=[pl.BlockSpec(memory_space=pl.ANY),
                  pl.BlockSpec((1, 1, d, de2), lambda e, b0, nb: (layer, e, 0, 0)),
                  pl.BlockSpec((1, 1, 1, de2), lambda e, b0, nb: (layer, e, 0, 0)),
                  pl.BlockSpec((1, 1, de, d), lambda e, b0, nb: (layer, e, 0, 0)),
                  pl.BlockSpec((1, 1, 1, d), lambda e, b0, nb: (layer, e, 0, 0))],
        out_specs=pl.BlockSpec(memory_space=pl.ANY),
        scratch_shapes=[pltpu.VMEM((2, tm * pitch, LANES), F32), pltpu.VMEM((2, tm * pitch, LANES), F32),
                        pltpu.VMEM((d, de2), BF16), pltpu.VMEM((de, d), BF16),
                        pltpu.SemaphoreType.DMA((2,)), pltpu.SemaphoreType.DMA((2,))])
    return pl.pallas_call(
        functools.partial(_expert_kernel, de=de, tm=tm, pitch=pitch),
        grid_spec=grid_spec,
        out_shape=jax.ShapeDtypeStruct(xs.shape, F32),
        compiler_params=_cparams("arbitrary"),
        name="moe_experts",
    )(blk0, nblk, xs, w_gu, b_gu.reshape(nl, ne, 1, de2), w_down, b_down.reshape(nl, ne, 1, d))


def _combine_kernel(site0_ref, site1_ref, lpos_ref, gate_ref, h_ref, mod_ref, y_ref, o_ref, ybuf, sems,
                    *, d, tm, row_fn):
    i = pl.program_id(0)
    n_tiles = pl.num_programs(0)
    slot = lax.rem(i, 2)
    rows = tm * TOP_K
    pitch = d // LANES

    def fetch(site_ref, s):
        _segment_copies(site_ref, ybuf, s, y_ref, sems.at[s], tm=tm, pitch=pitch, to_grouped=False)

    @pl.when(i == 0)
    def _():
        fetch(site0_ref, 0)

    @pl.when(i + 1 < n_tiles)
    def _():
        fetch(site1_ref, 1 - slot)

    pltpu.make_async_copy(y_ref.at[pl.ds(0, rows * pitch)], ybuf.at[slot], sems.at[slot]).wait()
    yb = _tiles_to_lanes(ybuf.at[slot], pitch).astype(BF16)
    pos = lax.broadcasted_iota(I32, (rows, tm), 0)
    lp = lpos_ref[...]
    gates = gate_ref[...]
    g = jnp.zeros((rows, tm), F32)
    for k in range(TOP_K):
        g = jnp.where(pos == lp[k:k + 1, :], gates[k:k + 1, :], g)
    g_hi, g_lo = _split_bf16(g, 2)
    acc = (lax.dot_general(g_hi, yb, _TN, preferred_element_type=F32)
           + lax.dot_general(g_lo, yb, _TN, preferred_element_type=F32))
    r = row_fn(i)
    o_ref[...] = h_ref[...] + mod_ref[pl.ds(r, 1), 5 * d:6 * d] * acc


def _combine_call(sites, lpos, gates, hn, mod, y_rows, *, tm, row_fn):
    n, d = hn.shape
    pitch = d // LANES
    n_tiles = n // tm
    site_block = (1, 1, SITE_TABLE_WIDTH)
    return pl.pallas_call(
        functools.partial(_combine_kernel, d=d, tm=tm, row_fn=row_fn),
        grid=(n_tiles,),
        in_specs=[pl.BlockSpec(site_block, lambda i: (0, 0, 0), memory_space=pltpu.SMEM),
                  pl.BlockSpec(site_block, lambda i: (jnp.minimum(i + 1, n_tiles - 1), 0, 0), memory_space=pltpu.SMEM),
                  pl.BlockSpec((8, tm), lambda i: (i, 0)),
                  pl.BlockSpec((8, tm), lambda i: (i, 0)),
                  pl.BlockSpec((tm, d), lambda i: (i, 0)),
                  pl.BlockSpec(mod.shape, lambda i: (0, 0)),
                  pl.BlockSpec(memory_space=pl.ANY)],
        out_specs=pl.BlockSpec((tm, d), lambda i: (i, 0)),
        out_shape=jax.ShapeDtypeStruct((n, d), F32),
        scratch_shapes=[pltpu.VMEM((2, tm * TOP_K * pitch, LANES), F32), pltpu.SemaphoreType.DMA((2,))],
        compiler_params=_cparams("arbitrary"),
        name="moe_combine",
    )(sites, sites, lpos, gates, hn, mod, y_rows)


def _moe(f, hn, mod, lpos, gates, ne_t, cb_t, counts, w_gu, b_gu, w_down, b_down, *, tm, row_fn, layer):
    n, _ = f.shape
    n_blocks = -(-n * TOP_K // tm) + N_EXPERTS
    cnt = counts[:, 0].astype(I32)
    padded = (cnt + tm - 1) // tm * tm
    pad_end = jnp.cumsum(padded)
    pad_start = pad_end - padded
    ne = ne_t[:, :, 0].astype(I32)
    cb = cb_t[:, :, 0].astype(I32)
    sites = _site_table(jnp.cumsum(ne, axis=1) - ne, pad_start[None] + cb, ne, tm=tm, pitch=f.shape[1] // LANES)
    n_valid = (pad_end[-1] // tm).astype(I32)
    ztab = jnp.concatenate([pad_start + cnt, padded - cnt, 2 * n_valid[None]])
    xs = _dispatch_call(ztab, sites, f, lpos, n_blocks * tm, tm=tm)
    y_rows = _expert_call(pad_start // tm, padded // tm, xs, w_gu, b_gu, w_down, b_down, tm=tm, layer=layer)
    return _combine_call(sites, lpos, gates, hn, mod, y_rows, tm=tm, row_fn=row_fn)


def _att_in_kernel(h_ref, mod_ref, gain_ref, w_ref, qg_gain, kg_gain, dq_gain, dk_gain, bd_ref,
                   cg_ref, sg_ref, cd_ref, sd_ref,
                   qg_ref, kg_ref, vg_ref, qd_ref, kd_ref, vd_ref, *, d, tpb, nct, nb):
    r = _unified_row(pl.program_id(0), tpb, nct, nb)
    a = _norm_mod(h_ref[...], gain_ref[...], mod_ref[pl.ds(r, 1), 0:d], mod_ref[pl.ds(r, 1), d:2 * d])
    p = _dot(a.astype(BF16), w_ref[...])
    cg, sg = cg_ref[...], sg_ref[...]

    def head_norm_rope(x, gain):
        y = x * lax.rsqrt(jnp.mean(x * x, axis=-1, keepdims=True) + NORM_EPS) * gain
        return _rope(y, cg, sg, GQA_HD).astype(BF16)

    def group_norm_rope(x, gain):
        ss = sum(_dot(piece, bd_ref[...]) for piece in _split_bf16(x * x, 2))
        y = x * lax.rsqrt(ss * (1.0 / DIFF_HD) + NORM_EPS) * gain
        return _rope(y, cd_ref[...], sd_ref[...], DIFF_HD).astype(BF16)

    c0 = 0
    for hh in range(GQA_HEADS):
        qg_ref[:, hh * GQA_HD:(hh + 1) * GQA_HD] = head_norm_rope(p[:, c0:c0 + GQA_HD], qg_gain[...])
        c0 += GQA_HD
    for hh in range(GQA_KV_HEADS):
        kg_ref[:, hh * GQA_HD:(hh + 1) * GQA_HD] = head_norm_rope(p[:, c0:c0 + GQA_HD], kg_gain[...])
        c0 += GQA_HD
    wv = GQA_KV_HEADS * GQA_HD
    vg_ref[...] = p[:, c0:c0 + wv].astype(BF16)
    c0 += wv
    qd_ref[...] = group_norm_rope(p[:, c0:c0 + HALF], dq_gain[...])
    c0 += HALF
    kd_ref[...] = group_norm_rope(p[:, c0:c0 + HALF], dk_gain[...])
    c0 += HALF
    vd_ref[...] = p[:, c0:c0 + HALF].astype(BF16)


def _att_in_call(h, mod, gain, w, qg_gain, kg_gain, dq_gain, dk_gain, bd, cg, sg, cd, sd, *, tm, tpb, nct, nb):
    nt, d = h.shape
    const = lambda a: pl.BlockSpec(a.shape, lambda i: (0,) * a.ndim)
    tab = lambda w_: pl.BlockSpec((tm, w_), lambda i: (lax.rem(i, tpb), 0))
    out = lambda w_: pl.BlockSpec((tm, w_), lambda i: (i, 0))
    shp = lambda w_: jax.ShapeDtypeStruct((nt, w_), BF16)
    wkv = GQA_KV_HEADS * GQA_HD
    return pl.pallas_call(
        functools.partial(_att_in_kernel, d=d, tpb=tpb, nct=nct, nb=nb),
        grid=(nt // tm,),
        in_specs=[pl.BlockSpec((tm, d), lambda i: (i, 0)), const(mod), const(gain), const(w),
                  const(qg_gain), const(kg_gain), const(dq_gain), const(dk_gain), const(bd),
                  tab(GQA_HD), tab(GQA_HD), tab(HALF), tab(HALF)],
        out_specs=[out(HALF), out(wkv), out(wkv), out(HALF), out(HALF), out(HALF)],
        out_shape=[shp(HALF), shp(wkv), shp(wkv), shp(HALF), shp(HALF), shp(HALF)],
        compiler_params=_cparams("arbitrary"),
        name="att_in_proj",
    )(h, mod, gain, w, qg_gain, kg_gain, dq_gain, dk_gain, bd, cg, sg, cd, sd)


def _with_ones(v):
    return jnp.concatenate([v, jnp.ones_like(v)], axis=1)


def _attend(q, k, v_ones):
    s = lax.dot_general(q, k, _NT, preferred_element_type=F32)
    e = jnp.exp2(s - jnp.max(s, axis=-1, keepdims=True))
    r = _dot(e.astype(BF16), v_ones)
    hd = v_ones.shape[1] // 2
    return r[:, :hd], 1.0 / r[:, hd:hd + 1]


def _gqa_kernel(q_ref, k_ref, v_ref, o_ref, *, group):
    v_ones = [_with_ones(v_ref[:, g * GQA_HD:(g + 1) * GQA_HD]) for g in range(GQA_HEADS // group)]
    for h in range(GQA_HEADS):
        sl = slice(h * GQA_HD, (h + 1) * GQA_HD)
        kv = slice(h // group * GQA_HD, (h // group + 1) * GQA_HD)
        o, inv = _attend(q_ref[:, sl], k_ref[:, kv], v_ones[h // group])
        o_ref[:, sl] = (o * inv).astype(BF16)


def _gqa_call(qg, kg, vg, *, nb, tq, tot, nqt, lat0):
    group = GQA_HEADS // GQA_KV_HEADS
    upb = tot // tq
    return pl.pallas_call(
        functools.partial(_gqa_kernel, group=group),
        grid=(nb, nqt),
        in_specs=[pl.BlockSpec((tq, GQA_HEADS * GQA_HD), lambda b, q: (b * upb + lat0 + q, 0)),
                  pl.BlockSpec((tot, GQA_KV_HEADS * GQA_HD), lambda b, q: (b, 0)),
                  pl.BlockSpec((tot, GQA_KV_HEADS * GQA_HD), lambda b, q: (b, 0))],
        out_specs=pl.BlockSpec((tq, GQA_HEADS * GQA_HD), lambda b, q: (b * nqt + q, 0)),
        out_shape=jax.ShapeDtypeStruct((nb * nqt * tq, HALF), BF16),
        compiler_params=_cparams("arbitrary", "arbitrary"),
        name="gqa_attention",
    )(qg, kg, vg)


def _diff_kernel(q_ref, k_ref, v_ref, lam_ref, gain_ref, o_ref, *, lam_init):
    lp = lam_ref[...]
    lam = (jnp.exp(jnp.sum(lp[0:1] * lp[1:2], axis=-1, keepdims=True))
           - jnp.exp(jnp.sum(lp[2:3] * lp[3:4], axis=-1, keepdims=True)) + lam_init)
    hd2 = 2 * DIFF_HD
    for j in range(q_ref.shape[1] // hd2):
        sl = slice(j * hd2, (j + 1) * hd2)
        q = q_ref[:, sl]
        k = k_ref[:, sl]
        lane = lax.broadcasted_iota(I32, q.shape, 1)
        zero = jnp.zeros_like(q)
        v_ones = _with_ones(v_ref[:, sl])
        o1, inv1 = _attend(jnp.where(lane < DIFF_HD, q, zero), k, v_ones)
        o2, inv2 = _attend(jnp.where(lane >= DIFF_HD, q, zero), k, v_ones)
        o = o1 * inv1 - o2 * (lam * inv2)
        y = o * lax.rsqrt(jnp.mean(o * o, axis=-1, keepdims=True) + NORM_EPS) * gain_ref[...]
        o_ref[:, sl] = (y * (1.0 - lam_init)).astype(BF16)


def _diff_call(qd, kd, vd, lam_params, gain, *, nb, tq, tot, nqt, lat0, lam_init):
    upb = tot // tq
    hd2 = DIFF_HEADS * (2 * DIFF_HD)
    return pl.pallas_call(
        functools.partial(_diff_kernel, lam_init=lam_init),
        grid=(nb, HALF // hd2, nqt),
        in_specs=[pl.BlockSpec((tq, hd2), lambda b, h, q: (b * upb + lat0 + q, h)),
                  pl.BlockSpec((tot, hd2), lambda b, h, q: (b, h)),
                  pl.BlockSpec((tot, hd2), lambda b, h, q: (b, h)),
                  pl.BlockSpec(lam_params.shape, lambda b, h, q: (0, 0)),
                  pl.BlockSpec(gain.shape, lambda b, h, q: (0, 0))],
        out_specs=pl.BlockSpec((tq, hd2), lambda b, h, q: (b * nqt + q, h)),
        out_shape=jax.ShapeDtypeStruct((nb * nqt * tq, HALF), BF16),
        compiler_params=_cparams("arbitrary", "arbitrary", "arbitrary"),
        name="diff_attention",
    )(qd, kd, vd, lam_params, gain)


def _deinterleave(hd):
    return np.concatenate([np.arange(0, hd, 2), np.arange(1, hd, 2)])


def _head_perm(n_heads, hd):
    return np.concatenate([h * hd + _deinterleave(hd) for h in range(n_heads)])


def _rope_tables(n_lat, n_ctx, hd, reps):
    t = np.arange(n_lat)
    row, col = (t // GRID_W).astype(np.float64), (t % GRID_W).astype(np.float64)
    axis_dim = hd // 2
    inv_freq = ROPE_THETA ** (-np.arange(0, axis_dim, 2, dtype=np.float64) / axis_dim)
    ang = np.concatenate([row[:, None] * inv_freq, col[:, None] * inv_freq], axis=-1)
    cos = np.concatenate([np.ones((n_ctx, hd // 2)), np.cos(ang)], axis=0)
    sin = np.concatenate([np.zeros((n_ctx, hd // 2)), np.sin(ang)], axis=0)
    c = np.tile(np.concatenate([cos, cos], axis=1), (1, reps))
    s = np.tile(np.concatenate([-sin, sin], axis=1), (1, reps))
    return jnp.asarray(c, F32), jnp.asarray(s, F32)


def _retention_tables(n):
    gam = 1.0 - 2.0 ** (-5.0 - np.arange(RET_HEADS, dtype=np.float64))
    gams = (gam, gam[::-1])
    c = np.arange(n, dtype=np.float64)
    diff = c[:, None] - c[None, :]
    dmat = np.zeros((2, RET_HEADS, n, n))
    qs = np.zeros((2, n, RET_HEADS * RET_DK))
    ks = np.zeros((2, n, RET_HEADS * RET_DK))
    gpow = np.zeros((2, 8, RET_HEADS * RET_DK))
    for d in range(2):
        for h in range(RET_HEADS):
            g = gams[d][h]
            sl = slice(h * RET_DK, (h + 1) * RET_DK)
            if d == 0:
                dmat[d, h] = np.where(diff >= 0, g ** np.maximum(diff, 0), 0.0)
                qs[d, :, sl] = (g ** (c + 1))[:, None]
                ks[d, :, sl] = (g ** (n - 1 - c))[:, None]
            else:
                dmat[d, h] = np.where(diff <= 0, g ** np.maximum(-diff, 0), 0.0)
                qs[d, :, sl] = (g ** (n - c))[:, None]
                ks[d, :, sl] = (g ** c)[:, None]
            gpow[d, :, sl] = g ** n
    return tuple(jnp.asarray(a, F32) for a in (dmat, qs, ks, gpow))


def _plan(nb, n_lat, n_ctx):
    tm = 256 if (n_ctx % 256 == 0 and n_lat % 256 == 0) else 128
    assert n_ctx % tm == 0 and n_lat % tm == 0 and tm % CHUNK == 0 and n_lat % GRID_W == 0
    tq = tm
    return tm, tq


def kernel(x, c, ctx, c_ctx, norm_mix, norm_ffn, w_ada, b_ada, rec_w_in, rec_lb_logits, rec_w_out, rec_hg_gain,
           rec_ret_gain, att_w_in, att_w_out, att_q_gain, att_k_gain, diff_q_gain, diff_k_gain, diff_lambda,
           diff_gain, w_router, b_router, w_gu, b_gu, w_down, b_down):
    nb, n_lat, d = x.shape
    n_ctx = ctx.shape[1]
    assert w_ada.shape[0] == 2, "two layers: recurrent mixer then attention mixer"
    tm, tq = _plan(nb, n_lat, n_ctx)
    tot = n_ctx + n_lat
    tpb, nct = tot // tm, n_ctx // tm
    nt = nb * tot
    nql = n_lat // tm

    unified_row = functools.partial(_unified_row, tpb=tpb, nct=nct, nb=nb)
    latent_row = lambda i: lax.div(i, nql)
    latent_block = lambda i: lax.div(i, nql) * tpb + nct + lax.rem(i, nql)

    n_cond = -(-(nb + 1) // 8) * 8
    cond = jnp.concatenate([c, c_ctx[None], jnp.zeros((n_cond - nb - 1, d), F32)], axis=0)
    mod = _ada_call(cond, w_ada, b_ada)

    wr = jnp.pad(jnp.swapaxes(w_router, 1, 2), ((0, 0), (0, E_PAD - N_EXPERTS), (0, 0)))
    br = b_router[:, :, None]

    is_context = lambda i: lax.rem(i, tpb) < nct
    stream0 = [(x.reshape(nb * n_lat, d), lambda i: lax.div(i, tpb) * nql + jnp.maximum(lax.rem(i, tpb) - nct, 0)),
               (ctx.reshape(nb * n_ctx, d), lambda i: lax.div(i, tpb) * nct + jnp.minimum(lax.rem(i, tpb), nct - 1))]

    lb = jnp.cumsum(jax.nn.softmax(rec_lb_logits.astype(F32), axis=0), axis=0)[0]
    w_in = rec_w_in[0]
    c_rq = HG_HEADS * HG_DK * 3 + HG_HEADS * HG_DV * 2
    nrk = RET_HEADS * RET_DK
    perm = _head_perm(RET_HEADS, RET_DK)
    w_in = jnp.concatenate([w_in[:, :c_rq], w_in[:, c_rq + perm], w_in[:, c_rq + nrk + perm] * (RET_DK ** -0.5),
                            w_in[:, c_rq + 2 * nrk:]], axis=1).astype(BF16)
    cs_r, sn_r = _rope_tables(n_lat, n_ctx, RET_DK, 2 * RET_HEADS)
    dmat, qs, ks, g64 = _retention_tables(tm)
    p0 = _inproj_call(stream0, mod[0], norm_mix[0][None], w_in, cs_r, sn_r, tm=tm, tpb=tpb, nct=nct, nb=nb)
    ohf, orf, ohb, orb = _rec_call(p0, lb, dmat, qs, ks, g64, nb=nb, tc=tm, tpb=tpb, nct=nct)
    mixer = [(ohf, 0), (ohb, 0), (orf, 0), (orb, 0), (p0, 4), (p0, 7),
             (rec_hg_gain[0][None], None), (rec_ret_gain[0][None], None)]
    hn, f, *routing = _post_call(
        mixer, stream0, mod[0], rec_w_out[0].astype(BF16), norm_ffn[0][None], wr[0], br[0],
        tm=tm, n_tiles=nt // tm, row_fn=unified_row, is_context=is_context)
    h = _moe(f, hn, mod[0], *routing, w_gu, b_gu, w_down, b_down, tm=tm, row_fn=unified_row, layer=0)

    pg, pd = _head_perm(GQA_HEADS, GQA_HD), _head_perm(2 * DIFF_HEADS, DIFF_HD)
    w_in = att_w_in[0]
    o_gk = GQA_HEADS * GQA_HD
    o_gv = o_gk + GQA_KV_HEADS * GQA_HD
    o_dq = o_gv + GQA_KV_HEADS * GQA_HD
    o_dk = o_dq + HALF
    o_dv = o_dk + HALF
    w_in = jnp.concatenate([w_in[:, pg], w_in[:, o_gk + pg[:GQA_KV_HEADS * GQA_HD]], w_in[:, o_gv:o_dq],
                            w_in[:, o_dq + pd], w_in[:, o_dk + pd], w_in[:, o_dv:]], axis=1).astype(BF16)
    dg, dd = _deinterleave(GQA_HD), _deinterleave(DIFF_HD)
    cg, sg = _rope_tables(n_lat, n_ctx, GQA_HD, 1)
    cd, sd = _rope_tables(n_lat, n_ctx, DIFF_HD, 2 * DIFF_HEADS)
    bd = jnp.asarray(np.kron(np.eye(2 * DIFF_HEADS), np.ones((DIFF_HD, DIFF_HD))), BF16)
    log2e = math.log2(math.e)
    qg, kg, vg, qd, kd, vd = _att_in_call(
        h, mod[1], norm_mix[1][None], w_in,
        (att_q_gain[0][dg] * (GQA_HD ** -0.5 * log2e))[None], att_k_gain[0][dg][None],
        jnp.tile(diff_q_gain[0][dd] * (DIFF_HD ** -0.5 * log2e), 2 * DIFF_HEADS)[None],
        jnp.tile(diff_k_gain[0][dd], 2 * DIFF_HEADS)[None],
        bd, cg, sg, cd, sd, tm=tm, tpb=tpb, nct=nct, nb=nb)
    nqt = n_lat // tq
    yg = _gqa_call(qg, kg, vg, nb=nb, tq=tq, tot=tot, nqt=nqt, lat0=n_ctx // tq)
    lam_init = 0.8 - 0.6 * math.exp(-0.3 * 1)
    yd = _diff_call(qd, kd, vd, diff_lambda[0], diff_gain[0][None], nb=nb, tq=tq, tot=tot, nqt=nqt,
                    lat0=n_ctx // tq, lam_init=lam_init)
    hn, f, *routing = _post_call(
        [(yg, 0), (yd, 0)], [(h, latent_block)], mod[1], att_w_out[0].astype(BF16), norm_ffn[1][None], wr[1], br[1],
        tm=tm, n_tiles=nb * nql, row_fn=latent_row)
    out = _moe(f, hn, mod[1], *routing, w_gu, b_gu, w_down, b_down, tm=tm, row_fn=latent_row, layer=1)
    return out.reshape(nb, n_lat, d)
```
